```python
import math
import jax
import jax.numpy as jnp
from jax import lax
import numpy as np

D_MODEL = 1024
BATCH = 16
SEQ = 2048
DEPTH = 2

GRID_W = 64
CTX_LEN = 256
EPS = 1e-6
ROPE_THETA = 10000.0
Q_BLOCK = 128

S5_WIDTH = D_MODEL // 2
S5_GROUP = 16
S5_GROUPS = S5_WIDTH // S5_GROUP
S5_STATE = 64
MLA_HEADS = 4
MLA_NOPE = 128
MLA_ROPE = 64
MLA_V = 128
MLA_Q_RANK = 3 * D_MODEL // 8
MLA_KV_RANK = D_MODEL // 4
RET_HEADS = 4
RET_DK = 128
RET_DV = 128
RET_CHUNK = 128
GQA_HEADS = 4
GQA_KV_HEADS = 2
GQA_HEAD_DIM = 128
FFN_HIDDEN = ((8 * D_MODEL // 3 + 127) // 128) * 128
CONV_W = 3

AB_SIZES = (S5_WIDTH, MLA_Q_RANK, MLA_KV_RANK, MLA_ROPE)
AB_IN = S5_WIDTH + MLA_Q_RANK + MLA_KV_RANK + MLA_ROPE
AB_OUT = S5_WIDTH + MLA_HEADS * MLA_V
CD_SIZES = (RET_HEADS * RET_DK, RET_HEADS * RET_DK, RET_HEADS * RET_DV, RET_HEADS * RET_DV,
            GQA_HEADS * GQA_HEAD_DIM, GQA_KV_HEADS * GQA_HEAD_DIM, GQA_KV_HEADS * GQA_HEAD_DIM)
CD_IN = 2 * RET_HEADS * RET_DK + 2 * RET_HEADS * RET_DV + (GQA_HEADS + 2 * GQA_KV_HEADS) * GQA_HEAD_DIM
CD_OUT = RET_HEADS * RET_DV + GQA_HEADS * GQA_HEAD_DIM

kernel_name = 'hybrid_s5_mla_retention_gqa_prefix_dit'


def rmsnorm(x):
    x32 = x.astype(jnp.float32)
    y = x32 * lax.rsqrt(jnp.mean(x32 * x32, axis=-1, keepdims=True) + EPS)
    return y.astype(x.dtype)


def modulate(h, shift, scale):
    return h * (1.0 + scale) + shift


def heads(t, n_heads):
    return t.reshape(t.shape[:2] + (n_heads, -1))


def split_cols(z, sizes):
    cuts, acc = [], 0
    for s in sizes[:-1]:
        acc += s
        cuts.append(acc)
    return jnp.split(z, cuts, axis=-1)


def rope_tables(rows, cols, dim):
    quarter = dim // 4
    inv = ROPE_THETA ** (-jnp.arange(quarter, dtype=jnp.float32) / quarter)
    ang_r = rows.astype(jnp.float32)[:, None] * inv
    ang_c = cols.astype(jnp.float32)[:, None] * inv
    ang = jnp.concatenate([ang_r, ang_r, ang_c, ang_c], axis=-1)
    return jnp.cos(ang), jnp.sin(ang)


def apply_rope(x, cos, sin):
    r1, r2, c1, c2 = jnp.split(x, 4, axis=-1)
    rot = jnp.concatenate([-r2, r1, -c2, c1], axis=-1)
    return (x * cos[:, None, :] + rot * sin[:, None, :]).astype(x.dtype)


def attend(q, k, v):
    s = jnp.einsum('bqgrd,bkgd->bgrqk', q, k).astype(jnp.float32) * (q.shape[-1] ** -0.5)
    p = jax.nn.softmax(s, axis=-1).astype(v.dtype)
    o = jnp.einsum('bgrqk,bkge->bqgre', p, v)
    return o.reshape(o.shape[:2] + (-1,))


def attend_blocked(q, k, v):
    b, n = q.shape[:2]
    qb = jnp.swapaxes(q.reshape((b, n // Q_BLOCK, Q_BLOCK) + q.shape[2:]), 0, 1)
    o = lax.map(lambda qi: attend(qi, k, v), qb)
    return jnp.swapaxes(o, 0, 1).reshape(b, n, -1)


def zoh_power(ar, ai, k):
    mag = jnp.exp(ar * k)
    return mag * jnp.cos(ai * k), mag * jnp.sin(ai * k)


def s5_discretise(lam_re, lam_im, log_step, b_re, b_im):
    lam_re, lam_im = lam_re.astype(jnp.float32), lam_im.astype(jnp.float32)
    dt = jnp.exp(log_step.astype(jnp.float32))[:, None]
    ar, ai = lam_re * dt, lam_im * dt
    lb_re, lb_im = zoh_power(ar, ai, 1.0)
    den = lam_re * lam_re + lam_im * lam_im
    cf_re = ((lb_re - 1.0) * lam_re + lb_im * lam_im) / den
    cf_im = (lb_im * lam_re - (lb_re - 1.0) * lam_im) / den
    b_re, b_im = b_re.astype(jnp.float32), b_im.astype(jnp.float32)
    bb_re = cf_re[..., None] * b_re - cf_im[..., None] * b_im
    bb_im = cf_re[..., None] * b_im + cf_im[..., None] * b_re
    return ar, ai, bb_re, bb_im


def complex_affine_combine(e1, e2):
    a1r, a1i, b1r, b1i = e1
    a2r, a2i, b2r, b2i = e2
    return (a2r * a1r - a2i * a1i, a2r * a1i + a2i * a1r,
            a2r * b1r - a2i * b1i + b2r, a2r * b1i + a2i * b1r + b2i)


def s5_scan(u, disc, s0):
    ar, ai, bb_re, bb_im = disc
    n = u.shape[1]
    bu_re = jnp.einsum('gph,bngh->bngp', bb_re, u)
    bu_im = jnp.einsum('gph,bngh->bngp', bb_im, u)
    lb_re, lb_im = zoh_power(ar, ai, 1.0)
    a_re = jnp.broadcast_to(lb_re, (1, n) + lb_re.shape)
    a_im = jnp.broadcast_to(lb_im, (1, n) + lb_im.shape)
    _, _, s_re, s_im = lax.associative_scan(complex_affine_combine, (a_re, a_im, bu_re, bu_im), axis=1)
    if s0 is not None:
        k = jnp.arange(1, n + 1, dtype=jnp.float32)[:, None, None]
        p_re, p_im = zoh_power(ar, ai, k)
        s0_re, s0_im = s0[0][:, None], s0[1][:, None]
        s_re, s_im = (s_re + p_re * s0_re - p_im * s0_im,
                      s_im + p_re * s0_im + p_im * s0_re)
    return s_re, s_im


def s5_readout(c_re, c_im, s_re, s_im):
    return (jnp.einsum('ghp,bngp->bngh', c_re.astype(jnp.float32), s_re)
            - jnp.einsum('ghp,bngp->bngh', c_im.astype(jnp.float32), s_im))


def s5_mixer(uc, ux, lam_re, lam_im, log_step, b_re, b_im, c_re, c_im, d_skip, w_glu, b_glu, ctx_out):
    dtype = ux.dtype
    uc32 = uc.astype(jnp.float32).reshape(uc.shape[:2] + (S5_GROUPS, S5_GROUP))
    ux32 = ux.astype(jnp.float32).reshape(ux.shape[:2] + (S5_GROUPS, S5_GROUP))
    d32 = d_skip.astype(jnp.float32)
    yx = d32 * ux32
    yc = d32 * uc32 if ctx_out else None
    for d in range(2):
        flip = (lambda t: jnp.flip(t, axis=1)) if d == 1 else (lambda t: t)
        disc = s5_discretise(lam_re[d], lam_im[d], log_step[d], b_re[d], b_im[d])
        sc_re, sc_im = s5_scan(flip(uc32), disc, None)
        sx_re, sx_im = s5_scan(flip(ux32), disc, (sc_re[:, -1], sc_im[:, -1]))
        yx = yx + flip(s5_readout(c_re[d], c_im[d], sx_re, sx_im))
        if ctx_out:
            yc = yc + flip(s5_readout(c_re[d], c_im[d], sc_re, sc_im))

    def glu(y):
        g = jax.nn.gelu(y.reshape(y.shape[:2] + (S5_WIDTH,)))
        return (g * jax.nn.sigmoid(g @ w_glu.astype(jnp.float32) + b_glu.astype(jnp.float32))).astype(dtype)

    return (glu(yc) if ctx_out else None), glu(yx)


def mla_q(cq, g_q, w_uq, rope):
    q = heads((rmsnorm(cq) * g_q) @ w_uq, MLA_HEADS)
    q_nope, q_rope = q[..., :MLA_NOPE], q[..., MLA_NOPE:]
    if rope is not None:
        q_rope = apply_rope(q_rope, *rope)
    return jnp.concatenate([q_nope, q_rope], axis=-1)[:, :, :, None, :]


def mla_kv(ckv, kr, g_kv, w_ukv, rope):
    kv = heads((rmsnorm(ckv) * g_kv) @ w_ukv, MLA_HEADS)
    k_nope, v = kv[..., :MLA_NOPE], kv[..., MLA_NOPE:]
    kr = kr[:, :, None, :]
    if rope is not None:
        kr = apply_rope(kr, *rope)
    kr = jnp.broadcast_to(kr, k_nope.shape[:3] + (MLA_ROPE,))
    return jnp.concatenate([k_nope, kr], axis=-1), v


def mixer_ab(cn, xn, rope, w_in, w_out, lam_re, lam_im, log_step, b_re, b_im, c_re, c_im, d_skip,
             w_glu, b_glu, g_q, w_uq, g_kv, w_ukv, ctx_out):
    uc, cqc, ckvc, krc = split_cols(cn @ w_in, AB_SIZES)
    ux, cqx, ckvx, krx = split_cols(xn @ w_in, AB_SIZES)
    s5_c, s5_x = s5_mixer(uc, ux, lam_re, lam_im, log_step, b_re, b_im, c_re, c_im, d_skip,
                          w_glu, b_glu, ctx_out)
    kc, vc = mla_kv(ckvc, krc, g_kv, w_ukv, None)
    kx, vx = mla_kv(ckvx, krx, g_kv, w_ukv, rope)
    att_x = attend_blocked(mla_q(cqx, g_q, w_uq, rope),
                           jnp.concatenate([kc, kx], axis=1), jnp.concatenate([vc, vx], axis=1))
    yx = jnp.concatenate([s5_x, att_x], axis=-1) @ w_out
    if not ctx_out:
        return None, yx
    att_c = attend(mla_q(cqc, g_q, w_uq, None), kc, vc)
    return jnp.concatenate([s5_c, att_c], axis=-1) @ w_out, yx


def retention_context_states(k, v, lg):
    L = k.shape[1]
    m = jnp.arange(L, dtype=jnp.float32)[:, None]
    w_f = jnp.exp((L - 1.0 - m) * lg[0])
    w_b = jnp.exp(m * lg[1])
    s_f = jnp.einsum('bmhk,bmhv,mh->bhkv', k, v, w_f)
    s_b = jnp.einsum('bmhk,bmhv,mh->bhkv', k, v, w_b)
    return s_f, s_b


def retention_chunkwise(q, k, v, lg, s0):
    b, n, h, _ = q.shape
    c = RET_CHUNK
    nc = n // c
    i = jnp.arange(c, dtype=jnp.float32)
    dist = i[:, None] - i[None, :]
    inner_decay = jnp.where(dist >= 0, jnp.exp(jnp.maximum(dist, 0.0)[None] * lg[:, None, None]), 0.0)
    q_decay = jnp.exp((i[:, None] + 1.0) * lg)
    k_decay = jnp.exp((c - 1.0 - i)[:, None] * lg)
    chunk_decay = jnp.exp(c * lg)

    def chunks(t):
        return jnp.swapaxes(t.reshape(b, nc, c, h, t.shape[-1]), 0, 1)

    def step(s, qkv):
        qc, kc, vc = qkv
        att = jnp.einsum('bihk,bjhk->bhij', qc, kc) * inner_decay
        o = (jnp.einsum('bhij,bjhv->bihv', att, vc)
             + jnp.einsum('bihk,bhkv->bihv', qc, s) * q_decay[:, :, None])
        s = chunk_decay[:, None, None] * s + jnp.einsum('bjhk,bjhv,jh->bhkv', kc, vc, k_decay)
        return s, o

    _, o = lax.scan(step, s0, (chunks(q), chunks(k), chunks(v)))
    return jnp.swapaxes(o, 0, 1).reshape(b, n, h, -1)


def retention_latent(q, k, v, s_f, s_b, lg):
    fwd = retention_chunkwise(q, k, v, lg[0], s_f)
    bwd = retention_chunkwise(jnp.flip(q, 1), jnp.flip(k, 1), jnp.flip(v, 1), lg[1], s_b)
    return fwd + jnp.flip(bwd, 1)


def retention_context(q, k, v, lg):
    L = q.shape[1]
    m = jnp.arange(L, dtype=jnp.float32)
    dist = m[:, None] - m[None, :]
    dec = (jnp.where(dist >= 0, jnp.exp(jnp.maximum(dist, 0.0)[None] * lg[0][:, None, None]), 0.0)
           + jnp.where(dist <= 0, jnp.exp(jnp.maximum(-dist, 0.0)[None] * lg[1][:, None, None]), 0.0))
    att = jnp.einsum('bihk,bjhk->bhij', q, k) * dec
    return jnp.einsum('bhij,bjhv->bihv', att, v)


def retention_output(o, gate):
    b, n = o.shape[:2]
    return rmsnorm(o).reshape(b, n, -1) * jax.nn.silu(gate.astype(jnp.float32))


def mixer_cd(cn, xn, rope, w_in, w_out, decay_logit, g_q, g_k, ctx_out):
    dtype = xn.dtype
    rq_c, rk_c, rv_c, rg_c, gq_c, gk_c, gv_c = split_cols(cn @ w_in, CD_SIZES)
    rq_x, rk_x, rv_x, rg_x, gq_x, gk_x, gv_x = split_cols(xn @ w_in, CD_SIZES)
    lg = jax.nn.log_sigmoid(decay_logit.astype(jnp.float32))
    k_scale = RET_DK ** -0.5

    def f32(t):
        return t.astype(jnp.float32)

    kc_r = f32(heads(rk_c, RET_HEADS)) * k_scale
    vc_r = f32(heads(rv_c, RET_HEADS))
    s_f, s_b = retention_context_states(kc_r, vc_r, lg)
    qx_r = f32(apply_rope(heads(rq_x, RET_HEADS), *rope))
    kx_r = f32(apply_rope(heads(rk_x, RET_HEADS), *rope)) * k_scale
    vx_r = f32(heads(rv_x, RET_HEADS))
    ret_x = retention_output(retention_latent(qx_r, kx_r, vx_r, s_f, s_b, lg), rg_x)

    def gqa_q(t, rot):
        q = rmsnorm(heads(t, GQA_HEADS)) * g_q
        if rot is not None:
            q = apply_rope(q, *rot)
        return q.reshape(q.shape[:2] + (GQA_KV_HEADS, GQA_HEADS // GQA_KV_HEADS, GQA_HEAD_DIM))

    def gqa_k(t, rot):
        k = rmsnorm(heads(t, GQA_KV_HEADS)) * g_k
        return apply_rope(k, *rot) if rot is not None else k

    kc_g, vc_g = gqa_k(gk_c, None), heads(gv_c, GQA_KV_HEADS)
    att_x = attend_blocked(gqa_q(gq_x, rope),
                           jnp.concatenate([kc_g, gqa_k(gk_x, rope)], axis=1),
                           jnp.concatenate([vc_g, heads(gv_x, GQA_KV_HEADS)], axis=1))
    yx = jnp.concatenate([ret_x.astype(dtype), att_x], axis=-1) @ w_out
    if not ctx_out:
        return None, yx
    qc_r = f32(heads(rq_c, RET_HEADS))
    ret_c = retention_output(retention_context(qc_r, kc_r, vc_r, lg), rg_c)
    att_c = attend(gqa_q(gq_c, None), kc_g, vc_g)
    return jnp.concatenate([ret_c.astype(dtype), att_c], axis=-1) @ w_out, yx


def conv_ffn(h, w_up, conv_w, conv_b, w_down):
    a, g = jnp.split(h @ w_up, 2, axis=-1)
    n = a.shape[1]
    pad = CONV_W // 2
    ap = jnp.pad(a, ((0, 0), (pad, pad), (0, 0)))
    conv = conv_b
    for j in range(CONV_W):
        conv = conv + ap[:, j:j + n] * conv_w[j]
    return (jax.nn.gelu(conv) * g) @ w_down


def setup_inputs(seed: int = 0) -> dict:
    key = jax.random.key(seed)
    ks = iter(jax.random.split(key, 48))

    def nrm(shape, scale):
        return jax.random.normal(next(ks), shape, jnp.float32) * scale

    ne, no = (DEPTH + 1) // 2, DEPTH // 2
    D, F = D_MODEL, FFN_HIDDEN
    gamma0 = 1.0 - 2.0 ** (-5.0 - jnp.arange(RET_HEADS, dtype=jnp.float32))
    decay_logit0 = jnp.log(gamma0) - jnp.log1p(-gamma0)
    return {
        'x': nrm((BATCH, SEQ, D), 1.0),
        'c': nrm((BATCH, D), 1.0),
        'ctx': nrm((BATCH, CTX_LEN, D), 1.0),
        'c_ctx': nrm((D,), 1.0),
        'w_mod': nrm((DEPTH, D, 6 * D), 0.5 * D ** -0.5),
        'b_mod': nrm((DEPTH, 6 * D), 0.02),
        'w_in_ab': nrm((ne, D, AB_IN), D ** -0.5),
        'w_out_ab': nrm((ne, AB_OUT, D), AB_OUT ** -0.5),
        's5_lam_re': -0.5 + nrm((ne, 2, S5_GROUPS, S5_STATE), 0.01),
        's5_lam_im': math.pi * jnp.arange(S5_STATE, dtype=jnp.float32) + nrm((ne, 2, S5_GROUPS, S5_STATE), 0.01),
        's5_log_step': jax.random.uniform(next(ks), (ne, 2, S5_GROUPS), jnp.float32,
                                          math.log(1e-3), math.log(1e-1)),
        's5_b_re': nrm((ne, 2, S5_GROUPS, S5_STATE, S5_GROUP), (2 * S5_GROUP) ** -0.5),
        's5_b_im': nrm((ne, 2, S5_GROUPS, S5_STATE, S5_GROUP), (2 * S5_GROUP) ** -0.5),
        's5_c_re': nrm((ne, 2, S5_GROUPS, S5_GROUP, S5_STATE), S5_STATE ** -0.5),
        's5_c_im': nrm((ne, 2, S5_GROUPS, S5_GROUP, S5_STATE), S5_STATE ** -0.5),
        's5_d': nrm((ne, S5_GROUPS, S5_GROUP), 1.0),
        's5_w_glu': nrm((ne, S5_WIDTH, S5_WIDTH), S5_WIDTH ** -0.5),
        's5_b_glu': nrm((ne, S5_WIDTH), 0.02),
        'mla_g_q': 1.0 + nrm((ne, MLA_Q_RANK), 0.02),
        'mla_w_uq': nrm((ne, MLA_Q_RANK, MLA_HEADS * (MLA_NOPE + MLA_ROPE)), MLA_Q_RANK ** -0.5),
        'mla_g_kv': 1.0 + nrm((ne, MLA_KV_RANK), 0.02),
        'mla_w_ukv': nrm((ne, MLA_KV_RANK, MLA_HEADS * (MLA_NOPE + MLA_V)), MLA_KV_RANK ** -0.5),
        'w_in_cd': nrm((no, D, CD_IN), D ** -0.5),
        'w_out_cd': nrm((no, CD_OUT, D), CD_OUT ** -0.5),
        'ret_decay_logit': decay_logit0 + nrm((no, 2, RET_HEADS), 0.05),
        'gqa_g_q': 1.0 + nrm((no, GQA_HEAD_DIM), 0.02),
        'gqa_g_k': 1.0 + nrm((no, GQA_HEAD_DIM), 0.02),
        'ffn_w_up': nrm((DEPTH, D, 2 * F), D ** -0.5),
        'ffn_conv_w': nrm((DEPTH, CONV_W, F), CONV_W ** -0.5),
        'ffn_conv_b': nrm((DEPTH, F), 0.02),
        'ffn_w_down': nrm((DEPTH, F, D), F ** -0.5),
        'g_final': 1.0 + nrm((D,), 0.02),
    }


def reference(x, c, ctx, c_ctx, w_mod, b_mod, w_in_ab, w_out_ab, s5_lam_re, s5_lam_im, s5_log_step,
              s5_b_re, s5_b_im, s5_c_re, s5_c_im, s5_d, s5_w_glu, s5_b_glu, mla_g_q, mla_w_uq,
              mla_g_kv, mla_w_ukv, w_in_cd, w_out_cd, ret_decay_logit, gqa_g_q, gqa_g_k,
              ffn_w_up, ffn_conv_w, ffn_conv_b, ffn_w_down, g_final):
    n = x.shape[1]
    ROWS = n // GRID_W
    rows = jnp.repeat(jnp.arange(ROWS, dtype=jnp.int32), GRID_W)
    cols = jnp.arange(n, dtype=jnp.int32) % GRID_W
    rope64 = rope_tables(rows, cols, MLA_ROPE)
    rope128 = rope_tables(rows, cols, RET_DK)

    cond_x = jax.nn.silu(c)
    cond_c = jax.nn.silu(c_ctx)
    hx, hc = x, ctx
    for layer in range(DEPTH):
        last = layer == DEPTH - 1
        i = layer // 2
        mx = [m[:, None, :] for m in jnp.split(cond_x @ w_mod[layer] + b_mod[layer], 6, axis=-1)]
        mc = jnp.split(cond_c @ w_mod[layer] + b_mod[layer], 6, axis=-1)
        xn = modulate(rmsnorm(hx), mx[0], mx[1])
        cn = modulate(rmsnorm(hc), mc[0], mc[1])
        if layer % 2 == 0:
            yc, yx = mixer_ab(cn, xn, rope64, w_in_ab[i], w_out_ab[i], s5_lam_re[i], s5_lam_im[i],
                              s5_log_step[i], s5_b_re[i], s5_b_im[i], s5_c_re[i], s5_c_im[i], s5_d[i],
                              s5_w_glu[i], s5_b_glu[i], mla_g_q[i], mla_w_uq[i], mla_g_kv[i],
                              mla_w_ukv[i], not last)
        else:
            yc, yx = mixer_cd(cn, xn, rope128, w_in_cd[i], w_out_cd[i], ret_decay_logit[i],
                              gqa_g_q[i], gqa_g_k[i], not last)
        hx = hx + mx[2] * yx
        xn = modulate(rmsnorm(hx), mx[3], mx[4])
        hx = hx + mx[5] * conv_ffn(xn, ffn_w_up[layer], ffn_conv_w[layer], ffn_conv_b[layer], ffn_w_down[layer])
        if not last:
            hc = hc + mc[2] * yc
            cn = modulate(rmsnorm(hc), mc[3], mc[4])
            hc = hc + mc[5] * conv_ffn(cn, ffn_w_up[layer], ffn_conv_w[layer], ffn_conv_b[layer], ffn_w_down[layer])
    return rmsnorm(hx) * g_final
```

```python
import functools
import math

import numpy as np
import jax
import jax.numpy as jnp
from jax import lax
from jax.experimental import pallas as pl
from jax.experimental.pallas import tpu as pltpu

F32 = jnp.float32
BF16 = jnp.bfloat16

D_MODEL = 1024
GRID_W = 64
CTX_LEN = 256
EPS = 1e-6
ROPE_THETA = 10000.0

S5_WIDTH = 512
S5_GROUP = 16
S5_GROUPS = 32
S5_STATE = 64
S5_CH = S5_GROUPS * S5_STATE
S5_COLS = 4
MLA_HEADS = 4
MLA_NOPE = 128
MLA_ROPE = 64
MLA_V = 128
MLA_Q_RANK = 384
MLA_KV_RANK = 256
MLA_QPAD = 256
RET_HEADS = 4
RET_D = 128
RET_CHUNK = 256
GQA_HEADS = 4
GQA_KV_HEADS = 2
GQA_D = 128
FFN_HIDDEN = 2816
FFN_CHUNK = 256
CONV_W = 3

TM = 256
HALO = 16
S5_LC = 16
VMEM_LIMIT = 56 * 1024 * 1024


def _cparams(sem):
    return pltpu.CompilerParams(dimension_semantics=sem, vmem_limit_bytes=VMEM_LIMIT)


def _rms(x):
    return x * lax.rsqrt(jnp.mean(x * x, axis=-1, keepdims=True) + EPS)


def _dot(a, b):
    return jnp.dot(a, b, preferred_element_type=F32)


def _rope(x, cos, sa, sb, quarter):
    return x * cos + pltpu.roll(x, quarter, 1) * sa + pltpu.roll(x, 128 - quarter, 1) * sb


def _rope_tables(n_lat, dim):
    quarter = dim // 4
    inv = (ROPE_THETA ** (-np.arange(quarter, dtype=np.float32) / quarter)).astype(np.float32)
    t = np.arange(n_lat)
    ang_r = (t // GRID_W).astype(np.float32)[:, None] * inv
    ang_c = (t % GRID_W).astype(np.float32)[:, None] * inv
    ang = np.concatenate([ang_r, ang_r, ang_c, ang_c], axis=-1).astype(np.float32)
    cos, sin = np.cos(ang), np.sin(ang)
    lane_q = (np.arange(dim) // quarter) % 2
    nt = CTX_LEN + n_lat
    cos_t = np.ones((nt, 128), np.float32)
    sa_t = np.zeros((nt, 128), np.float32)
    sb_t = np.zeros((nt, 128), np.float32)
    cos_t[CTX_LEN:, :dim] = cos
    sa_t[CTX_LEN:, :dim] = np.where(lane_q == 1, sin, 0.0)
    sb_t[CTX_LEN:, :dim] = np.where(lane_q == 0, -sin, 0.0)
    return jnp.asarray(cos_t), jnp.asarray(sa_t), jnp.asarray(sb_t)


def _mod_kernel(c_ref, w_ref, b_ref, o_ref):
    c = c_ref[...]
    s = c * jax.nn.sigmoid(c)
    o_ref[0] = _dot(s.astype(BF16), w_ref[0].astype(BF16)) + b_ref[0]


def _modulation(cond, w_mod, b_mod):
    depth, d, n6 = w_mod.shape
    rows = cond.shape[0]
    tn = 1024
    return pl.pallas_call(
        _mod_kernel,
        grid=(depth, n6 // tn),
        in_specs=[pl.BlockSpec((rows, d), lambda l, j: (0, 0)),
                  pl.BlockSpec((1, d, tn), lambda l, j: (l, 0, j)),
                  pl.BlockSpec((1, 1, tn), lambda l, j: (l, 0, j))],
        out_specs=pl.BlockSpec((1, rows, tn), lambda l, j: (l, 0, j)),
        out_shape=jax.ShapeDtypeStruct((depth, rows, n6), F32),
        compiler_params=_cparams(("arbitrary", "arbitrary")),
        name="modulation",
    )(cond, w_mod, b_mod.reshape(depth, 1, n6))


def _proj_ab_kernel(h_ref, m_ref, win_ref, gq_ref, wuq_ref, gkv_ref, wkn_ref, wv_ref,
                    cos_ref, sa_ref, sb_ref, u_ref, q_ref, k_ref, v_ref):
    m = m_ref[0, 0]
    xn = _rms(h_ref[0]) * (1.0 + m[1:2]) + m[0:1]
    z = _dot(xn.astype(BF16), win_ref[...])
    u_ref[...] = z[:, :S5_WIDTH]
    cq = z[:, 512:896]
    ckv = z[:, 896:1152]
    kr = z[:, 1152:1280]
    cos, sa, sb = cos_ref[...], sa_ref[...], sb_ref[...]
    q = _dot((_rms(cq) * gq_ref[...]).astype(BF16), wuq_ref[...])
    ckvn = (_rms(ckv) * gkv_ref[...]).astype(BF16)
    kn = _dot(ckvn, wkn_ref[...])
    krr = _rope(kr, cos, sa, sb, MLA_ROPE // 4)
    qparts, kparts = [], []
    for h in range(MLA_HEADS):
        o = MLA_QPAD * h
        qparts += [q[:, o:o + 128], _rope(q[:, o + 128:o + 256], cos, sa, sb, MLA_ROPE // 4)]
        kparts += [kn[:, 128 * h:128 * (h + 1)], krr]
    q_ref[0] = jnp.concatenate(qparts, axis=-1).astype(BF16)
    k_ref[0] = jnp.concatenate(kparts, axis=-1).astype(BF16)
    v_ref[0] = _dot(ckvn, wv_ref[...]).astype(BF16)


def _proj_ab(h, msel, win, gq, wuq, gkv, wkn, wv, tabs):
    b, nt, d = h.shape
    const = lambda bi, t: (0, 0)
    row = lambda bi, t: (t, 0)
    tok = lambda bi, t: (bi, t, 0)
    return pl.pallas_call(
        _proj_ab_kernel,
        grid=(b, nt // TM),
        in_specs=[pl.BlockSpec((1, TM, d), tok),
                  pl.BlockSpec((1, 1, 6, d), lambda bi, t: (bi, jnp.minimum(t, 1), 0, 0)),
                  pl.BlockSpec(win.shape, const),
                  pl.BlockSpec(gq.shape, const),
                  pl.BlockSpec(wuq.shape, const),
                  pl.BlockSpec(gkv.shape, const),
                  pl.BlockSpec(wkn.shape, const),
                  pl.BlockSpec(wv.shape, const),
                  pl.BlockSpec((TM, 128), row),
                  pl.BlockSpec((TM, 128), row),
                  pl.BlockSpec((TM, 128), row)],
        out_specs=[pl.BlockSpec((TM, S5_WIDTH), lambda bi, t: (t, bi)),
                   pl.BlockSpec((1, TM, MLA_HEADS * MLA_QPAD), tok),
                   pl.BlockSpec((1, TM, MLA_HEADS * MLA_QPAD), tok),
                   pl.BlockSpec((1, TM, MLA_HEADS * MLA_V), tok)],
        out_shape=[jax.ShapeDtypeStruct((nt, b * S5_WIDTH), F32),
                   jax.ShapeDtypeStruct((b, nt, MLA_HEADS * MLA_QPAD), BF16),
                   jax.ShapeDtypeStruct((b, nt, MLA_HEADS * MLA_QPAD), BF16),
                   jax.ShapeDtypeStruct((b, nt, MLA_HEADS * MLA_V), BF16)],
        compiler_params=_cparams(("parallel", "parallel")),
        name="proj_ab",
    )(h, msel, win, gq, wuq, gkv, wkn, wv, *tabs)


def _s5_prep_kernel(lre_ref, lim_ref, ls_ref, bre_ref, bim_ref, ore_ref, oim_ref, obre_ref, obim_ref):
    lre, lim = lre_ref[...], lim_ref[...]
    dt = jnp.exp(ls_ref[...])
    ar, ai = lre * dt, lim * dt
    mag = jnp.exp(ar)
    lb_re, lb_im = mag * jnp.cos(ai), mag * jnp.sin(ai)
    den = lre * lre + lim * lim
    cf_re = ((lb_re - 1.0) * lre + lb_im * lim) / den
    cf_im = (lb_im * lre - (lb_re - 1.0) * lim) / den
    ore_ref[...] = lb_re
    oim_ref[...] = lb_im
    bre, bim = bre_ref[...], bim_ref[...]
    obre_ref[...] = cf_re * bre - cf_im * bim
    obim_ref[...] = cf_re * bim + cf_im * bre


def _s5_prep(lam_re, lam_im, log_step, b_re, b_im):
    g2 = 2 * S5_GROUPS
    args = (lam_re.reshape(g2, 1, S5_STATE), lam_im.reshape(g2, 1, S5_STATE), log_step.reshape(g2, 1, 1),
            jnp.swapaxes(b_re, -1, -2).reshape(g2, S5_GROUP, S5_STATE),
            jnp.swapaxes(b_im, -1, -2).reshape(g2, S5_GROUP, S5_STATE))
    full = lambda a: pl.BlockSpec(a.shape, lambda i: (0,) * a.ndim)
    lam_sds = jax.ShapeDtypeStruct((g2, 1, S5_STATE), F32)
    bb_sds = jax.ShapeDtypeStruct((g2, S5_GROUP, S5_STATE), F32)
    return pl.pallas_call(
        _s5_prep_kernel,
        grid=(1,),
        in_specs=[full(a) for a in args],
        out_specs=[full(lam_sds), full(lam_sds), full(bb_sds), full(bb_sds)],
        out_shape=[lam_sds, lam_sds, bb_sds, bb_sds],
        name="s5_prep",
    )(*args)


def _block_diag8(blocks):
    eye = jnp.eye(8, dtype=blocks.dtype)
    out = blocks[..., :, :, None, :] * eye[:, None, :, None]
    r, c = blocks.shape[-2:]
    return out.reshape(blocks.shape[:-3] + (8 * r, 8 * c))


def _s5_chunk(u_ref, bcat_ref, lam_ref, cre_ref, cim_ref, st_ref, bre_ref, bim_ref, lc, nb, reverse):
    u = u_ref[...].astype(BF16)
    for c in range(S5_COLS):
        bu = _dot(u[:, 128 * c:128 * (c + 1)], bcat_ref[c])
        bre_ref[:, 512 * c:512 * (c + 1)] = bu[:, :512]
        bim_ref[:, 512 * c:512 * (c + 1)] = bu[:, 512:]
    order = range(lc - 1, -1, -1) if reverse else range(lc)
    for c in range(S5_COLS):
        cs = slice(512 * c, 512 * (c + 1))
        lre = jnp.broadcast_to(lam_ref[0:1, cs], (nb, 512))
        lim = jnp.broadcast_to(lam_ref[1:2, cs], (nb, 512))
        sre, sim = st_ref[0, :, cs], st_ref[1, :, cs]
        for t in order:
            rs = slice(t * nb, (t + 1) * nb)
            nre = lre * sre - lim * sim + bre_ref[rs, cs]
            nim = lre * sim + lim * sre + bim_ref[rs, cs]
            bre_ref[rs, cs] = nre
            bim_ref[rs, cs] = nim
            sre, sim = nre, nim
        st_ref[0, :, cs] = sre
        st_ref[1, :, cs] = sim
    ys = []
    for c in range(S5_COLS):
        cs = slice(512 * c, 512 * (c + 1))
        ys.append(_dot(bre_ref[:, cs].astype(BF16), cre_ref[c]) - _dot(bim_ref[:, cs].astype(BF16), cim_ref[c]))
    return jnp.concatenate(ys, axis=-1)


def _s5_fwd_kernel(u_ref, bcat_ref, lam_ref, cre_ref, cim_ref, y_ref, st_ref, bre_ref, bim_ref, *, lc, nb):
    @pl.when(pl.program_id(0) == 0)
    def _():
        st_ref[...] = jnp.zeros(st_ref.shape, F32)

    y_ref[...] = _s5_chunk(u_ref, bcat_ref, lam_ref, cre_ref, cim_ref, st_ref, bre_ref, bim_ref, lc, nb, False)


def _s5_bwd_kernel(u_ref, yf_ref, bcat_ref, lam_ref, cre_ref, cim_ref, d_ref, wglu_ref, bglu_ref, o_ref,
                   st_ref, bre_ref, bim_ref, *, lc, nb):
    @pl.when(pl.program_id(0) == 0)
    def _():
        st_ref[...] = jnp.zeros(st_ref.shape, F32)

    yb = _s5_chunk(u_ref, bcat_ref, lam_ref, cre_ref, cim_ref, st_ref, bre_ref, bim_ref, lc, nb, True)
    y = d_ref[...] * u_ref[...] + yf_ref[...] + yb
    g = jax.nn.gelu(y)
    o_ref[...] = (g * jax.nn.sigmoid(_dot(g.astype(BF16), wglu_ref[...]) + bglu_ref[...])).astype(BF16)


def _s5_mixer(u_tm, nb, bcat, lam, cre, cim, d_skip, w_glu, b_glu):
    rows_total = u_tm.shape[0]
    lc = S5_LC
    rows = lc * nb
    n_steps = rows_total // rows
    n_ctx = CTX_LEN // lc
    scratch =[pltpu.VMEM((2, nb, S5_CH), F32), pltpu.VMEM((rows, S5_CH), F32), pltpu.VMEM((rows, S5_CH), F32)]
    fwd_blk = pl.BlockSpec((rows, S5_WIDTH), lambda i: (i, 0))
    yf = pl.pallas_call(
        functools.partial(_s5_fwd_kernel, lc=lc, nb=nb),
        grid=(n_steps,),
        in_specs=[fwd_blk] + [pl.BlockSpec((None,) + a.shape[1:], lambda i, n=a.ndim: (0,) * n)
                              for a in (bcat, lam, cre, cim)],
        out_specs=fwd_blk,
        out_shape=jax.ShapeDtypeStruct((rows_total, S5_WIDTH), F32),
        scratch_shapes=scratch,
        compiler_params=_cparams(("arbitrary",)),
        name="s5_fwd",
    )(u_tm, bcat, lam, cre, cim)

    def chunk(i):
        return jnp.where(i < n_ctx, n_ctx - 1 - i, n_steps - 1 - (i - n_ctx))

    bwd_blk = pl.BlockSpec((rows, S5_WIDTH), lambda i: (chunk(i), 0))
    whole = lambda a: pl.BlockSpec(a.shape, lambda i, n=a.ndim: (0,) * n)
    return pl.pallas_call(
        functools.partial(_s5_bwd_kernel, lc=lc, nb=nb),
        grid=(n_steps,),
        in_specs=[bwd_blk, bwd_blk] + [pl.BlockSpec((None,) + a.shape[1:], lambda i, n=a.ndim: (1,) + (0,) * (n - 1))
                                       for a in (bcat, lam, cre, cim)]
                 + [whole(d_skip), whole(w_glu), whole(b_glu)],
        out_specs=bwd_blk,
        out_shape=jax.ShapeDtypeStruct((rows_total, S5_WIDTH), BF16),
        scratch_shapes=scratch,
        compiler_params=_cparams(("arbitrary",)),
        name="s5_bwd_glu",
    )(u_tm, yf, bcat, lam, cre, cim, d_skip, w_glu, b_glu)


def _attn_kernel(q_ref, k_ref, v_ref, o_ref, *, scale, n_ctx_keys, ctx_tile):
    def run(nk):
        q = q_ref[0]
        k = k_ref[0, :nk]
        v = v_ref[0, :nk]
        s = lax.dot_general(q, k, (((1,), (1,)), ((), ())), preferred_element_type=F32) * scale
        p = jnp.exp(s - jnp.max(s, axis=-1, keepdims=True))
        l = jnp.sum(p, axis=-1, keepdims=True)
        o_ref[0] = (_dot(p.astype(BF16), v) / l).astype(o_ref.dtype)

    nk_all = k_ref.shape[1]
    if ctx_tile:
        qi = pl.program_id(2)

        @pl.when(qi == 0)
        def _():
            run(n_ctx_keys)

        @pl.when(qi > 0)
        def _():
            run(nk_all)
    else:
        run(nk_all)


def _attention(q, k, v, *, heads, kv_heads, dq, dv, scale, q_tile0, n_q_tiles, out_rows):
    b, nt, _ = q.shape
    rep = heads // kv_heads
    return pl.pallas_call(
        functools.partial(_attn_kernel, scale=scale, n_ctx_keys=CTX_LEN, ctx_tile=(q_tile0 == 0)),
        grid=(b, heads, n_q_tiles),
        in_specs=[pl.BlockSpec((1, TM, dq), lambda bi, h, qi: (bi, qi + q_tile0, h)),
                  pl.BlockSpec((1, nt, dq), lambda bi, h, qi: (bi, 0, h // rep)),
                  pl.BlockSpec((1, nt, dv), lambda bi, h, qi: (bi, 0, h // rep))],
        out_specs=pl.BlockSpec((1, TM, dv), lambda bi, h, qi: (bi, qi, h)),
        out_shape=jax.ShapeDtypeStruct((b, out_rows, heads * dv), BF16),
        compiler_params=_cparams(("parallel", "parallel", "arbitrary")),
        name="attention",
    )(q, k, v)


def _oproj_kernel(h_ref, a_ref, b_ref, m_ref, wa_ref, wb_ref, h1_ref, xn_ref, *, a_time_major):
    a = a_ref[...] if a_time_major else a_ref[0]
    y = _dot(a, wa_ref[...]) + _dot(b_ref[0], wb_ref[...])
    m = m_ref[0, 0]
    h1 = h_ref[0] + m[2:3] * y
    h1_ref[0] = h1
    xn_ref[0] = (_rms(h1) * (1.0 + m[4:5]) + m[3:4]).astype(BF16)


def _oproj(h, a, bmix, msel, wa, wb, *, a_time_major, h_tile0, n_tiles, seg_of_tile):
    b, _, d = h.shape
    rows = n_tiles * TM
    tok = lambda bi, t: (bi, t, 0)
    const = lambda bi, t: (0, 0)
    if a_time_major:
        a_spec = pl.BlockSpec((TM, a.shape[1] // b), lambda bi, t: (t, bi))
    else:
        a_spec = pl.BlockSpec((1, TM, a.shape[2]), tok)
    return pl.pallas_call(
        functools.partial(_oproj_kernel, a_time_major=a_time_major),
        grid=(b, n_tiles),
        in_specs=[pl.BlockSpec((1, TM, d), lambda bi, t: (bi, t + h_tile0, 0)),
                  a_spec,
                  pl.BlockSpec((1, TM, bmix.shape[2]), tok),
                  pl.BlockSpec((1, 1, 6, d), lambda bi, t: (bi, seg_of_tile(t), 0, 0)),
                  pl.BlockSpec(wa.shape, const),
                  pl.BlockSpec(wb.shape, const)],
        out_specs=[pl.BlockSpec((1, TM, d), tok), pl.BlockSpec((1, TM, d), tok)],
        out_shape=[jax.ShapeDtypeStruct((b, rows, d), F32), jax.ShapeDtypeStruct((b, rows, d), BF16)],
        compiler_params=_cparams(("parallel", "parallel")),
        name="oproj",
    )(h, a, bmix, msel, wa, wb)


def _ffn_kernel(h_ref, xn_ref, xp_ref, xq_ref, m_ref, wup_ref, cw_ref, wdn_ref, gfin_ref, o_ref,
                xe_ref, acc_ref, *, first_tiles, last_tiles, final):
    t = pl.program_id(1)
    has_prev = functools.reduce(jnp.logical_and, [t != ft for ft in first_tiles])
    has_next = functools.reduce(jnp.logical_and, [t != lt for lt in last_tiles])
    xe_ref[0:TM] = xn_ref[0]
    xe_ref[TM:TM + HALO] = xp_ref[0]
    xe_ref[TM + HALO:TM + 2 * HALO] = xq_ref[0]
    xe = xe_ref[...]
    x = xe[:TM]
    row = lax.broadcasted_iota(jnp.int32, (TM, FFN_CHUNK), 0)
    n_chunks = wup_ref.shape[0]
    for j in range(n_chunks):
        ae = _dot(xe, wup_ref[j, :, :FFN_CHUNK])
        g = _dot(x, wup_ref[j, :, FFN_CHUNK:])
        a = ae[:TM]
        a_prev = jnp.where(has_prev, ae[TM + HALO - 1:TM + HALO], 0.0)
        a_next = jnp.where(has_next, ae[TM + HALO:TM + HALO + 1], 0.0)
        a_dn = jnp.where(row == 0, a_prev, pltpu.roll(a, 1, 0))
        a_up = jnp.where(row == TM - 1, a_next, pltpu.roll(a, TM - 1, 0))
        cw = cw_ref[j]
        conv = cw[3:4] + a_dn * cw[0:1] + a * cw[1:2] + a_up * cw[2:3]
        hid = (jax.nn.gelu(conv) * g).astype(BF16)
        part = _dot(hid, wdn_ref[j])
        if j == 0:
            acc_ref[...] = part
        else:
            acc_ref[...] += part
    out = h_ref[0] + m_ref[0, 0][5:6] * acc_ref[...]
    if final:
        out = _rms(out) * gfin_ref[...]
    o_ref[0] = out


def _ffn(h1, xn2, msel, wup, cw, wdn, gfin, *, first_tiles, last_tiles, seg_of_tile, final):
    b, rows, d = h1.shape
    n_tiles = rows // TM
    hb = TM // HALO
    tok = lambda bi, t: (bi, t, 0)
    c3 = lambda bi, t: (0, 0, 0)
    single = dict(pipeline_mode=pl.Buffered(1))
    return pl.pallas_call(
        functools.partial(_ffn_kernel, first_tiles=first_tiles, last_tiles=last_tiles, final=final),
        grid=(b, n_tiles),
        in_specs=[pl.BlockSpec((1, TM, d), tok),
                  pl.BlockSpec((1, TM, d), tok),
                  pl.BlockSpec((1, HALO, d), lambda bi, t: (bi, jnp.maximum(t * hb - 1, 0), 0)),
                  pl.BlockSpec((1, HALO, d), lambda bi, t: (bi, jnp.minimum((t + 1) * hb, rows // HALO - 1), 0)),
                  pl.BlockSpec((1, 1, 6, d), lambda bi, t: (bi, seg_of_tile(t), 0, 0)),
                  pl.BlockSpec(wup.shape, c3, **single),
                  pl.BlockSpec(cw.shape, c3, **single),
                  pl.BlockSpec(wdn.shape, c3, **single),
                  pl.BlockSpec(gfin.shape, lambda bi, t: (0, 0))],
        out_specs=pl.BlockSpec((1, TM, d), tok),
        out_shape=jax.ShapeDtypeStruct((b, rows, d), F32),
        scratch_shapes=[pltpu.VMEM((TM + 2 * HALO, d), BF16), pltpu.VMEM((TM, d), F32)],
        compiler_params=_cparams(("parallel", "parallel")),
        name="conv_ffn",
    )(h1, xn2, xn2, xn2, msel, wup, cw, wdn, gfin)


def _proj_cd_kernel(h_ref, m_ref, win_ref, ggq_ref, ggk_ref, cos_ref, sa_ref, sb_ref,
                    rq_ref, rk_ref, rv_ref, rg_ref, gq_ref, gk_ref, gv_ref):
    m = m_ref[0, 0]
    xn = _rms(h_ref[0]) * (1.0 + m[1:2]) + m[0:1]
    z = _dot(xn.astype(BF16), win_ref[...])
    cos, sa, sb = cos_ref[...], sa_ref[...], sb_ref[...]
    rope = lambda x: _rope(x, cos, sa, sb, RET_D // 4)
    head = lambda base, h: z[:, base + 128 * h:base + 128 * (h + 1)]
    k_scale = RET_D ** -0.5
    rq_ref[0] = jnp.concatenate([rope(head(0, h)) for h in range(RET_HEADS)], -1).astype(BF16)
    rk_ref[0] = jnp.concatenate([rope(head(512, h)) * k_scale for h in range(RET_HEADS)], -1).astype(BF16)
    rv_ref[0] = z[:, 1024:1536].astype(BF16)
    rg_ref[0] = z[:, 1536:2048]
    gq_ref[0] = jnp.concatenate([rope(_rms(head(2048, h)) * ggq_ref[...]) for h in range(GQA_HEADS)],
                                -1).astype(BF16)
    gk_ref[0] = jnp.concatenate([rope(_rms(head(2560, h)) * ggk_ref[...]) for h in range(GQA_KV_HEADS)],
                                -1).astype(BF16)
    gv_ref[0] = z[:, 2816:3072].astype(BF16)


def _proj_cd(h, msel, win, ggq, ggk, tabs):
    b, nt, d = h.shape
    const = lambda bi, t: (0, 0)
    row = lambda bi, t: (t, 0)
    tok = lambda bi, t: (bi, t, 0)
    widths = (512, 512, 512, 512, 512, 256, 256)
    dtypes = (BF16, BF16, BF16, F32, BF16, BF16, BF16)
    return pl.pallas_call(
        _proj_cd_kernel,
        grid=(b, nt // TM),
        in_specs=[pl.BlockSpec((1, TM, d), tok),
                  pl.BlockSpec((1, 1, 6, d), lambda bi, t: (bi, jnp.minimum(t, 1), 0, 0)),
                  pl.BlockSpec(win.shape, const),
                  pl.BlockSpec(ggq.shape, const),
                  pl.BlockSpec(ggk.shape, const),
                  pl.BlockSpec((TM, 128), row),
                  pl.BlockSpec((TM, 128), row),
                  pl.BlockSpec((TM, 128), row)],
        out_specs=[pl.BlockSpec((1, TM, w), tok) for w in widths],
        out_shape=[jax.ShapeDtypeStruct((b, nt, w), dt) for w, dt in zip(widths, dtypes)],
        compiler_params=_cparams(("parallel", "parallel")),
        name="proj_cd",
    )(h, msel, win, ggq, ggk, *tabs)


def _retention_kernel(q_ref, k_ref, v_ref, g_ref, dl_ref, o_ref, *, n_lat):
    c = RET_CHUNK
    nc = n_lat // c
    lg = jax.nn.log_sigmoid(dl_ref[0])
    lgf, lgb = lg[0:1], lg[1:2]
    tdot = lambda a, b: lax.dot_general(a, b, (((0,), (0,)), ((), ())), preferred_element_type=F32)
    k_all, v_all = k_ref[0], v_ref[0]

    mc = lax.broadcasted_iota(jnp.int32, (CTX_LEN, RET_D), 0).astype(F32)
    kc = k_all[:CTX_LEN].astype(F32)
    vc = v_all[:CTX_LEN]
    s_f = tdot((kc * jnp.exp((CTX_LEN - 1.0 - mc) * lgf)).astype(BF16), vc)
    s_b = tdot((kc * jnp.exp(mc * lgb)).astype(BF16), vc)

    i = lax.broadcasted_iota(jnp.int32, (c, RET_D), 0).astype(F32)
    kd_f = jnp.exp((c - 1.0 - i) * lgf)
    kd_b = jnp.exp(i * lgb)
    qd_f = jnp.exp((i + 1.0) * lgf)
    qd_b = jnp.exp((c - i) * lgb)
    cd_f = jnp.exp(c * lgf)
    cd_b = jnp.exp(c * lgb)
    ii = lax.broadcasted_iota(jnp.int32, (c, c), 0)
    jj = lax.broadcasted_iota(jnp.int32, (c, c), 1)
    dist = (ii - jj).astype(F32)
    lgf_c = jnp.concatenate([lgf] * (c // RET_D), axis=-1)
    lgb_c = jnp.concatenate([lgb] * (c // RET_D), axis=-1)
    dec = (jnp.where(ii >= jj, jnp.exp(jnp.maximum(dist, 0.0) * lgf_c), 0.0)
           + jnp.where(ii <= jj, jnp.exp(jnp.maximum(-dist, 0.0) * lgb_c), 0.0))

    ks = [k_all[CTX_LEN + c * n:CTX_LEN + c * (n + 1)] for n in range(nc)]
    vs = [v_all[CTX_LEN + c * n:CTX_LEN + c * (n + 1)] for n in range(nc)]
    sf = [s_f]
    for n in range(nc - 1):
        sf.append(cd_f * sf[n] + tdot((ks[n].astype(F32) * kd_f).astype(BF16), vs[n]))
    sb = [None] * nc
    sb[nc - 1] = s_b
    for n in range(nc - 1, 0, -1):
        sb[n - 1] = cd_b * sb[n] + tdot((ks[n].astype(F32) * kd_b).astype(BF16), vs[n])

    for n in range(nc):
        rs = slice(CTX_LEN + c * n, CTX_LEN + c * (n + 1))
        qn = q_ref[0, rs]
        att = lax.dot_general(qn, ks[n], (((1,), (1,)), ((), ())), preferred_element_type=F32) * dec
        o = (_dot(att.astype(BF16), vs[n])
             + _dot(qn, sf[n].astype(BF16)) * qd_f
             + _dot(qn, sb[n].astype(BF16)) * qd_b)
        gate = g_ref[0, rs]
        o_ref[0, c * n:c * (n + 1)] = (_rms(o) * (gate * jax.nn.sigmoid(gate))).astype(BF16)


def _retention(rq, rk, rv, rg, decay_logit, n_lat):
    b, nt, _ = rq.shape
    dl = jnp.broadcast_to(jnp.swapaxes(decay_logit, 0, 1)[:, :, None], (RET_HEADS, 2, 128))
    blk = pl.BlockSpec((1, nt, RET_D), lambda bi, h: (bi, 0, h))
    return pl.pallas_call(
        functools.partial(_retention_kernel, n_lat=n_lat),
        grid=(b, RET_HEADS),
        in_specs=[blk, blk, blk, blk, pl.BlockSpec((1, 2, 128), lambda bi, h: (h, 0, 0))],
        out_specs=pl.BlockSpec((1, n_lat, RET_D), lambda bi, h: (bi, 0, h)),
        out_shape=jax.ShapeDtypeStruct((b, n_lat, RET_HEADS * RET_D), BF16),
        compiler_params=_cparams(("parallel", "parallel")),
        name="retention",
    )(rq, rk, rv, rg, dl)


def _ffn_weights(w_up, conv_w, conv_b, w_down):
    d, f2 = w_up.shape
    f = f2 // 2
    nf = f // FFN_CHUNK
    wa = w_up[:, :f].reshape(d, nf, FFN_CHUNK)
    wg = w_up[:, f:].reshape(d, nf, FFN_CHUNK)
    wup = jnp.transpose(jnp.concatenate([wa, wg], axis=-1), (1, 0, 2)).astype(BF16)
    cw = jnp.concatenate([conv_w, conv_b[None], jnp.zeros((4, f), F32)], axis=0)
    cw = jnp.transpose(cw.reshape(8, nf, FFN_CHUNK), (1, 0, 2))
    wdn = w_down.reshape(nf, FFN_CHUNK, d).astype(BF16)
    return wup, cw, wdn


def kernel(x, c, ctx, c_ctx, w_mod, b_mod, w_in_ab, w_out_ab, s5_lam_re, s5_lam_im, s5_log_step, s5_b_re, s5_b_im, s5_c_re, s5_c_im, s5_d, s5_w_glu, s5_b_glu, mla_g_q, mla_w_uq, mla_g_kv, mla_w_ukv, w_in_cd, w_out_cd, ret_decay_logit, gqa_g_q, gqa_g_k, ffn_w_up, ffn_conv_w, ffn_conv_b, ffn_w_down, g_final):
    b, n_lat, d = x.shape
    nt = CTX_LEN + n_lat
    n_tiles = nt // TM

    rows = ((b + 1 + 7) // 8) * 8
    cond = jnp.concatenate([c, c_ctx[None], jnp.zeros((rows - b - 1, d), F32)], axis=0)
    mods = _modulation(cond, w_mod, b_mod).reshape(w_mod.shape[0], rows, 6, d)

    def mod_select(layer):
        mctx = jnp.broadcast_to(mods[layer, b][None], (b, 6, d))
        return jnp.stack([mctx, mods[layer, :b]], axis=1)

    h = jnp.concatenate([ctx, x], axis=1)

    msel = mod_select(0)
    win = jnp.pad(w_in_ab[0], ((0, 0), (0, 64))).astype(BF16)
    wuq = jnp.pad(mla_w_uq[0].reshape(MLA_Q_RANK, MLA_HEADS, MLA_NOPE + MLA_ROPE),
                  ((0, 0), (0, 0), (0, MLA_QPAD - MLA_NOPE - MLA_ROPE))).reshape(MLA_Q_RANK, -1).astype(BF16)
    wukv = mla_w_ukv[0].reshape(MLA_KV_RANK, MLA_HEADS, MLA_NOPE + MLA_V)
    wkn = wukv[:, :, :MLA_NOPE].reshape(MLA_KV_RANK, -1).astype(BF16)
    wv = wukv[:, :, MLA_NOPE:].reshape(MLA_KV_RANK, -1).astype(BF16)
    tabs64 = _rope_tables(n_lat, MLA_ROPE)
    u_tm, q, k, v = _proj_ab(h, msel, win, mla_g_q[0][None], wuq, mla_g_kv[0][None], wkn, wv, tabs64)

    lam_re, lam_im, bbt_re, bbt_im = _s5_prep(s5_lam_re[0], s5_lam_im[0], s5_log_step[0], s5_b_re[0], s5_b_im[0])
    lam = jnp.stack([lam_re.reshape(2, S5_CH), lam_im.reshape(2, S5_CH)], axis=1)
    bshape = (2, S5_COLS, 8, S5_GROUP, S5_STATE)
    bcat = jnp.concatenate([_block_diag8(bbt_re.reshape(bshape)), _block_diag8(bbt_im.reshape(bshape))],
                           axis=-1).astype(BF16)
    cshape = (2, S5_COLS, 8, S5_GROUP, S5_STATE)
    cre = _block_diag8(jnp.swapaxes(s5_c_re[0].reshape(cshape), -1, -2)).astype(BF16)
    cim = _block_diag8(jnp.swapaxes(s5_c_im[0].reshape(cshape), -1, -2)).astype(BF16)
    s5_out = _s5_mixer(u_tm.reshape(nt * b, S5_WIDTH), b, bcat, lam, cre, cim,
                       s5_d[0].reshape(1, S5_WIDTH), s5_w_glu[0].astype(BF16), s5_b_glu[0][None])
    s5_out = s5_out.reshape(nt, b * S5_WIDTH)

    att = _attention(q, k, v, heads=MLA_HEADS, kv_heads=MLA_HEADS, dq=MLA_QPAD, dv=MLA_V,
                     scale=(MLA_NOPE + MLA_ROPE) ** -0.5, q_tile0=0, n_q_tiles=n_tiles, out_rows=nt)
    wo = w_out_ab[0].astype(BF16)
    seg0 = lambda t: jnp.minimum(t, 1)
    h1, xn2 = _oproj(h, s5_out, att, msel, wo[:S5_WIDTH], wo[S5_WIDTH:], a_time_major=True,
                     h_tile0=0, n_tiles=n_tiles, seg_of_tile=seg0)
    wup, cw, wdn = _ffn_weights(ffn_w_up[0], ffn_conv_w[0], ffn_conv_b[0], ffn_w_down[0])
    gfin = g_final[None]
    h = _ffn(h1, xn2, msel, wup, cw, wdn, gfin, first_tiles=(0, 1), last_tiles=(0, n_tiles - 1),
             seg_of_tile=seg0, final=False)

    msel = mod_select(1)
    tabs128 = _rope_tables(n_lat, RET_D)
    rq, rk, rv, rg, gq, gk, gv = _proj_cd(h, msel, w_in_cd[0].astype(BF16), gqa_g_q[0][None], gqa_g_k[0][None],
                                          tabs128)
    ret = _retention(rq, rk, rv, rg, ret_decay_logit[0], n_lat)
    att = _attention(gq, gk, gv, heads=GQA_HEADS, kv_heads=GQA_KV_HEADS, dq=GQA_D, dv=GQA_D,
                     scale=GQA_D ** -0.5, q_tile0=1, n_q_tiles=n_tiles - 1, out_rows=n_lat)
    wo = w_out_cd[0].astype(BF16)
    seg1 = lambda t: 1
    n_half = RET_HEADS * RET_D
    h1, xn2 = _oproj(h, ret, att, msel, wo[:n_half], wo[n_half:], a_time_major=False,
                     h_tile0=1, n_tiles=n_tiles - 1, seg_of_tile=seg1)
    wup, cw, wdn = _ffn_weights(ffn_w_up[1], ffn_conv_w[1], ffn_conv_b[1], ffn_w_down[1])
    return _ffn(h1, xn2, msel, wup, cw, wdn, gfin, first_tiles=(0,), last_tiles=(n_tiles - 2,),
                seg_of_tile=seg1, final=True)
```

```python
import functools
import math

import numpy as np
import jax
import jax.numpy as jnp
from jax import lax
from jax.experimental import pallas as pl
from jax.experimental.pallas import tpu as pltpu

F32 = jnp.float32
BF16 = jnp.bfloat16

D_MODEL = 1024
GRID_W = 64
CTX_LEN = 256
EPS = 1e-6
ROPE_THETA = 10000.0

S5_WIDTH = 512
S5_GROUP = 16
S5_GROUPS = 32
S5_STATE = 64
S5_CH = S5_GROUPS * S5_STATE
S5_COLS = 4
MLA_HEADS = 4
MLA_NOPE = 128
MLA_ROPE = 64
MLA_V = 128
MLA_Q_RANK = 384
MLA_KV_RANK = 256
MLA_QPAD = 256
RET_HEADS = 4
RET_D = 128
RET_CHUNK = 256
GQA_HEADS = 4
GQA_KV_HEADS = 2
GQA_D = 128
LOG2E = 1.4426950408889634
MLA_QSCALE = (MLA_NOPE + MLA_ROPE) ** -0.5 * LOG2E
GQA_QSCALE = GQA_D ** -0.5 * LOG2E
FFN_HIDDEN = 2816
FFN_CHUNK = 256
CONV_W = 3

TM = 256
HALO = 16
S5_LC = 16
VMEM_LIMIT = 56 * 1024 * 1024


def _cparams(sem):
    return pltpu.CompilerParams(dimension_semantics=sem, vmem_limit_bytes=VMEM_LIMIT)


def _rms(x):
    return x * lax.rsqrt(jnp.mean(x * x, axis=-1, keepdims=True) + EPS)


def _dot(a, b):
    return jnp.dot(a, b, preferred_element_type=F32)


def _rope(x, cos, sa, sb, quarter):
    return x * cos + pltpu.roll(x, quarter, 1) * sa + pltpu.roll(x, 128 - quarter, 1) * sb


def _rope_tables(n_lat, dim):
    quarter = dim // 4
    inv = (ROPE_THETA ** (-np.arange(quarter, dtype=np.float32) / quarter)).astype(np.float32)
    t = np.arange(n_lat)
    ang_r = (t // GRID_W).astype(np.float32)[:, None] * inv
    ang_c = (t % GRID_W).astype(np.float32)[:, None] * inv
    ang = np.concatenate([ang_r, ang_r, ang_c, ang_c], axis=-1).astype(np.float32)
    cos, sin = np.cos(ang), np.sin(ang)
    lane_q = (np.arange(dim) // quarter) % 2
    nt = CTX_LEN + n_lat
    cos_t = np.ones((nt, 128), np.float32)
    sa_t = np.zeros((nt, 128), np.float32)
    sb_t = np.zeros((nt, 128), np.float32)
    cos_t[CTX_LEN:, :dim] = cos
    sa_t[CTX_LEN:, :dim] = np.where(lane_q == 1, sin, 0.0)
    sb_t[CTX_LEN:, :dim] = np.where(lane_q == 0, -sin, 0.0)
    return jnp.asarray(cos_t), jnp.asarray(sa_t), jnp.asarray(sb_t)


def _mod_kernel(c_ref, w_ref, b_ref, o_ref):
    c = c_ref[...]
    s = c * jax.nn.sigmoid(c)
    o_ref[0] = _dot(s.astype(BF16), w_ref[0].astype(BF16)) + b_ref[0]


def _modulation(cond, w_mod, b_mod):
    depth, d, n6 = w_mod.shape
    rows = cond.shape[0]
    tn = 1024
    return pl.pallas_call(
        _mod_kernel,
        grid=(depth, n6 // tn),
        in_specs=[pl.BlockSpec((rows, d), lambda l, j: (0, 0)),
                  pl.BlockSpec((1, d, tn), lambda l, j: (l, 0, j)),
                  pl.BlockSpec((1, 1, tn), lambda l, j: (l, 0, j))],
        out_specs=pl.BlockSpec((1, rows, tn), lambda l, j: (l, 0, j)),
        out_shape=jax.ShapeDtypeStruct((depth, rows, n6), F32),
        compiler_params=_cparams(("arbitrary", "arbitrary")),
        name="modulation",
    )(cond, w_mod, b_mod.reshape(depth, 1, n6))


def _proj_ab_kernel(h_ref, m_ref, win_ref, gq_ref, wuq_ref, gkv_ref, wkn_ref, wv_ref,
                    cos_ref, sa_ref, sb_ref, u_ref, q_ref, k_ref, v_ref):
    m = m_ref[0, 0]
    xn = _rms(h_ref[0]) * (1.0 + m[1:2]) + m[0:1]
    z = _dot(xn.astype(BF16), win_ref[...])
    u_ref[...] = z[:, :S5_WIDTH]
    cq = z[:, 512:896]
    ckv = z[:, 896:1152]
    kr = z[:, 1152:1280]
    cos, sa, sb = cos_ref[...], sa_ref[...], sb_ref[...]
    q = _dot((_rms(cq) * gq_ref[...]).astype(BF16), wuq_ref[...]) * MLA_QSCALE
    ckvn = (_rms(ckv) * gkv_ref[...]).astype(BF16)
    kn = _dot(ckvn, wkn_ref[...])
    krr = _rope(kr, cos, sa, sb, MLA_ROPE // 4)
    qparts, kparts = [], []
    for h in range(MLA_HEADS):
        o = MLA_QPAD * h
        qparts += [q[:, o:o + 128], _rope(q[:, o + 128:o + 256], cos, sa, sb, MLA_ROPE // 4)]
        kparts += [kn[:, 128 * h:128 * (h + 1)], krr]
    q_ref[0] = jnp.concatenate(qparts, axis=-1).astype(BF16)
    k_ref[0] = jnp.concatenate(kparts, axis=-1).astype(BF16)
    v_ref[0] = _dot(ckvn, wv_ref[...]).astype(BF16)


def _proj_ab(h, msel, win, gq, wuq, gkv, wkn, wv, tabs):
    b, nt, d = h.shape
    const = lambda bi, t: (0, 0)
    row = lambda bi, t: (t, 0)
    tok = lambda bi, t: (bi, t, 0)
    return pl.pallas_call(
        _proj_ab_kernel,
        grid=(b, nt // TM),
        in_specs=[pl.BlockSpec((1, TM, d), tok),
                  pl.BlockSpec((1, 1, 6, d), lambda bi, t: (bi, jnp.minimum(t, 1), 0, 0)),
                  pl.BlockSpec(win.shape, const),
                  pl.BlockSpec(gq.shape, const),
                  pl.BlockSpec(wuq.shape, const),
                  pl.BlockSpec(gkv.shape, const),
                  pl.BlockSpec(wkn.shape, const),
                  pl.BlockSpec(wv.shape, const),
                  pl.BlockSpec((TM, 128), row),
                  pl.BlockSpec((TM, 128), row),
                  pl.BlockSpec((TM, 128), row)],
        out_specs=[pl.BlockSpec((TM, S5_WIDTH), lambda bi, t: (t, bi)),
                   pl.BlockSpec((1, TM, MLA_HEADS * MLA_QPAD), tok),
                   pl.BlockSpec((1, TM, MLA_HEADS * MLA_QPAD), tok),
                   pl.BlockSpec((1, TM, MLA_HEADS * MLA_V), tok)],
        out_shape=[jax.ShapeDtypeStruct((nt, b * S5_WIDTH), F32),
                   jax.ShapeDtypeStruct((b, nt, MLA_HEADS * MLA_QPAD), BF16),
                   jax.ShapeDtypeStruct((b, nt, MLA_HEADS * MLA_QPAD), BF16),
                   jax.ShapeDtypeStruct((b, nt, MLA_HEADS * MLA_V), BF16)],
        compiler_params=_cparams(("parallel", "parallel")),
        name="proj_ab",
    )(h, msel, win, gq, wuq, gkv, wkn, wv, *tabs)


def _s5_prep_kernel(lre_ref, lim_ref, ls_ref, bre_ref, bim_ref, ore_ref, oim_ref, obre_ref, obim_ref):
    lre, lim = lre_ref[...], lim_ref[...]
    dt = jnp.exp(ls_ref[...])
    ar, ai = lre * dt, lim * dt
    mag = jnp.exp(ar)
    lb_re, lb_im = mag * jnp.cos(ai), mag * jnp.sin(ai)
    den = lre * lre + lim * lim
    cf_re = ((lb_re - 1.0) * lre + lb_im * lim) / den
    cf_im = (lb_im * lre - (lb_re - 1.0) * lim) / den
    ore_ref[...] = lb_re
    oim_ref[...] = lb_im
    bre, bim = bre_ref[...], bim_ref[...]
    obre_ref[...] = cf_re * bre - cf_im * bim
    obim_ref[...] = cf_re * bim + cf_im * bre


def _s5_prep(lam_re, lam_im, log_step, b_re, b_im):
    g2 = 2 * S5_GROUPS
    args = (lam_re.reshape(g2, 1, S5_STATE), lam_im.reshape(g2, 1, S5_STATE), log_step.reshape(g2, 1, 1),
            jnp.swapaxes(b_re, -1, -2).reshape(g2, S5_GROUP, S5_STATE),
            jnp.swapaxes(b_im, -1, -2).reshape(g2, S5_GROUP, S5_STATE))
    full = lambda a: pl.BlockSpec(a.shape, lambda i: (0,) * a.ndim)
    lam_sds = jax.ShapeDtypeStruct((g2, 1, S5_STATE), F32)
    bb_sds = jax.ShapeDtypeStruct((g2, S5_GROUP, S5_STATE), F32)
    return pl.pallas_call(
        _s5_prep_kernel,
        grid=(1,),
        in_specs=[full(a) for a in args],
        out_specs=[full(lam_sds), full(lam_sds), full(bb_sds), full(bb_sds)],
        out_shape=[lam_sds, lam_sds, bb_sds, bb_sds],
        name="s5_prep",
    )(*args)


def _block_diag8(blocks):
    eye = jnp.eye(8, dtype=blocks.dtype)
    out = blocks[..., :, :, None, :] * eye[:, None, :, None]
    r, c = blocks.shape[-2:]
    return out.reshape(blocks.shape[:-3] + (8 * r, 8 * c))


def _s5_chunk(u_ref, bcat_ref, lam_ref, cre_ref, cim_ref, st_ref, bre_ref, bim_ref, lc, nb, reverse):
    u = u_ref[...].astype(BF16)
    for c in range(S5_COLS):
        bu = _dot(u[:, 128 * c:128 * (c + 1)], bcat_ref[c])
        bre_ref[:, 512 * c:512 * (c + 1)] = bu[:, :512]
        bim_ref[:, 512 * c:512 * (c + 1)] = bu[:, 512:]
    order = range(lc - 1, -1, -1) if reverse else range(lc)
    for c in range(S5_COLS):
        cs = slice(512 * c, 512 * (c + 1))
        lre = jnp.broadcast_to(lam_ref[0:1, cs], (nb, 512))
        lim = jnp.broadcast_to(lam_ref[1:2, cs], (nb, 512))
        sre, sim = st_ref[0, :, cs], st_ref[1, :, cs]
        for t in order:
            rs = slice(t * nb, (t + 1) * nb)
            nre = lre * sre - lim * sim + bre_ref[rs, cs]
            nim = lre * sim + lim * sre + bim_ref[rs, cs]
            bre_ref[rs, cs] = nre
            bim_ref[rs, cs] = nim
            sre, sim = nre, nim
        st_ref[0, :, cs] = sre
        st_ref[1, :, cs] = sim
    ys = []
    for c in range(S5_COLS):
        cs = slice(512 * c, 512 * (c + 1))
        ys.append(_dot(bre_ref[:, cs].astype(BF16), cre_ref[c]) - _dot(bim_ref[:, cs].astype(BF16), cim_ref[c]))
    return jnp.concatenate(ys, axis=-1)


def _s5_fwd_kernel(u_ref, bcat_ref, lam_ref, cre_ref, cim_ref, y_ref, st_ref, bre_ref, bim_ref, *, lc, nb):
    @pl.when(pl.program_id(0) == 0)
    def _():
        st_ref[...] = jnp.zeros(st_ref.shape, F32)

    y_ref[...] = _s5_chunk(u_ref, bcat_ref, lam_ref, cre_ref, cim_ref, st_ref, bre_ref, bim_ref, lc, nb, False)


def _s5_bwd_kernel(u_ref, yf_ref, bcat_ref, lam_ref, cre_ref, cim_ref, d_ref, wglu_ref, bglu_ref, o_ref,
                   st_ref, bre_ref, bim_ref, *, lc, nb):
    @pl.when(pl.program_id(0) == 0)
    def _():
        st_ref[...] = jnp.zeros(st_ref.shape, F32)

    yb = _s5_chunk(u_ref, bcat_ref, lam_ref, cre_ref, cim_ref, st_ref, bre_ref, bim_ref, lc, nb, True)
    y = d_ref[...] * u_ref[...] + yf_ref[...] + yb
    g = jax.nn.gelu(y)
    o_ref[...] = (g * jax.nn.sigmoid(_dot(g.astype(BF16), wglu_ref[...]) + bglu_ref[...])).astype(BF16)


def _s5_mixer(u_tm, nb, bcat, lam, cre, cim, d_skip, w_glu, b_glu):
    rows_total = u_tm.shape[0]
    lc = S5_LC
    rows = lc * nb
    n_steps = rows_total // rows
    n_ctx = CTX_LEN // lc
    scratch =[pltpu.VMEM((2, nb, S5_CH), F32), pltpu.VMEM((rows, S5_CH), F32), pltpu.VMEM((rows, S5_CH), F32)]
    fwd_blk = pl.BlockSpec((rows, S5_WIDTH), lambda i: (i, 0))
    yf = pl.pallas_call(
        functools.partial(_s5_fwd_kernel, lc=lc, nb=nb),
        grid=(n_steps,),
        in_specs=[fwd_blk] + [pl.BlockSpec((None,) + a.shape[1:], lambda i, n=a.ndim: (0,) * n)
                              for a in (bcat, lam, cre, cim)],
        out_specs=fwd_blk,
        out_shape=jax.ShapeDtypeStruct((rows_total, S5_WIDTH), F32),
        scratch_shapes=scratch,
        compiler_params=_cparams(("arbitrary",)),
        name="s5_fwd",
    )(u_tm, bcat, lam, cre, cim)

    def chunk(i):
        return jnp.where(i < n_ctx, n_ctx - 1 - i, n_steps - 1 - (i - n_ctx))

    bwd_blk = pl.BlockSpec((rows, S5_WIDTH), lambda i: (chunk(i), 0))
    whole = lambda a: pl.BlockSpec(a.shape, lambda i, n=a.ndim: (0,) * n)
    return pl.pallas_call(
        functools.partial(_s5_bwd_kernel, lc=lc, nb=nb),
        grid=(n_steps,),
        in_specs=[bwd_blk, bwd_blk] + [pl.BlockSpec((None,) + a.shape[1:], lambda i, n=a.ndim: (1,) + (0,) * (n - 1))
                                       for a in (bcat, lam, cre, cim)]
                 + [whole(d_skip), whole(w_glu), whole(b_glu)],
        out_specs=bwd_blk,
        out_shape=jax.ShapeDtypeStruct((rows_total, S5_WIDTH), BF16),
        scratch_shapes=scratch,
        compiler_params=_cparams(("arbitrary",)),
        name="s5_bwd_glu",
    )(u_tm, yf, bcat, lam, cre, cim, d_skip, w_glu, b_glu)


def _attn_kernel(q_ref, k_ref, v_ref, o_ref, *, heads, rep, dq, dv, n_ctx_keys, ctx_tile):
    def run(nk):
        def scores(h):
            g = h // rep
            return lax.dot_general(q_ref[0, :, dq * h:dq * (h + 1)], k_ref[0, :nk, dq * g:dq * (g + 1)],
                                   (((1,), (1,)), ((), ())), preferred_element_type=F32)

        outs = []
        nxt = scores(0)
        for h in range(heads):
            s = nxt
            if h + 1 < heads:
                nxt = scores(h + 1)
            p = jnp.exp2(s - jnp.max(s, axis=-1, keepdims=True))
            l = jnp.sum(p, axis=-1, keepdims=True)
            g = h // rep
            outs.append(_dot(p.astype(BF16), v_ref[0, :nk, dv * g:dv * (g + 1)]) / l)
        o_ref[0] = jnp.concatenate(outs, axis=-1).astype(o_ref.dtype)

    nk_all = k_ref.shape[1]
    if ctx_tile:
        qi = pl.program_id(1)

        @pl.when(qi == 0)
        def _():
            run(n_ctx_keys)

        @pl.when(qi > 0)
        def _():
            run(nk_all)
    else:
        run(nk_all)


def _attention(q, k, v, *, heads, kv_heads, dq, dv, q_tile0, n_q_tiles, out_rows):
    b, nt, _ = q.shape
    return pl.pallas_call(
        functools.partial(_attn_kernel, heads=heads, rep=heads // kv_heads, dq=dq, dv=dv,
                          n_ctx_keys=CTX_LEN, ctx_tile=(q_tile0 == 0)),
        grid=(b, n_q_tiles),
        in_specs=[pl.BlockSpec((1, TM, heads * dq), lambda bi, qi: (bi, qi + q_tile0, 0)),
                  pl.BlockSpec((1, nt, kv_heads * dq), lambda bi, qi: (bi, 0, 0)),
                  pl.BlockSpec((1, nt, kv_heads * dv), lambda bi, qi: (bi, 0, 0))],
        out_specs=pl.BlockSpec((1, TM, heads * dv), lambda bi, qi: (bi, qi, 0)),
        out_shape=jax.ShapeDtypeStruct((b, out_rows, heads * dv), BF16),
        compiler_params=_cparams(("parallel", "arbitrary")),
        name="attention",
    )(q, k, v)


def _oproj_kernel(h_ref, a_ref, b_ref, m_ref, wa_ref, wb_ref, h1_ref, xn_ref, *, a_time_major):
    a = a_ref[...] if a_time_major else a_ref[0]
    y = _dot(a, wa_ref[...]) + _dot(b_ref[0], wb_ref[...])
    m = m_ref[0, 0]
    h1 = h_ref[0] + m[2:3] * y
    h1_ref[0] = h1
    xn_ref[0] = (_rms(h1) * (1.0 + m[4:5]) + m[3:4]).astype(BF16)


def _oproj(h, a, bmix, msel, wa, wb, *, a_time_major, h_tile0, n_tiles, seg_of_tile):
    b, _, d = h.shape
    rows = n_tiles * TM
    tok = lambda bi, t: (bi, t, 0)
    const = lambda bi, t: (0, 0)
    if a_time_major:
        a_spec = pl.BlockSpec((TM, a.shape[1] // b), lambda bi, t: (t, bi))
    else:
        a_spec = pl.BlockSpec((1, TM, a.shape[2]), tok)
    return pl.pallas_call(
        functools.partial(_oproj_kernel, a_time_major=a_time_major),
        grid=(b, n_tiles),
        in_specs=[pl.BlockSpec((1, TM, d), lambda bi, t: (bi, t + h_tile0, 0)),
                  a_spec,
                  pl.BlockSpec((1, TM, bmix.shape[2]), tok),
                  pl.BlockSpec((1, 1, 6, d), lambda bi, t: (bi, seg_of_tile(t), 0, 0)),
                  pl.BlockSpec(wa.shape, const),
                  pl.BlockSpec(wb.shape, const)],
        out_specs=[pl.BlockSpec((1, TM, d), tok), pl.BlockSpec((1, TM, d), tok)],
        out_shape=[jax.ShapeDtypeStruct((b, rows, d), F32), jax.ShapeDtypeStruct((b, rows, d), BF16)],
        compiler_params=_cparams(("parallel", "parallel")),
        name="oproj",
    )(h, a, bmix, msel, wa, wb)


def _ffn_kernel(h_ref, xn_ref, xp_ref, xq_ref, m_ref, wup_ref, cw_ref, wdn_ref, gfin_ref, o_ref,
                xe_ref, acc_ref, *, first_tiles, last_tiles, final):
    t = pl.program_id(1)
    has_prev = functools.reduce(jnp.logical_and, [t != ft for ft in first_tiles])
    has_next = functools.reduce(jnp.logical_and, [t != lt for lt in last_tiles])
    xe_ref[0:TM] = xn_ref[0]
    xe_ref[TM:TM + HALO] = xp_ref[0]
    xe_ref[TM + HALO:TM + 2 * HALO] = xq_ref[0]
    xe = xe_ref[...]
    x = xe[:TM]
    row = lax.broadcasted_iota(jnp.int32, (TM, FFN_CHUNK), 0)
    n_chunks = wup_ref.shape[0]

    def up(j):
        return _dot(xe, wup_ref[j, :, :FFN_CHUNK]), _dot(x, wup_ref[j, :, FFN_CHUNK:])

    nxt = up(0)
    for j in range(n_chunks):
        ae, g = nxt
        if j + 1 < n_chunks:
            nxt = up(j + 1)
        a = ae[:TM]
        a_prev = jnp.where(has_prev, ae[TM + HALO - 1:TM + HALO], 0.0)
        a_next = jnp.where(has_next, ae[TM + HALO:TM + HALO + 1], 0.0)
        a_dn = jnp.where(row == 0, a_prev, pltpu.roll(a, 1, 0))
        a_up = jnp.where(row == TM - 1, a_next, pltpu.roll(a, TM - 1, 0))
        cw = cw_ref[j]
        conv = cw[3:4] + a_dn * cw[0:1] + a * cw[1:2] + a_up * cw[2:3]
        hid = (jax.nn.gelu(conv) * g).astype(BF16)
        part = _dot(hid, wdn_ref[j])
        if j == 0:
            acc_ref[...] = part
        else:
            acc_ref[...] += part
    out = h_ref[0] + m_ref[0, 0][5:6] * acc_ref[...]
    if final:
        out = _rms(out) * gfin_ref[...]
    o_ref[0] = out


def _ffn(h1, xn2, msel, wup, cw, wdn, gfin, *, first_tiles, last_tiles, seg_of_tile, final):
    b, rows, d = h1.shape
    n_tiles = rows // TM
    hb = TM // HALO
    tok = lambda bi, t: (bi, t, 0)
    c3 = lambda bi, t: (0, 0, 0)
    single = dict(pipeline_mode=pl.Buffered(1))
    return pl.pallas_call(
        functools.partial(_ffn_kernel, first_tiles=first_tiles, last_tiles=last_tiles, final=final),
        grid=(b, n_tiles),
        in_specs=[pl.BlockSpec((1, TM, d), tok),
                  pl.BlockSpec((1, TM, d), tok),
                  pl.BlockSpec((1, HALO, d), lambda bi, t: (bi, jnp.maximum(t * hb - 1, 0), 0)),
                  pl.BlockSpec((1, HALO, d), lambda bi, t: (bi, jnp.minimum((t + 1) * hb, rows // HALO - 1), 0)),
                  pl.BlockSpec((1, 1, 6, d), lambda bi, t: (bi, seg_of_tile(t), 0, 0)),
                  pl.BlockSpec(wup.shape, c3, **single),
                  pl.BlockSpec(cw.shape, c3, **single),
                  pl.BlockSpec(wdn.shape, c3, **single),
                  pl.BlockSpec(gfin.shape, lambda bi, t: (0, 0))],
        out_specs=pl.BlockSpec((1, TM, d), tok),
        out_shape=jax.ShapeDtypeStruct((b, rows, d), F32),
        scratch_shapes=[pltpu.VMEM((TM + 2 * HALO, d), BF16), pltpu.VMEM((TM, d), F32)],
        compiler_params=_cparams(("parallel", "parallel")),
        name="conv_ffn",
    )(h1, xn2, xn2, xn2, msel, wup, cw, wdn, gfin)


def _proj_cd_kernel(h_ref, m_ref, win_ref, ggq_ref, ggk_ref, cos_ref, sa_ref, sb_ref,
                    rq_ref, rk_ref, rv_ref, rg_ref, gq_ref, gk_ref, gv_ref):
    m = m_ref[0, 0]
    xn = _rms(h_ref[0]) * (1.0 + m[1:2]) + m[0:1]
    z = _dot(xn.astype(BF16), win_ref[...])
    cos, sa, sb = cos_ref[...], sa_ref[...], sb_ref[...]
    rope = lambda x: _rope(x, cos, sa, sb, RET_D // 4)
    head = lambda base, h: z[:, base + 128 * h:base + 128 * (h + 1)]
    k_scale = RET_D ** -0.5
    rq_ref[0] = jnp.concatenate([rope(head(0, h)) for h in range(RET_HEADS)], -1).astype(BF16)
    rk_ref[0] = jnp.concatenate([rope(head(512, h)) * k_scale for h in range(RET_HEADS)], -1).astype(BF16)
    rv_ref[0] = z[:, 1024:1536].astype(BF16)
    rg_ref[0] = z[:, 1536:2048]
    gq_ref[0] = jnp.concatenate([rope(_rms(head(2048, h)) * ggq_ref[...]) * GQA_QSCALE
                                 for h in range(GQA_HEADS)], -1).astype(BF16)
    gk_ref[0] = jnp.concatenate([rope(_rms(head(2560, h)) * ggk_ref[...]) for h in range(GQA_KV_HEADS)],
                                -1).astype(BF16)
    gv_ref[0] = z[:, 2816:3072].astype(BF16)


def _proj_cd(h, msel, win, ggq, ggk, tabs):
    b, nt, d = h.shape
    const = lambda bi, t: (0, 0)
    row = lambda bi, t: (t, 0)
    tok = lambda bi, t: (bi, t, 0)
    widths = (512, 512, 512, 512, 512, 256, 256)
    dtypes = (BF16, BF16, BF16, F32, BF16, BF16, BF16)
    return pl.pallas_call(
        _proj_cd_kernel,
        grid=(b, nt // TM),
        in_specs=[pl.BlockSpec((1, TM, d), tok),
                  pl.BlockSpec((1, 1, 6, d), lambda bi, t: (bi, jnp.minimum(t, 1), 0, 0)),
                  pl.BlockSpec(win.shape, const),
                  pl.BlockSpec(ggq.shape, const),
                  pl.BlockSpec(ggk.shape, const),
                  pl.BlockSpec((TM, 128), row),
                  pl.BlockSpec((TM, 128), row),
                  pl.BlockSpec((TM, 128), row)],
        out_specs=[pl.BlockSpec((1, TM, w), tok) for w in widths],
        out_shape=[jax.ShapeDtypeStruct((b, nt, w), dt) for w, dt in zip(widths, dtypes)],
        compiler_params=_cparams(("parallel", "parallel")),
        name="proj_cd",
    )(h, msel, win, ggq, ggk, *tabs)


def _retention_kernel(q_ref, k_ref, v_ref, g_ref, dl_ref, o_ref, *, n_lat):
    c = RET_CHUNK
    nc = n_lat // c
    lg = jax.nn.log_sigmoid(dl_ref[0])
    lgf, lgb = lg[0:1], lg[1:2]
    tdot = lambda a, b: lax.dot_general(a, b, (((0,), (0,)), ((), ())), preferred_element_type=F32)
    k_all, v_all = k_ref[0], v_ref[0]

    mc = lax.broadcasted_iota(jnp.int32, (CTX_LEN, RET_D), 0).astype(F32)
    kc = k_all[:CTX_LEN].astype(F32)
    vc = v_all[:CTX_LEN]
    s_f = tdot((kc * jnp.exp((CTX_LEN - 1.0 - mc) * lgf)).astype(BF16), vc)
    s_b = tdot((kc * jnp.exp(mc * lgb)).astype(BF16), vc)

    i = lax.broadcasted_iota(jnp.int32, (c, RET_D), 0).astype(F32)
    kd_f = jnp.exp((c - 1.0 - i) * lgf)
    kd_b = jnp.exp(i * lgb)
    qd_f = jnp.exp((i + 1.0) * lgf)
    qd_b = jnp.exp((c - i) * lgb)
    cd_f = jnp.exp(c * lgf)
    cd_b = jnp.exp(c * lgb)
    ii = lax.broadcasted_iota(jnp.int32, (c, c), 0)
    jj = lax.broadcasted_iota(jnp.int32, (c, c), 1)
    dist = (ii - jj).astype(F32)
    lgf_c = jnp.concatenate([lgf] * (c // RET_D), axis=-1)
    lgb_c = jnp.concatenate([lgb] * (c // RET_D), axis=-1)
    dec = (jnp.where(ii >= jj, jnp.exp(jnp.maximum(dist, 0.0) * lgf_c), 0.0)
           + jnp.where(ii <= jj, jnp.exp(jnp.maximum(-dist, 0.0) * lgb_c), 0.0))

    ks = [k_all[CTX_LEN + c * n:CTX_LEN + c * (n + 1)] for n in range(nc)]
    vs = [v_all[CTX_LEN + c * n:CTX_LEN + c * (n + 1)] for n in range(nc)]
    sf = [s_f]
    for n in range(nc - 1):
        sf.append(cd_f * sf[n] + tdot((ks[n].astype(F32) * kd_f).astype(BF16), vs[n]))
    sb = [None] * nc
    sb[nc - 1] = s_b
    for n in range(nc - 1, 0, -1):
        sb[n - 1] = cd_b * sb[n] + tdot((ks[n].astype(F32) * kd_b).astype(BF16), vs[n])

    for n in range(nc):
        rs = slice(CTX_LEN + c * n, CTX_LEN + c * (n + 1))
        qn = q_ref[0, rs]
        att = lax.dot_general(qn, ks[n], (((1,), (1,)), ((), ())), preferred_element_type=F32) * dec
        o = (_dot(att.astype(BF16), vs[n])
             + _dot(qn, sf[n].astype(BF16)) * qd_f
             + _dot(qn, sb[n].astype(BF16)) * qd_b)
        gate = g_ref[0, rs]
        o_ref[0, c * n:c * (n + 1)] = (_rms(o) * (gate * jax.nn.sigmoid(gate))).astype(BF16)


def _retention(rq, rk, rv, rg, decay_logit, n_lat):
    b, nt, _ = rq.shape
    dl = jnp.broadcast_to(jnp.swapaxes(decay_logit, 0, 1)[:, :, None], (RET_HEADS, 2, 128))
    blk = pl.BlockSpec((1, nt, RET_D), lambda bi, h: (bi, 0, h))
    return pl.pallas_call(
        functools.partial(_retention_kernel, n_lat=n_lat),
        grid=(b, RET_HEADS),
        in_specs=[blk, blk, blk, blk, pl.BlockSpec((1, 2, 128), lambda bi, h: (h, 0, 0))],
        out_specs=pl.BlockSpec((1, n_lat, RET_D), lambda bi, h: (bi, 0, h)),
        out_shape=jax.ShapeDtypeStruct((b, n_lat, RET_HEADS * RET_D), BF16),
        compiler_params=_cparams(("parallel", "parallel")),
        name="retention",
    )(rq, rk, rv, rg, dl)


def _ffn_weights(w_up, conv_w, conv_b, w_down):
    d, f2 = w_up.shape
    f = f2 // 2
    nf = f // FFN_CHUNK
    wa = w_up[:, :f].reshape(d, nf, FFN_CHUNK)
    wg = w_up[:, f:].reshape(d, nf, FFN_CHUNK)
    wup = jnp.transpose(jnp.concatenate([wa, wg], axis=-1), (1, 0, 2)).astype(BF16)
    cw = jnp.concatenate([conv_w, conv_b[None], jnp.zeros((4, f), F32)], axis=0)
    cw = jnp.transpose(cw.reshape(8, nf, FFN_CHUNK), (1, 0, 2))
    wdn = w_down.reshape(nf, FFN_CHUNK, d).astype(BF16)
    return wup, cw, wdn


def kernel(x, c, ctx, c_ctx, w_mod, b_mod, w_in_ab, w_out_ab, s5_lam_re, s5_lam_im, s5_log_step, s5_b_re, s5_b_im, s5_c_re, s5_c_im, s5_d, s5_w_glu, s5_b_glu, mla_g_q, mla_w_uq, mla_g_kv, mla_w_ukv, w_in_cd, w_out_cd, ret_decay_logit, gqa_g_q, gqa_g_k, ffn_w_up, ffn_conv_w, ffn_conv_b, ffn_w_down, g_final):
    b, n_lat, d = x.shape
    nt = CTX_LEN + n_lat
    n_tiles = nt // TM

    rows = ((b + 1 + 7) // 8) * 8
    cond = jnp.concatenate([c, c_ctx[None], jnp.zeros((rows - b - 1, d), F32)], axis=0)
    mods = _modulation(cond, w_mod, b_mod).reshape(w_mod.shape[0], rows, 6, d)

    def mod_select(layer):
        mctx = jnp.broadcast_to(mods[layer, b][None], (b, 6, d))
        return jnp.stack([mctx, mods[layer, :b]], axis=1)

    h = jnp.concatenate([ctx, x], axis=1)

    msel = mod_select(0)
    win = jnp.pad(w_in_ab[0], ((0, 0), (0, 64))).astype(BF16)
    wuq = jnp.pad(mla_w_uq[0].reshape(MLA_Q_RANK, MLA_HEADS, MLA_NOPE + MLA_ROPE),
                  ((0, 0), (0, 0), (0, MLA_QPAD - MLA_NOPE - MLA_ROPE))).reshape(MLA_Q_RANK, -1).astype(BF16)
    wukv = mla_w_ukv[0].reshape(MLA_KV_RANK, MLA_HEADS, MLA_NOPE + MLA_V)
    wkn = wukv[:, :, :MLA_NOPE].reshape(MLA_KV_RANK, -1).astype(BF16)
    wv = wukv[:, :, MLA_NOPE:].reshape(MLA_KV_RANK, -1).astype(BF16)
    tabs64 = _rope_tables(n_lat, MLA_ROPE)
    u_tm, q, k, v = _proj_ab(h, msel, win, mla_g_q[0][None], wuq, mla_g_kv[0][None], wkn, wv, tabs64)

    lam_re, lam_im, bbt_re, bbt_im = _s5_prep(s5_lam_re[0], s5_lam_im[0], s5_log_step[0], s5_b_re[0], s5_b_im[0])
    lam = jnp.stack([lam_re.reshape(2, S5_CH), lam_im.reshape(2, S5_CH)], axis=1)
    bshape = (2, S5_COLS, 8, S5_GROUP, S5_STATE)
    bcat = jnp.concatenate([_block_diag8(bbt_re.reshape(bshape)), _block_diag8(bbt_im.reshape(bshape))],
                           axis=-1).astype(BF16)
    cshape = (2, S5_COLS, 8, S5_GROUP, S5_STATE)
    cre = _block_diag8(jnp.swapaxes(s5_c_re[0].reshape(cshape), -1, -2)).astype(BF16)
    cim = _block_diag8(jnp.swapaxes(s5_c_im[0].reshape(cshape), -1, -2)).astype(BF16)
    s5_out = _s5_mixer(u_tm.reshape(nt * b, S5_WIDTH), b, bcat, lam, cre, cim,
                       s5_d[0].reshape(1, S5_WIDTH), s5_w_glu[0].astype(BF16), s5_b_glu[0][None])
    s5_out = s5_out.reshape(nt, b * S5_WIDTH)

    att = _attention(q, k, v, heads=MLA_HEADS, kv_heads=MLA_HEADS, dq=MLA_QPAD, dv=MLA_V,
                     q_tile0=0, n_q_tiles=n_tiles, out_rows=nt)
    wo = w_out_ab[0].astype(BF16)
    seg0 = lambda t: jnp.minimum(t, 1)
    h1, xn2 = _oproj(h, s5_out, att, msel, wo[:S5_WIDTH], wo[S5_WIDTH:], a_time_major=True,
                     h_tile0=0, n_tiles=n_tiles, seg_of_tile=seg0)
    wup, cw, wdn = _ffn_weights(ffn_w_up[0], ffn_conv_w[0], ffn_conv_b[0], ffn_w_down[0])
    gfin = g_final[None]
    h = _ffn(h1, xn2, msel, wup, cw, wdn, gfin, first_tiles=(0, 1), last_tiles=(0, n_tiles - 1),
             seg_of_tile=seg0, final=False)

    msel = mod_select(1)
    tabs128 = _rope_tables(n_lat, RET_D)
    rq, rk, rv, rg, gq, gk, gv = _proj_cd(h, msel, w_in_cd[0].astype(BF16), gqa_g_q[0][None], gqa_g_k[0][None],
                                          tabs128)
    ret = _retention(rq, rk, rv, rg, ret_decay_logit[0], n_lat)
    att = _attention(gq, gk, gv, heads=GQA_HEADS, kv_heads=GQA_KV_HEADS, dq=GQA_D, dv=GQA_D,
                     q_tile0=1, n_q_tiles=n_tiles - 1, out_rows=n_lat)
    wo = w_out_cd[0].astype(BF16)
    seg1 = lambda t: 1
    n_half = RET_HEADS * RET_D
    h1, xn2 = _oproj(h, ret, att, msel, wo[:n_half], wo[n_half:], a_time_major=False,
                     h_tile0=1, n_tiles=n_tiles - 1, seg_of_tile=seg1)
    wup, cw, wdn = _ffn_weights(ffn_w_up[1], ffn_conv_w[1], ffn_conv_b[1], ffn_w_down[1])
    return _ffn(h1, xn2, msel, wup, cw, wdn, gfin, first_tiles=(0,), last_tiles=(n_tiles - 2,),
                seg_of_tile=seg1, final=True)
```

```python
import functools
import math

import numpy as np
import jax
import jax.numpy as jnp
from jax import lax
from jax.experimental import pallas as pl
from jax.experimental.pallas import tpu as pltpu

F32 = jnp.float32
BF16 = jnp.bfloat16

D_MODEL = 1024
GRID_W = 64
CTX_LEN = 256
EPS = 1e-6
ROPE_THETA = 10000.0

S5_WIDTH = 512
S5_GROUP = 16
S5_GROUPS = 32
S5_STATE = 64
S5_CH = S5_GROUPS * S5_STATE
S5_COLS = 4
S5_OUT_COLS = 2
MLA_HEADS = 4
MLA_NOPE = 128
MLA_ROPE = 64
MLA_V = 128
MLA_Q_RANK = 384
MLA_KV_RANK = 256
MLA_QPAD = 256
RET_HEADS = 4
RET_D = 128
RET_CHUNK = 256
GQA_HEADS = 4
GQA_KV_HEADS = 2
GQA_D = 128
LOG2E = 1.4426950408889634
MLA_QSCALE = (MLA_NOPE + MLA_ROPE) ** -0.5 * LOG2E
GQA_QSCALE = GQA_D ** -0.5 * LOG2E
FFN_HIDDEN = 2816
FFN_CHUNK = 256
CONV_W = 3

TM = 256
HALO = 16
SUB = 128
S5_LC = 32
VMEM_LIMIT = 56 * 1024 * 1024


def _cparams(sem):
    return pltpu.CompilerParams(dimension_semantics=sem, vmem_limit_bytes=VMEM_LIMIT)


def _rms(x):
    return x * lax.rsqrt(jnp.mean(x * x, axis=-1, keepdims=True) + EPS)


def _dot(a, b):
    return jnp.dot(a, b, preferred_element_type=F32)


def _rope(x, cos, sa, sb, quarter):
    return x * cos + pltpu.roll(x, quarter, 1) * sa + pltpu.roll(x, 128 - quarter, 1) * sb


def _rope_tables(n_lat, dim):
    quarter = dim // 4
    inv = (ROPE_THETA ** (-np.arange(quarter, dtype=np.float32) / quarter)).astype(np.float32)
    t = np.arange(n_lat)
    ang_r = (t // GRID_W).astype(np.float32)[:, None] * inv
    ang_c = (t % GRID_W).astype(np.float32)[:, None] * inv
    ang = np.concatenate([ang_r, ang_r, ang_c, ang_c], axis=-1).astype(np.float32)
    cos, sin = np.cos(ang), np.sin(ang)
    lane_q = (np.arange(dim) // quarter) % 2
    nt = CTX_LEN + n_lat
    cos_t = np.ones((nt, 128), np.float32)
    sa_t = np.zeros((nt, 128), np.float32)
    sb_t = np.zeros((nt, 128), np.float32)
    cos_t[CTX_LEN:, :dim] = cos
    sa_t[CTX_LEN:, :dim] = np.where(lane_q == 1, sin, 0.0)
    sb_t[CTX_LEN:, :dim] = np.where(lane_q == 0, -sin, 0.0)
    return jnp.asarray(cos_t), jnp.asarray(sa_t), jnp.asarray(sb_t)


def _mod_kernel(c_ref, w_ref, b_ref, o_ref):
    c = c_ref[...]
    s = c * jax.nn.sigmoid(c)
    o_ref[0] = _dot(s.astype(BF16), w_ref[0].astype(BF16)) + b_ref[0]


def _modulation(cond, w_mod, b_mod):
    depth, d, n6 = w_mod.shape
    rows = cond.shape[0]
    tn = 1024
    return pl.pallas_call(
        _mod_kernel,
        grid=(depth, n6 // tn),
        in_specs=[pl.BlockSpec((rows, d), lambda l, j: (0, 0)),
                  pl.BlockSpec((1, d, tn), lambda l, j: (l, 0, j)),
                  pl.BlockSpec((1, 1, tn), lambda l, j: (l, 0, j))],
        out_specs=pl.BlockSpec((1, rows, tn), lambda l, j: (l, 0, j)),
        out_shape=jax.ShapeDtypeStruct((depth, rows, n6), F32),
        compiler_params=_cparams(("arbitrary", "arbitrary")),
        name="modulation",
    )(cond, w_mod, b_mod.reshape(depth, 1, n6))


def _stream_rows(t, x_ref, ctx_ref, rs):
    return jnp.where(t == 0, ctx_ref[0, rs], x_ref[0, rs])


def _proj_ab_kernel(x_ref, ctx_ref, m_ref, win_ref, gq_ref, wuq_ref, gkv_ref, wkn_ref, wv_ref,
                    cos_ref, sa_ref, sb_ref, u_ref, q_ref, k_ref, v_ref):
    m = m_ref[0, 0]
    t = pl.program_id(1)
    for r in range(TM // SUB):
        rs = slice(SUB * r, SUB * (r + 1))
        xn = _rms(_stream_rows(t, x_ref, ctx_ref, rs)) * (1.0 + m[1:2]) + m[0:1]
        z = _dot(xn.astype(BF16), win_ref[...])
        u_ref[rs] = z[:, :S5_WIDTH]
        cq = z[:, 512:896]
        ckv = z[:, 896:1152]
        kr = z[:, 1152:1280]
        cos, sa, sb = cos_ref[rs], sa_ref[rs], sb_ref[rs]
        q = _dot((_rms(cq) * gq_ref[...]).astype(BF16), wuq_ref[...]) * MLA_QSCALE
        ckvn = (_rms(ckv) * gkv_ref[...]).astype(BF16)
        kn = _dot(ckvn, wkn_ref[...])
        krr = _rope(kr, cos, sa, sb, MLA_ROPE // 4)
        qparts, kparts = [], []
        for h in range(MLA_HEADS):
            o = MLA_QPAD * h
            qparts += [q[:, o:o + 128], _rope(q[:, o + 128:o + 256], cos, sa, sb, MLA_ROPE // 4)]
            kparts += [kn[:, 128 * h:128 * (h + 1)], krr]
        q_ref[0, rs] = jnp.concatenate(qparts, axis=-1).astype(BF16)
        k_ref[0, rs] = jnp.concatenate(kparts, axis=-1).astype(BF16)
        v_ref[0, rs] = _dot(ckvn, wv_ref[...]).astype(BF16)


def _proj_ab(x, ctx, msel, win, gq, wuq, gkv, wkn, wv, tabs):
    b, n_lat, d = x.shape
    nt = CTX_LEN + n_lat
    const = lambda bi, t: (0, 0)
    row = lambda bi, t: (t, 0)
    tok = lambda bi, t: (bi, t, 0)
    return pl.pallas_call(
        _proj_ab_kernel,
        grid=(b, nt // TM),
        in_specs=[pl.BlockSpec((1, TM, d), lambda bi, t: (bi, jnp.maximum(t - 1, 0), 0)),
                  pl.BlockSpec((1, CTX_LEN, d), lambda bi, t: (bi, 0, 0)),
                  pl.BlockSpec((1, 1, 6, d), lambda bi, t: (bi, jnp.minimum(t, 1), 0, 0)),
                  pl.BlockSpec(win.shape, const),
                  pl.BlockSpec(gq.shape, const),
                  pl.BlockSpec(wuq.shape, const),
                  pl.BlockSpec(gkv.shape, const),
                  pl.BlockSpec(wkn.shape, const),
                  pl.BlockSpec(wv.shape, const),
                  pl.BlockSpec((TM, 128), row),
                  pl.BlockSpec((TM, 128), row),
                  pl.BlockSpec((TM, 128), row)],
        out_specs=[pl.BlockSpec((TM, S5_WIDTH), lambda bi, t: (t, bi)),
                   pl.BlockSpec((1, TM, MLA_HEADS * MLA_QPAD), tok),
                   pl.BlockSpec((1, TM, MLA_HEADS * MLA_QPAD), tok),
                   pl.BlockSpec((1, TM, MLA_HEADS * MLA_V), tok)],
        out_shape=[jax.ShapeDtypeStruct((nt, b * S5_WIDTH), F32),
                   jax.ShapeDtypeStruct((b, nt, MLA_HEADS * MLA_QPAD), BF16),
                   jax.ShapeDtypeStruct((b, nt, MLA_HEADS * MLA_QPAD), BF16),
                   jax.ShapeDtypeStruct((b, nt, MLA_HEADS * MLA_V), BF16)],
        compiler_params=_cparams(("parallel", "parallel")),
        name="proj_ab",
    )(x, ctx, msel, win, gq, wuq, gkv, wkn, wv, *tabs)


def _s5_prep_kernel(lre_ref, lim_ref, ls_ref, bre_ref, bim_ref, ore_ref, oim_ref, obre_ref, obim_ref):
    lre, lim = lre_ref[...], lim_ref[...]
    dt = jnp.exp(ls_ref[...])
    ar, ai = lre * dt, lim * dt
    mag = jnp.exp(ar)
    lb_re, lb_im = mag * jnp.cos(ai), mag * jnp.sin(ai)
    den = lre * lre + lim * lim
    cf_re = ((lb_re - 1.0) * lre + lb_im * lim) / den
    cf_im = (lb_im * lre - (lb_re - 1.0) * lim) / den
    ore_ref[...] = lb_re
    oim_ref[...] = lb_im
    bre, bim = bre_ref[...], bim_ref[...]
    obre_ref[...] = cf_re * bre - cf_im * bim
    obim_ref[...] = cf_re * bim + cf_im * bre


def _s5_prep(lam_re, lam_im, log_step, b_re, b_im):
    g2 = 2 * S5_GROUPS
    args = (lam_re.reshape(g2, 1, S5_STATE), lam_im.reshape(g2, 1, S5_STATE), log_step.reshape(g2, 1, 1),
            jnp.swapaxes(b_re, -1, -2).reshape(g2, S5_GROUP, S5_STATE),
            jnp.swapaxes(b_im, -1, -2).reshape(g2, S5_GROUP, S5_STATE))
    full = lambda a: pl.BlockSpec(a.shape, lambda i: (0,) * a.ndim)
    lam_sds = jax.ShapeDtypeStruct((g2, 1, S5_STATE), F32)
    bb_sds = jax.ShapeDtypeStruct((g2, S5_GROUP, S5_STATE), F32)
    return pl.pallas_call(
        _s5_prep_kernel,
        grid=(1,),
        in_specs=[full(a) for a in args],
        out_specs=[full(lam_sds), full(lam_sds), full(bb_sds), full(bb_sds)],
        out_shape=[lam_sds, lam_sds, bb_sds, bb_sds],
        name="s5_prep",
    )(*args)


def _block_diag(blocks):
    n, r, c = blocks.shape[-3:]
    eye = jnp.eye(n, dtype=blocks.dtype)
    out = blocks[..., :, :, None, :] * eye[:, None, :, None]
    return out.reshape(blocks.shape[:-3] + (n * r, n * c))


def _s5_chunk(u_ref, bcat_ref, lam_ref, cre_ref, cim_ref, st_ref, bre_ref, bim_ref, lc, nb, reverse):
    u = u_ref[...].astype(BF16)
    for c in range(S5_COLS):
        bu = _dot(u[:, 128 * c:128 * (c + 1)], bcat_ref[c])
        bre_ref[:, 512 * c:512 * (c + 1)] = bu[:, :512]
        bim_ref[:, 512 * c:512 * (c + 1)] = bu[:, 512:]
    order = range(lc - 1, -1, -1) if reverse else range(lc)
    for c in range(S5_COLS):
        cs = slice(512 * c, 512 * (c + 1))
        lre = jnp.broadcast_to(lam_ref[0:1, cs], (nb, 512))
        lim = jnp.broadcast_to(lam_ref[1:2, cs], (nb, 512))
        sre, sim = st_ref[0, :, cs], st_ref[1, :, cs]
        for t in order:
            rs = slice(t * nb, (t + 1) * nb)
            nre = lre * sre - lim * sim + bre_ref[rs, cs]
            nim = lre * sim + lim * sre + bim_ref[rs, cs]
            bre_ref[rs, cs] = nre
            bim_ref[rs, cs] = nim
            sre, sim = nre, nim
        st_ref[0, :, cs] = sre
        st_ref[1, :, cs] = sim
    ys = []
    for c in range(S5_OUT_COLS):
        cs = slice(1024 * c, 1024 * (c + 1))
        ys.append(_dot(bre_ref[:, cs].astype(BF16), cre_ref[c]) - _dot(bim_ref[:, cs].astype(BF16), cim_ref[c]))
    return jnp.concatenate(ys, axis=-1)


def _s5_fwd_kernel(u_ref, bcat_ref, lam_ref, cre_ref, cim_ref, y_ref, st_ref, bre_ref, bim_ref, *, lc, nb):
    @pl.when(pl.program_id(0) == 0)
    def _():
        st_ref[...] = jnp.zeros(st_ref.shape, F32)

    y_ref[...] = _s5_chunk(u_ref, bcat_ref, lam_ref, cre_ref, cim_ref, st_ref, bre_ref, bim_ref, lc, nb, False)


def _s5_bwd_kernel(u_ref, yf_ref, bcat_ref, lam_ref, cre_ref, cim_ref, d_ref, wglu_ref, bglu_ref, o_ref,
                   st_ref, bre_ref, bim_ref, *, lc, nb):
    @pl.when(pl.program_id(0) == 0)
    def _():
        st_ref[...] = jnp.zeros(st_ref.shape, F32)

    yb = _s5_chunk(u_ref, bcat_ref, lam_ref, cre_ref, cim_ref, st_ref, bre_ref, bim_ref, lc, nb, True)
    y = d_ref[...] * u_ref[...] + yf_ref[...] + yb
    g = jax.nn.gelu(y)
    o_ref[...] = (g * jax.nn.sigmoid(_dot(g.astype(BF16), wglu_ref[...]) + bglu_ref[...])).astype(BF16)


def _s5_mixer(u_tm, nb, bcat, lam, cre, cim, d_skip, w_glu, b_glu):
    rows_total = u_tm.shape[0]
    lc = S5_LC
    rows = lc * nb
    n_steps = rows_total // rows
    n_ctx = CTX_LEN // lc
    scratch = [pltpu.VMEM((2, nb, S5_CH), F32), pltpu.VMEM((rows, S5_CH), F32), pltpu.VMEM((rows, S5_CH), F32)]
    fwd_blk = pl.BlockSpec((rows, S5_WIDTH), lambda i: (i, 0))
    yf = pl.pallas_call(
        functools.partial(_s5_fwd_kernel, lc=lc, nb=nb),
        grid=(n_steps,),
        in_specs=[fwd_blk] + [pl.BlockSpec((None,) + a.shape[1:], lambda i, n=a.ndim: (0,) * n)
                              for a in (bcat, lam, cre, cim)],
        out_specs=fwd_blk,
        out_shape=jax.ShapeDtypeStruct((rows_total, S5_WIDTH), F32),
        scratch_shapes=scratch,
        compiler_params=_cparams(("arbitrary",)),
        name="s5_fwd",
    )(u_tm, bcat, lam, cre, cim)

    def chunk(i):
        return jnp.where(i < n_ctx, n_ctx - 1 - i, n_steps - 1 - (i - n_ctx))

    bwd_blk = pl.BlockSpec((rows, S5_WIDTH), lambda i: (chunk(i), 0))
    whole = lambda a: pl.BlockSpec(a.shape, lambda i, n=a.ndim: (0,) * n)
    return pl.pallas_call(
        functools.partial(_s5_bwd_kernel, lc=lc, nb=nb),
        grid=(n_steps,),
        in_specs=[bwd_blk, bwd_blk] + [pl.BlockSpec((None,) + a.shape[1:], lambda i, n=a.ndim: (1,) + (0,) * (n - 1))
                                       for a in (bcat, lam, cre, cim)]
                 + [whole(d_skip), whole(w_glu), whole(b_glu)],
        out_specs=bwd_blk,
        out_shape=jax.ShapeDtypeStruct((rows_total, S5_WIDTH), BF16),
        scratch_shapes=scratch,
        compiler_params=_cparams(("arbitrary",)),
        name="s5_bwd_glu",
    )(u_tm, yf, bcat, lam, cre, cim, d_skip, w_glu, b_glu)


def _attn_kernel(q_ref, k_ref, v_ref, o_ref, *, heads, rep, dq, dv, n_ctx_keys, ctx_tile):
    def run(nk):
        def scores(h):
            g = h // rep
            return lax.dot_general(q_ref[0, :, dq * h:dq * (h + 1)], k_ref[0, :nk, dq * g:dq * (g + 1)],
                                   (((1,), (1,)), ((), ())), preferred_element_type=F32)

        outs = []
        nxt = scores(0)
        for h in range(heads):
            s = nxt
            if h + 1 < heads:
                nxt = scores(h + 1)
            p = jnp.exp2(s - jnp.max(s, axis=-1, keepdims=True))
            l = jnp.sum(p, axis=-1, keepdims=True)
            g = h // rep
            outs.append(_dot(p.astype(BF16), v_ref[0, :nk, dv * g:dv * (g + 1)]) / l)
        o_ref[0] = jnp.concatenate(outs, axis=-1).astype(o_ref.dtype)

    nk_all = k_ref.shape[1]
    if ctx_tile:
        qi = pl.program_id(1)

        @pl.when(qi == 0)
        def _():
            run(n_ctx_keys)

        @pl.when(qi > 0)
        def _():
            run(nk_all)
    else:
        run(nk_all)


def _attention(q, k, v, *, heads, kv_heads, dq, dv, q_tile0, n_q_tiles, out_rows):
    b, nt, _ = q.shape
    return pl.pallas_call(
        functools.partial(_attn_kernel, heads=heads, rep=heads // kv_heads, dq=dq, dv=dv,
                          n_ctx_keys=CTX_LEN, ctx_tile=(q_tile0 == 0)),
        grid=(b, n_q_tiles),
        in_specs=[pl.BlockSpec((1, TM, heads * dq), lambda bi, qi: (bi, qi + q_tile0, 0)),
                  pl.BlockSpec((1, nt, kv_heads * dq), lambda bi, qi: (bi, 0, 0)),
                  pl.BlockSpec((1, nt, kv_heads * dv), lambda bi, qi: (bi, 0, 0))],
        out_specs=pl.BlockSpec((1, TM, heads * dv), lambda bi, qi: (bi, qi, 0)),
        out_shape=jax.ShapeDtypeStruct((b, out_rows, heads * dv), BF16),
        compiler_params=_cparams(("parallel", "arbitrary")),
        name="attention",
    )(q, k, v)


def _mix_ffn_kernel(*refs, dual, a_time_major, first_tiles, last_tiles, final):
    if dual:
        x_ref, ctx_ref, *refs = refs
    else:
        (x_ref, *refs), ctx_ref = refs, None
    (hp_ref, hq_ref, a_ref, ap_ref, aq_ref, b_ref, bp_ref, bq_ref, m_ref, wa_ref, wb_ref,
     wup_ref, cw_ref, wdn_ref, gfin_ref, o_ref, la_ref, lb_ref, h1_ref, xe_ref, acc_ref) = refs
    t = pl.program_id(1)
    has_prev = functools.reduce(jnp.logical_and, [t != ft for ft in first_tiles])
    has_next = functools.reduce(jnp.logical_and, [t != lt for lt in last_tiles])
    m = m_ref[0, 0]
    ext = TM + 2 * HALO
    ld = (lambda r: r[...]) if a_time_major else (lambda r: r[0])
    la_ref[0:TM] = ld(a_ref)
    la_ref[TM:TM + HALO] = ld(ap_ref)
    la_ref[TM + HALO:ext] = ld(aq_ref)
    lb_ref[0:TM] = b_ref[0]
    lb_ref[TM:TM + HALO] = bp_ref[0]
    lb_ref[TM + HALO:ext] = bq_ref[0]
    h_main = _stream_rows(t, x_ref, ctx_ref, slice(0, TM)) if dual else x_ref[0]
    h_ext = jnp.concatenate([h_main, hp_ref[0], hq_ref[0]], axis=0)
    for r in range(2):
        rs = slice(r * ext // 2, (r + 1) * ext // 2)
        y = _dot(la_ref[rs], wa_ref[...]) + _dot(lb_ref[rs], wb_ref[...])
        h1 = h_ext[rs] + m[2:3] * y
        h1_ref[rs] = h1
        xe_ref[rs] = (_rms(h1) * (1.0 + m[4:5]) + m[3:4]).astype(BF16)

    xe = xe_ref[...]
    x = xe[:TM]
    row = lax.broadcasted_iota(jnp.int32, (TM, FFN_CHUNK), 0)
    n_chunks = wup_ref.shape[0]

    def up(j):
        return _dot(xe, wup_ref[j, :, :FFN_CHUNK]), _dot(x, wup_ref[j, :, FFN_CHUNK:])

    nxt = up(0)
    for j in range(n_chunks):
        ae, g = nxt
        if j + 1 < n_chunks:
            nxt = up(j + 1)
        a = ae[:TM]
        a_prev = jnp.where(has_prev, ae[TM + HALO - 1:TM + HALO], 0.0)
        a_next = jnp.where(has_next, ae[TM + HALO:TM + HALO + 1], 0.0)
        a_dn = jnp.where(row == 0, a_prev, pltpu.roll(a, 1, 0))
        a_up = jnp.where(row == TM - 1, a_next, pltpu.roll(a, TM - 1, 0))
        cw = cw_ref[j]
        conv = cw[3:4] + a_dn * cw[0:1] + a * cw[1:2] + a_up * cw[2:3]
        hid = (jax.nn.gelu(conv) * g).astype(BF16)
        part = _dot(hid, wdn_ref[j])
        if j == 0:
            acc_ref[...] = part
        else:
            acc_ref[...] += part
    out = h1_ref[0:TM] + m[5:6] * acc_ref[...]
    if final:
        out = _rms(out) * gfin_ref[...]
    o_ref[0] = out


def _mix_ffn(hs, a, bmix, msel, wa, wb, wup, cw, wdn, gfin, *, dual, a_time_major, n_tiles,
             first_tiles, last_tiles, seg_of_tile, final):
    b, _, d = hs[0].shape
    hb = TM // HALO
    ext = TM + 2 * HALO
    c2 = lambda bi, t: (0, 0)
    c3 = lambda bi, t: (0, 0, 0)
    single = dict(pipeline_mode=pl.Buffered(1))
    prev_blk = lambda t: jnp.maximum(t * hb - 1, 0)
    next_blk = lambda t, n_rows: jnp.minimum((t + 1) * hb, n_rows // HALO - 1)
    if dual:
        n_x = hs[0].shape[1]
        h_specs = [pl.BlockSpec((1, TM, d), lambda bi, t: (bi, jnp.maximum(t - 1, 0), 0)),
                   pl.BlockSpec((1, CTX_LEN, d), lambda bi, t: (bi, 0, 0)),
                   pl.BlockSpec((1, HALO, d), lambda bi, t: (bi, prev_blk(jnp.maximum(t - 1, 0)), 0)),
                   pl.BlockSpec((1, HALO, d), lambda bi, t: (bi, next_blk(jnp.maximum(t - 1, 0), n_x), 0))]
        h_args = [hs[0], hs[1], hs[0], hs[0]]
    else:
        n_h = hs[0].shape[1]
        h_specs = [pl.BlockSpec((1, TM, d), lambda bi, t: (bi, t + 1, 0)),
                   pl.BlockSpec((1, HALO, d), lambda bi, t: (bi, prev_blk(t + 1), 0)),
                   pl.BlockSpec((1, HALO, d), lambda bi, t: (bi, next_blk(t + 1, n_h), 0))]
        h_args = [hs[0], hs[0], hs[0]]

    def mixer_specs(arr, time_major):
        if time_major:
            w, n_rows = arr.shape[1] // b, arr.shape[0]
            return [pl.BlockSpec((TM, w), lambda bi, t: (t, bi)),
                    pl.BlockSpec((HALO, w), lambda bi, t: (prev_blk(t), bi)),
                    pl.BlockSpec((HALO, w), lambda bi, t: (next_blk(t, n_rows), bi))]
        w, n_rows = arr.shape[2], arr.shape[1]
        return [pl.BlockSpec((1, TM, w), lambda bi, t: (bi, t, 0)),
                pl.BlockSpec((1, HALO, w), lambda bi, t: (bi, prev_blk(t), 0)),
                pl.BlockSpec((1, HALO, w), lambda bi, t: (bi, next_blk(t, n_rows), 0))]

    return pl.pallas_call(
        functools.partial(_mix_ffn_kernel, dual=dual, a_time_major=a_time_major, first_tiles=first_tiles,
                          last_tiles=last_tiles, final=final),
        grid=(b, n_tiles),
        in_specs=h_specs + mixer_specs(a, a_time_major) + mixer_specs(bmix, False)
                 + [pl.BlockSpec((1, 1, 6, d), lambda bi, t: (bi, seg_of_tile(t), 0, 0)),
                    pl.BlockSpec(wa.shape, c2, **single),
                    pl.BlockSpec(wb.shape, c2, **single),
                    pl.BlockSpec(wup.shape, c3, **single),
                    pl.BlockSpec(cw.shape, c3, **single),
                    pl.BlockSpec(wdn.shape, c3, **single),
                    pl.BlockSpec(gfin.shape, c2)],
        out_specs=pl.BlockSpec((1, TM, d), lambda bi, t: (bi, t, 0)),
        out_shape=jax.ShapeDtypeStruct((b, n_tiles * TM, d), F32),
        scratch_shapes=[pltpu.VMEM((ext, wa.shape[0]), BF16), pltpu.VMEM((ext, wb.shape[0]), BF16),
                        pltpu.VMEM((ext, d), F32), pltpu.VMEM((ext, d), BF16), pltpu.VMEM((TM, d), F32)],
        compiler_params=_cparams(("parallel", "parallel")),
        name="mix_ffn",
    )(*h_args, a, a, a, bmix, bmix, bmix, msel, wa, wb, wup, cw, wdn, gfin)


def _proj_cd_kernel(h_ref, m_ref, win_ref, ggq_ref, ggk_ref, cos_ref, sa_ref, sb_ref,
                    rq_ref, rk_ref, rv_ref, rg_ref, gq_ref, gk_ref, gv_ref):
    m = m_ref[0, 0]
    k_scale = RET_D ** -0.5
    for r in range(TM // SUB):
        rs = slice(SUB * r, SUB * (r + 1))
        xn = _rms(h_ref[0, rs]) * (1.0 + m[1:2]) + m[0:1]
        z = _dot(xn.astype(BF16), win_ref[...])
        cos, sa, sb = cos_ref[rs], sa_ref[rs], sb_ref[rs]
        rope = lambda x: _rope(x, cos, sa, sb, RET_D // 4)
        head = lambda base, h: z[:, base + 128 * h:base + 128 * (h + 1)]
        rq_ref[0, rs] = jnp.concatenate([rope(head(0, h)) for h in range(RET_HEADS)], -1).astype(BF16)
        rk_ref[0, rs] = jnp.concatenate([rope(head(512, h)) * k_scale for h in range(RET_HEADS)], -1).astype(BF16)
        rv_ref[0, rs] = z[:, 1024:1536].astype(BF16)
        rg_ref[0, rs] = z[:, 1536:2048]
        gq_ref[0, rs] = jnp.concatenate([rope(_rms(head(2048, h)) * ggq_ref[...]) * GQA_QSCALE
                                         for h in range(GQA_HEADS)], -1).astype(BF16)
        gk_ref[0, rs] = jnp.concatenate([rope(_rms(head(2560, h)) * ggk_ref[...]) for h in range(GQA_KV_HEADS)],
                                        -1).astype(BF16)
        gv_ref[0, rs] = z[:, 2816:3072].astype(BF16)


def _proj_cd(h, msel, win, ggq, ggk, tabs):
    b, nt, d = h.shape
    const = lambda bi, t: (0, 0)
    row = lambda bi, t: (t, 0)
    tok = lambda bi, t: (bi, t, 0)
    widths = (512, 512, 512, 512, 512, 256, 256)
    dtypes = (BF16, BF16, BF16, F32, BF16, BF16, BF16)
    return pl.pallas_call(
        _proj_cd_kernel,
        grid=(b, nt // TM),
        in_specs=[pl.BlockSpec((1, TM, d), tok),
                  pl.BlockSpec((1, 1, 6, d), lambda bi, t: (bi, jnp.minimum(t, 1), 0, 0)),
                  pl.BlockSpec(win.shape, const),
                  pl.BlockSpec(ggq.shape, const),
                  pl.BlockSpec(ggk.shape, const),
                  pl.BlockSpec((TM, 128), row),
                  pl.BlockSpec((TM, 128), row),
                  pl.BlockSpec((TM, 128), row)],
        out_specs=[pl.BlockSpec((1, TM, w), tok) for w in widths],
        out_shape=[jax.ShapeDtypeStruct((b, nt, w), dt) for w, dt in zip(widths, dtypes)],
        compiler_params=_cparams(("parallel", "parallel")),
        name="proj_cd",
    )(h, msel, win, ggq, ggk, *tabs)


def _retention_kernel(q_ref, k_ref, v_ref, g_ref, dl_ref, o_ref, *, n_lat):
    c = RET_CHUNK
    nc = n_lat // c
    lg = jax.nn.log_sigmoid(dl_ref[0])
    lgf, lgb = lg[0:1], lg[1:2]
    tdot = lambda a, b: lax.dot_general(a, b, (((0,), (0,)), ((), ())), preferred_element_type=F32)
    k_all, v_all = k_ref[0], v_ref[0]

    mc = lax.broadcasted_iota(jnp.int32, (CTX_LEN, RET_D), 0).astype(F32)
    kc = k_all[:CTX_LEN].astype(F32)
    vc = v_all[:CTX_LEN]
    s_f = tdot((kc * jnp.exp((CTX_LEN - 1.0 - mc) * lgf)).astype(BF16), vc)
    s_b = tdot((kc * jnp.exp(mc * lgb)).astype(BF16), vc)

    i = lax.broadcasted_iota(jnp.int32, (c, RET_D), 0).astype(F32)
    kd_f = jnp.exp((c - 1.0 - i) * lgf)
    kd_b = jnp.exp(i * lgb)
    qd_f = jnp.exp((i + 1.0) * lgf)
    qd_b = jnp.exp((c - i) * lgb)
    cd_f = jnp.exp(c * lgf)
    cd_b = jnp.exp(c * lgb)
    ii = lax.broadcasted_iota(jnp.int32, (c, c), 0)
    jj = lax.broadcasted_iota(jnp.int32, (c, c), 1)
    dist = (ii - jj).astype(F32)
    lgf_c = jnp.concatenate([lgf] * (c // RET_D), axis=-1)
    lgb_c = jnp.concatenate([lgb] * (c // RET_D), axis=-1)
    dec = (jnp.where(ii >= jj, jnp.exp(jnp.maximum(dist, 0.0) * lgf_c), 0.0)
           + jnp.where(ii <= jj, jnp.exp(jnp.maximum(-dist, 0.0) * lgb_c), 0.0))

    ks = [k_all[CTX_LEN + c * n:CTX_LEN + c * (n + 1)] for n in range(nc)]
    vs = [v_all[CTX_LEN + c * n:CTX_LEN + c * (n + 1)] for n in range(nc)]
    sf = [s_f]
    for n in range(nc - 1):
        sf.append(cd_f * sf[n] + tdot((ks[n].astype(F32) * kd_f).astype(BF16), vs[n]))
    sb = [None] * nc
    sb[nc - 1] = s_b
    for n in range(nc - 1, 0, -1):
        sb[n - 1] = cd_b * sb[n] + tdot((ks[n].astype(F32) * kd_b).astype(BF16), vs[n])

    for n in range(nc):
        rs = slice(CTX_LEN + c * n, CTX_LEN + c * (n + 1))
        qn = q_ref[0, rs]
        att = lax.dot_general(qn, ks[n], (((1,), (1,)), ((), ())), preferred_element_type=F32) * dec
        o = (_dot(att.astype(BF16), vs[n])
             + _dot(qn, sf[n].astype(BF16)) * qd_f
             + _dot(qn, sb[n].astype(BF16)) * qd_b)
        gate = g_ref[0, rs]
        o_ref[0, c * n:c * (n + 1)] = (_rms(o) * (gate * jax.nn.sigmoid(gate))).astype(BF16)


def _retention(rq, rk, rv, rg, decay_logit, n_lat):
    b, nt, _ = rq.shape
    dl = jnp.broadcast_to(jnp.swapaxes(decay_logit, 0, 1)[:, :, None], (RET_HEADS, 2, 128))
    blk = pl.BlockSpec((1, nt, RET_D), lambda bi, h: (bi, 0, h))
    return pl.pallas_call(
        functools.partial(_retention_kernel, n_lat=n_lat),
        grid=(b, RET_HEADS),
        in_specs=[blk, blk, blk, blk, pl.BlockSpec((1, 2, 128), lambda bi, h: (h, 0, 0))],
        out_specs=pl.BlockSpec((1, n_lat, RET_D), lambda bi, h: (bi, 0, h)),
        out_shape=jax.ShapeDtypeStruct((b, n_lat, RET_HEADS * RET_D), BF16),
        compiler_params=_cparams(("parallel", "parallel")),
        name="retention",
    )(rq, rk, rv, rg, dl)


def _ffn_weights(w_up, conv_w, conv_b, w_down):
    d, f2 = w_up.shape
    f = f2 // 2
    nf = f // FFN_CHUNK
    wa = w_up[:, :f].reshape(d, nf, FFN_CHUNK)
    wg = w_up[:, f:].reshape(d, nf, FFN_CHUNK)
    wup = jnp.transpose(jnp.concatenate([wa, wg], axis=-1), (1, 0, 2)).astype(BF16)
    cw = jnp.concatenate([conv_w, conv_b[None], jnp.zeros((4, f), F32)], axis=0)
    cw = jnp.transpose(cw.reshape(8, nf, FFN_CHUNK), (1, 0, 2))
    wdn = w_down.reshape(nf, FFN_CHUNK, d).astype(BF16)
    return wup, cw, wdn


def kernel(x, c, ctx, c_ctx, w_mod, b_mod, w_in_ab, w_out_ab, s5_lam_re, s5_lam_im, s5_log_step, s5_b_re, s5_b_im, s5_c_re, s5_c_im, s5_d, s5_w_glu, s5_b_glu, mla_g_q, mla_w_uq, mla_g_kv, mla_w_ukv, w_in_cd, w_out_cd, ret_decay_logit, gqa_g_q, gqa_g_k, ffn_w_up, ffn_conv_w, ffn_conv_b, ffn_w_down, g_final):
    b, n_lat, d = x.shape
    nt = CTX_LEN + n_lat
    n_tiles = nt // TM

    rows = ((b + 1 + 7) // 8) * 8
    cond = jnp.concatenate([c, c_ctx[None], jnp.zeros((rows - b - 1, d), F32)], axis=0)
    mods = _modulation(cond, w_mod, b_mod).reshape(w_mod.shape[0], rows, 6, d)

    def mod_select(layer):
        mctx = jnp.broadcast_to(mods[layer, b][None], (b, 6, d))
        return jnp.stack([mctx, mods[layer, :b]], axis=1)

    msel = mod_select(0)
    win = jnp.pad(w_in_ab[0], ((0, 0), (0, 64))).astype(BF16)
    wuq = jnp.pad(mla_w_uq[0].reshape(MLA_Q_RANK, MLA_HEADS, MLA_NOPE + MLA_ROPE),
                  ((0, 0), (0, 0), (0, MLA_QPAD - MLA_NOPE - MLA_ROPE))).reshape(MLA_Q_RANK, -1).astype(BF16)
    wukv = mla_w_ukv[0].reshape(MLA_KV_RANK, MLA_HEADS, MLA_NOPE + MLA_V)
    wkn = wukv[:, :, :MLA_NOPE].reshape(MLA_KV_RANK, -1).astype(BF16)
    wv = wukv[:, :, MLA_NOPE:].reshape(MLA_KV_RANK, -1).astype(BF16)
    tabs64 = _rope_tables(n_lat, MLA_ROPE)
    u_tm, q, k, v = _proj_ab(x, ctx, msel, win, mla_g_q[0][None], wuq, mla_g_kv[0][None], wkn, wv, tabs64)

    lam_re, lam_im, bbt_re, bbt_im = _s5_prep(s5_lam_re[0], s5_lam_im[0], s5_log_step[0], s5_b_re[0], s5_b_im[0])
    lam = jnp.stack([lam_re.reshape(2, S5_CH), lam_im.reshape(2, S5_CH)], axis=1)
    bshape = (2, S5_COLS, 8, S5_GROUP, S5_STATE)
    bcat = jnp.concatenate([_block_diag(bbt_re.reshape(bshape)), _block_diag(bbt_im.reshape(bshape))],
                           axis=-1).astype(BF16)
    cshape = (2, S5_OUT_COLS, S5_GROUPS // S5_OUT_COLS, S5_GROUP, S5_STATE)
    cre = _block_diag(jnp.swapaxes(s5_c_re[0].reshape(cshape), -1, -2)).astype(BF16)
    cim = _block_diag(jnp.swapaxes(s5_c_im[0].reshape(cshape), -1, -2)).astype(BF16)
    s5_out = _s5_mixer(u_tm.reshape(nt * b, S5_WIDTH), b, bcat, lam, cre, cim,
                       s5_d[0].reshape(1, S5_WIDTH), s5_w_glu[0].astype(BF16), s5_b_glu[0][None])
    s5_out = s5_out.reshape(nt, b * S5_WIDTH)

    att = _attention(q, k, v, heads=MLA_HEADS, kv_heads=MLA_HEADS, dq=MLA_QPAD, dv=MLA_V,
                     q_tile0=0, n_q_tiles=n_tiles, out_rows=nt)
    wo = w_out_ab[0].astype(BF16)
    seg0 = lambda t: jnp.minimum(t, 1)
    wup, cw, wdn = _ffn_weights(ffn_w_up[0], ffn_conv_w[0], ffn_conv_b[0], ffn_w_down[0])
    gfin = g_final[None]
    h = _mix_ffn((x, ctx), s5_out, att, msel, wo[:S5_WIDTH], wo[S5_WIDTH:], wup, cw, wdn, gfin,
                 dual=True, a_time_major=True, n_tiles=n_tiles, first_tiles=(0, 1),
                 last_tiles=(0, n_tiles - 1), seg_of_tile=seg0, final=False)

    msel = mod_select(1)
    tabs128 = _rope_tables(n_lat, RET_D)
    rq, rk, rv, rg, gq, gk, gv = _proj_cd(h, msel, w_in_cd[0].astype(BF16), gqa_g_q[0][None], gqa_g_k[0][None],
                                          tabs128)
    ret = _retention(rq, rk, rv, rg, ret_decay_logit[0], n_lat)
    att = _attention(gq, gk, gv, heads=GQA_HEADS, kv_heads=GQA_KV_HEADS, dq=GQA_D, dv=GQA_D,
                     q_tile0=1, n_q_tiles=n_tiles - 1, out_rows=n_lat)
    wo = w_out_cd[0].astype(BF16)
    n_half = RET_HEADS * RET_D
    wup, cw, wdn = _ffn_weights(ffn_w_up[1], ffn_conv_w[1], ffn_conv_b[1], ffn_w_down[1])
    return _mix_ffn((h,), ret, att, msel, wo[:n_half], wo[n_half:], wup, cw, wdn, gfin,
                    dual=False, a_time_major=False, n_tiles=n_tiles - 1, first_tiles=(0,),
                    last_tiles=(n_tiles - 2,), seg_of_tile=lambda t: 1, final=True)
```

```python
import functools
import math

import numpy as np
import jax
import jax.numpy as jnp
from jax import lax
from jax.experimental import pallas as pl
from jax.experimental.pallas import tpu as pltpu

F32 = jnp.float32
BF16 = jnp.bfloat16

D_MODEL = 1024
GRID_W = 64
CTX_LEN = 256
EPS = 1e-6
ROPE_THETA = 10000.0

S5_WIDTH = 512
S5_GROUP = 16
S5_GROUPS = 32
S5_STATE = 64
S5_CH = S5_GROUPS * S5_STATE
S5_COLS = 4
S5_OUT_COLS = 2
MLA_HEADS = 4
MLA_NOPE = 128
MLA_ROPE = 64
MLA_V = 128
MLA_Q_RANK = 384
MLA_KV_RANK = 256
MLA_QPAD = 256
RET_HEADS = 4
RET_D = 128
RET_CHUNK = 256
GQA_HEADS = 4
GQA_KV_HEADS = 2
GQA_D = 128
LOG2E = 1.4426950408889634
MLA_QSCALE = (MLA_NOPE + MLA_ROPE) ** -0.5 * LOG2E
GQA_QSCALE = GQA_D ** -0.5 * LOG2E
FFN_HIDDEN = 2816
FFN_CHUNK = 256
CONV_W = 3

TM = 256
HALO = 16
SUB = 128
KEY_SPLIT = 2
SCORE_AHEAD = 2
S5_LC = 32
VMEM_LIMIT = 56 * 1024 * 1024


def _cparams(sem):
    return pltpu.CompilerParams(dimension_semantics=sem, vmem_limit_bytes=VMEM_LIMIT)


def _rms(x):
    return x * lax.rsqrt(jnp.mean(x * x, axis=-1, keepdims=True) + EPS)


def _dot(a, b):
    return jnp.dot(a, b, preferred_element_type=F32)


def _dot_nt(a, b):
    return lax.dot_general(a, b, (((1,), (1,)), ((), ())), preferred_element_type=F32)


def _rope(x, cos, sa, sb, quarter):
    return x * cos + pltpu.roll(x, quarter, 1) * sa + pltpu.roll(x, 128 - quarter, 1) * sb


def _rope_tables(n_lat, dim):
    quarter = dim // 4
    inv = (ROPE_THETA ** (-np.arange(quarter, dtype=np.float32) / quarter)).astype(np.float32)
    t = np.arange(n_lat)
    ang_r = (t // GRID_W).astype(np.float32)[:, None] * inv
    ang_c = (t % GRID_W).astype(np.float32)[:, None] * inv
    ang = np.concatenate([ang_r, ang_r, ang_c, ang_c], axis=-1).astype(np.float32)
    cos, sin = np.cos(ang), np.sin(ang)
    lane_q = (np.arange(dim) // quarter) % 2
    nt = CTX_LEN + n_lat
    cos_t = np.ones((nt, 128), np.float32)
    sa_t = np.zeros((nt, 128), np.float32)
    sb_t = np.zeros((nt, 128), np.float32)
    cos_t[CTX_LEN:, :dim] = cos
    sa_t[CTX_LEN:, :dim] = np.where(lane_q == 1, sin, 0.0)
    sb_t[CTX_LEN:, :dim] = np.where(lane_q == 0, -sin, 0.0)
    return jnp.asarray(cos_t), jnp.asarray(sa_t), jnp.asarray(sb_t)


def _mod_kernel(c_ref, w_ref, b_ref, o_ref):
    c = c_ref[...]
    s = c * jax.nn.sigmoid(c)
    o_ref[0] = _dot(s.astype(BF16), w_ref[0].astype(BF16)) + b_ref[0]


def _modulation(cond, w_mod, b_mod):
    depth, d, n6 = w_mod.shape
    rows = cond.shape[0]
    tn = 1024
    return pl.pallas_call(
        _mod_kernel,
        grid=(depth, n6 // tn),
        in_specs=[pl.BlockSpec((rows, d), lambda l, j: (0, 0)),
                  pl.BlockSpec((1, d, tn), lambda l, j: (l, 0, j)),
                  pl.BlockSpec((1, 1, tn), lambda l, j: (l, 0, j))],
        out_specs=pl.BlockSpec((1, rows, tn), lambda l, j: (l, 0, j)),
        out_shape=jax.ShapeDtypeStruct((depth, rows, n6), F32),
        compiler_params=_cparams(("arbitrary", "arbitrary")),
        name="modulation",
    )(cond, w_mod, b_mod.reshape(depth, 1, n6))


def _stream_rows(t, x_ref, ctx_ref, rs):
    return jnp.where(t == 0, ctx_ref[0, rs], x_ref[0, rs])


def _proj_ab_kernel(x_ref, ctx_ref, m_ref, win_ref, gq_ref, wuq_ref, gkv_ref, wkn_ref, wvt_ref,
                    cos_ref, sa_ref, sb_ref, u_ref, q_ref, k_ref, vt_ref):
    m = m_ref[0, 0]
    t = pl.program_id(1)
    for r in range(TM // SUB):
        rs = slice(SUB * r, SUB * (r + 1))
        xn = _rms(_stream_rows(t, x_ref, ctx_ref, rs)) * (1.0 + m[1:2]) + m[0:1]
        z = _dot(xn.astype(BF16), win_ref[...])
        u_ref[rs] = z[:, :S5_WIDTH]
        cq = z[:, 512:896]
        ckv = z[:, 896:1152]
        kr = z[:, 1152:1280]
        cos, sa, sb = cos_ref[rs], sa_ref[rs], sb_ref[rs]
        q = _dot((_rms(cq) * gq_ref[...]).astype(BF16), wuq_ref[...]) * MLA_QSCALE
        ckvn = (_rms(ckv) * gkv_ref[...]).astype(BF16)
        kn = _dot(ckvn, wkn_ref[...])
        krr = _rope(kr, cos, sa, sb, MLA_ROPE // 4)
        qparts, kparts = [], []
        for h in range(MLA_HEADS):
            o = MLA_QPAD * h
            qparts += [q[:, o:o + 128], _rope(q[:, o + 128:o + 256], cos, sa, sb, MLA_ROPE // 4)]
            kparts += [kn[:, 128 * h:128 * (h + 1)], krr]
        q_ref[0, rs] = jnp.concatenate(qparts, axis=-1).astype(BF16)
        k_ref[0, rs] = jnp.concatenate(kparts, axis=-1).astype(BF16)
        vt_ref[0, :, rs] = _dot_nt(wvt_ref[...], ckvn).astype(BF16)


def _proj_ab(x, ctx, msel, win, gq, wuq, gkv, wkn, wvt, tabs):
    b, n_lat, d = x.shape
    nt = CTX_LEN + n_lat
    const = lambda bi, t: (0, 0)
    row = lambda bi, t: (t, 0)
    tok = lambda bi, t: (bi, t, 0)
    return pl.pallas_call(
        _proj_ab_kernel,
        grid=(b, nt // TM),
        in_specs=[pl.BlockSpec((1, TM, d), lambda bi, t: (bi, jnp.maximum(t - 1, 0), 0)),
                  pl.BlockSpec((1, CTX_LEN, d), lambda bi, t: (bi, 0, 0)),
                  pl.BlockSpec((1, 1, 6, d), lambda bi, t: (bi, jnp.minimum(t, 1), 0, 0)),
                  pl.BlockSpec(win.shape, const),
                  pl.BlockSpec(gq.shape, const),
                  pl.BlockSpec(wuq.shape, const),
                  pl.BlockSpec(gkv.shape, const),
                  pl.BlockSpec(wkn.shape, const),
                  pl.BlockSpec(wvt.shape, const),
                  pl.BlockSpec((TM, 128), row),
                  pl.BlockSpec((TM, 128), row),
                  pl.BlockSpec((TM, 128), row)],
        out_specs=[pl.BlockSpec((TM, S5_WIDTH), lambda bi, t: (t, bi)),
                   pl.BlockSpec((1, TM, MLA_HEADS * MLA_QPAD), tok),
                   pl.BlockSpec((1, TM, MLA_HEADS * MLA_QPAD), tok),
                   pl.BlockSpec((1, MLA_HEADS * MLA_V, TM), lambda bi, t: (bi, 0, t))],
        out_shape=[jax.ShapeDtypeStruct((nt, b * S5_WIDTH), F32),
                   jax.ShapeDtypeStruct((b, nt, MLA_HEADS * MLA_QPAD), BF16),
                   jax.ShapeDtypeStruct((b, nt, MLA_HEADS * MLA_QPAD), BF16),
                   jax.ShapeDtypeStruct((b, MLA_HEADS * MLA_V, nt), BF16)],
        compiler_params=_cparams(("parallel", "parallel")),
        name="proj_ab",
    )(x, ctx, msel, win, gq, wuq, gkv, wkn, wvt, *tabs)


def _s5_prep_kernel(lre_ref, lim_ref, ls_ref, bre_ref, bim_ref, ore_ref, oim_ref, obre_ref, obim_ref):
    lre, lim = lre_ref[...], lim_ref[...]
    dt = jnp.exp(ls_ref[...])
    ar, ai = lre * dt, lim * dt
    mag = jnp.exp(ar)
    lb_re, lb_im = mag * jnp.cos(ai), mag * jnp.sin(ai)
    den = lre * lre + lim * lim
    cf_re = ((lb_re - 1.0) * lre + lb_im * lim) / den
    cf_im = (lb_im * lre - (lb_re - 1.0) * lim) / den
    ore_ref[...] = lb_re
    oim_ref[...] = lb_im
    bre, bim = bre_ref[...], bim_ref[...]
    obre_ref[...] = cf_re * bre - cf_im * bim
    obim_ref[...] = cf_re * bim + cf_im * bre


def _s5_prep(lam_re, lam_im, log_step, b_re, b_im):
    g2 = 2 * S5_GROUPS
    args = (lam_re.reshape(g2, 1, S5_STATE), lam_im.reshape(g2, 1, S5_STATE), log_step.reshape(g2, 1, 1),
            jnp.swapaxes(b_re, -1, -2).reshape(g2, S5_GROUP, S5_STATE),
            jnp.swapaxes(b_im, -1, -2).reshape(g2, S5_GROUP, S5_STATE))
    full = lambda a: pl.BlockSpec(a.shape, lambda i: (0,) * a.ndim)
    lam_sds = jax.ShapeDtypeStruct((g2, 1, S5_STATE), F32)
    bb_sds = jax.ShapeDtypeStruct((g2, S5_GROUP, S5_STATE), F32)
    return pl.pallas_call(
        _s5_prep_kernel,
        grid=(1,),
        in_specs=[full(a) for a in args],
        out_specs=[full(lam_sds), full(lam_sds), full(bb_sds), full(bb_sds)],
        out_shape=[lam_sds, lam_sds, bb_sds, bb_sds],
        name="s5_prep",
    )(*args)


def _block_diag(blocks):
    n, r, c = blocks.shape[-3:]
    eye = jnp.eye(n, dtype=blocks.dtype)
    out = blocks[..., :, :, None, :] * eye[:, None, :, None]
    return out.reshape(blocks.shape[:-3] + (n * r, n * c))


def _s5_chunk(u_ref, bcat_ref, lam_ref, cre_ref, cim_ref, st_ref, bre_ref, bim_ref, lc, nb, reverse):
    u = u_ref[...].astype(BF16)
    for c in range(S5_COLS):
        bu = _dot(u[:, 128 * c:128 * (c + 1)], bcat_ref[c])
        bre_ref[:, 512 * c:512 * (c + 1)] = bu[:, :512]
        bim_ref[:, 512 * c:512 * (c + 1)] = bu[:, 512:]
    order = range(lc - 1, -1, -1) if reverse else range(lc)
    for c in range(S5_COLS):
        cs = slice(512 * c, 512 * (c + 1))
        lre = jnp.broadcast_to(lam_ref[0:1, cs], (nb, 512))
        lim = jnp.broadcast_to(lam_ref[1:2, cs], (nb, 512))
        sre, sim = st_ref[0, :, cs], st_ref[1, :, cs]
        for t in order:
            rs = slice(t * nb, (t + 1) * nb)
            nre = lre * sre - lim * sim + bre_ref[rs, cs]
            nim = lre * sim + lim * sre + bim_ref[rs, cs]
            bre_ref[rs, cs] = nre
            bim_ref[rs, cs] = nim
            sre, sim = nre, nim
        st_ref[0, :, cs] = sre
        st_ref[1, :, cs] = sim
    ys = []
    for c in range(S5_OUT_COLS):
        cs = slice(1024 * c, 1024 * (c + 1))
        ys.append(_dot(bre_ref[:, cs].astype(BF16), cre_ref[c]) - _dot(bim_ref[:, cs].astype(BF16), cim_ref[c]))
    return jnp.concatenate(ys, axis=-1)


def _s5_fwd_kernel(u_ref, bcat_ref, lam_ref, cre_ref, cim_ref, y_ref, st_ref, bre_ref, bim_ref, *, lc, nb):
    @pl.when(pl.program_id(0) == 0)
    def _():
        st_ref[...] = jnp.zeros(st_ref.shape, F32)

    y_ref[...] = _s5_chunk(u_ref, bcat_ref, lam_ref, cre_ref, cim_ref, st_ref, bre_ref, bim_ref, lc, nb, False)


def _s5_bwd_kernel(u_ref, yf_ref, bcat_ref, lam_ref, cre_ref, cim_ref, d_ref, wglu_ref, bglu_ref, o_ref,
                   st_ref, bre_ref, bim_ref, *, lc, nb):
    @pl.when(pl.program_id(0) == 0)
    def _():
        st_ref[...] = jnp.zeros(st_ref.shape, F32)

    yb = _s5_chunk(u_ref, bcat_ref, lam_ref, cre_ref, cim_ref, st_ref, bre_ref, bim_ref, lc, nb, True)
    y = d_ref[...] * u_ref[...] + yf_ref[...] + yb
    g = jax.nn.gelu(y)
    o_ref[...] = (g * jax.nn.sigmoid(_dot(g.astype(BF16), wglu_ref[...]) + bglu_ref[...])).astype(BF16)


def _s5_mixer(u_tm, nb, bcat, lam, cre, cim, d_skip, w_glu, b_glu):
    rows_total = u_tm.shape[0]
    lc = S5_LC
    rows = lc * nb
    n_steps = rows_total // rows
    n_ctx = CTX_LEN // lc
    scratch = [pltpu.VMEM((2, nb, S5_CH), F32), pltpu.VMEM((rows, S5_CH), F32), pltpu.VMEM((rows, S5_CH), F32)]
    fwd_blk = pl.BlockSpec((rows, S5_WIDTH), lambda i: (i, 0))
    yf = pl.pallas_call(
        functools.partial(_s5_fwd_kernel, lc=lc, nb=nb),
        grid=(n_steps,),
        in_specs=[fwd_blk] + [pl.BlockSpec((None,) + a.shape[1:], lambda i, n=a.ndim: (0,) * n)
                              for a in (bcat, lam, cre, cim)],
        out_specs=fwd_blk,
        out_shape=jax.ShapeDtypeStruct((rows_total, S5_WIDTH), F32),
        scratch_shapes=scratch,
        compiler_params=_cparams(("arbitrary",)),
        name="s5_fwd",
    )(u_tm, bcat, lam, cre, cim)

    def chunk(i):
        return jnp.where(i < n_ctx, n_ctx - 1 - i, n_steps - 1 - (i - n_ctx))

    bwd_blk = pl.BlockSpec((rows, S5_WIDTH), lambda i: (chunk(i), 0))
    whole = lambda a: pl.BlockSpec(a.shape, lambda i, n=a.ndim: (0,) * n)
    return pl.pallas_call(
        functools.partial(_s5_bwd_kernel, lc=lc, nb=nb),
        grid=(n_steps,),
        in_specs=[bwd_blk, bwd_blk] + [pl.BlockSpec((None,) + a.shape[1:], lambda i, n=a.ndim: (1,) + (0,) * (n - 1))
                                       for a in (bcat, lam, cre, cim)]
                 + [whole(d_skip), whole(w_glu), whole(b_glu)],
        out_specs=bwd_blk,
        out_shape=jax.ShapeDtypeStruct((rows_total, S5_WIDTH), BF16),
        scratch_shapes=scratch,
        compiler_params=_cparams(("arbitrary",)),
        name="s5_bwd_glu",
    )(u_tm, yf, bcat, lam, cre, cim, d_skip, w_glu, b_glu)


def _attn_kernel(q_ref, k_ref, vt_ref, o_ref, *, heads, rep, dq, dv, n_ctx_keys, ctx_tile):
    def run(nk):
        nb = nk // KEY_SPLIT
        kb = [slice(nb * i, nb * (i + 1)) for i in range(KEY_SPLIT)]

        def scores(h):
            g = h // rep
            q = q_ref[0, :, dq * h:dq * (h + 1)]
            return [_dot_nt(k_ref[0, r, dq * g:dq * (g + 1)], q) for r in kb]

        outs = []
        pending = [scores(h) for h in range(min(SCORE_AHEAD, heads))]
        for h in range(heads):
            ss = pending.pop(0)
            if h + SCORE_AHEAD < heads:
                pending.append(scores(h + SCORE_AHEAD))
            m = functools.reduce(jnp.maximum, [jnp.max(s, axis=0, keepdims=True) for s in ss])
            ps = [jnp.exp2(s - m) for s in ss]
            l = sum(jnp.sum(p, axis=0, keepdims=True) for p in ps)
            g = h // rep
            ot = sum(_dot(vt_ref[0, dv * g:dv * (g + 1), r], p.astype(BF16)) for r, p in zip(kb, ps))
            outs.append((ot / l).T)
        o_ref[0] = jnp.concatenate(outs, axis=-1).astype(o_ref.dtype)

    nk_all = k_ref.shape[1]
    if ctx_tile:
        qi = pl.program_id(1)

        @pl.when(qi == 0)
        def _():
            run(n_ctx_keys)

        @pl.when(qi > 0)
        def _():
            run(nk_all)
    else:
        run(nk_all)


def _attention(q, k, vt, *, heads, kv_heads, dq, dv, q_tile0, n_q_tiles, out_rows):
    b, nt, _ = q.shape
    return pl.pallas_call(
        functools.partial(_attn_kernel, heads=heads, rep=heads // kv_heads, dq=dq, dv=dv,
                          n_ctx_keys=CTX_LEN, ctx_tile=(q_tile0 == 0)),
        grid=(b, n_q_tiles),
        in_specs=[pl.BlockSpec((1, TM, heads * dq), lambda bi, qi: (bi, qi + q_tile0, 0)),
                  pl.BlockSpec((1, nt, kv_heads * dq), lambda bi, qi: (bi, 0, 0)),
                  pl.BlockSpec((1, kv_heads * dv, nt), lambda bi, qi: (bi, 0, 0))],
        out_specs=pl.BlockSpec((1, TM, heads * dv), lambda bi, qi: (bi, qi, 0)),
        out_shape=jax.ShapeDtypeStruct((b, out_rows, heads * dv), BF16),
        compiler_params=_cparams(("parallel", "arbitrary")),
        name="attention",
    )(q, k, vt)


def _mix_ffn_kernel(*refs, dual, a_time_major, first_tiles, last_tiles, final):
    if dual:
        x_ref, ctx_ref, *refs = refs
    else:
        (x_ref, *refs), ctx_ref = refs, None
    (hp_ref, hq_ref, a_ref, ap_ref, aq_ref, b_ref, bp_ref, bq_ref, m_ref, wa_ref, wb_ref,
     wup_ref, cw_ref, cb_ref, wdn_ref, gfin_ref, o_ref, la_ref, lb_ref, h1_ref, xe_ref, acc_ref) = refs
    t = pl.program_id(1)
    has_prev = functools.reduce(jnp.logical_and, [t != ft for ft in first_tiles])
    has_next = functools.reduce(jnp.logical_and, [t != lt for lt in last_tiles])
    m = m_ref[0, 0]
    ext = TM + 2 * HALO
    ld = (lambda r: r[...]) if a_time_major else (lambda r: r[0])
    la_ref[0:TM] = ld(a_ref)
    la_ref[TM:TM + HALO] = ld(ap_ref)
    la_ref[TM + HALO:ext] = ld(aq_ref)
    lb_ref[0:TM] = b_ref[0]
    lb_ref[TM:TM + HALO] = bp_ref[0]
    lb_ref[TM + HALO:ext] = bq_ref[0]
    h_main = _stream_rows(t, x_ref, ctx_ref, slice(0, TM)) if dual else x_ref[0]
    h_ext = jnp.concatenate([h_main, hp_ref[0], hq_ref[0]], axis=0)
    for r in range(2):
        rs = slice(r * ext // 2, (r + 1) * ext // 2)
        y = _dot(la_ref[rs], wa_ref[...]) + _dot(lb_ref[rs], wb_ref[...])
        h1 = h_ext[rs] + m[2:3] * y
        h1_ref[rs] = h1
        xe_ref[rs] = (_rms(h1) * (1.0 + m[4:5]) + m[3:4]).astype(BF16)

    xe = xe_ref[...]
    x = xe[:TM]
    row = lax.broadcasted_iota(jnp.int32, (TM, FFN_CHUNK), 0)
    f = wdn_ref.shape[0]
    n_chunks = f // FFN_CHUNK
    cols = lambda j: slice(FFN_CHUNK * j, FFN_CHUNK * (j + 1))

    def up(j):
        return (_dot(xe, wup_ref[:, cols(j)]),
                _dot(x, wup_ref[:, f + FFN_CHUNK * j:f + FFN_CHUNK * (j + 1)]))

    nxt = up(0)
    for j in range(n_chunks):
        ae, g = nxt
        if j + 1 < n_chunks:
            nxt = up(j + 1)
        a = ae[:TM]
        a_prev = jnp.where(has_prev, ae[TM + HALO - 1:TM + HALO], 0.0)
        a_next = jnp.where(has_next, ae[TM + HALO:TM + HALO + 1], 0.0)
        a_dn = jnp.where(row == 0, a_prev, pltpu.roll(a, 1, 0))
        a_up = jnp.where(row == TM - 1, a_next, pltpu.roll(a, TM - 1, 0))
        cw = cw_ref[:, cols(j)]
        conv = cb_ref[:, cols(j)] + a_dn * cw[0:1] + a * cw[1:2] + a_up * cw[2:3]
        hid = (jax.nn.gelu(conv) * g).astype(BF16)
        part = _dot(hid, wdn_ref[cols(j), :])
        if j == 0:
            acc_ref[...] = part
        else:
            acc_ref[...] += part
    out = h1_ref[0:TM] + m[5:6] * acc_ref[...]
    if final:
        out = _rms(out) * gfin_ref[...]
    o_ref[0] = out


def _mix_ffn(hs, a, bmix, msel, wa, wb, wup, cw, cb, wdn, gfin, *, dual, a_time_major, n_tiles,
             first_tiles, last_tiles, seg_of_tile, final):
    b, _, d = hs[0].shape
    hb = TM // HALO
    ext = TM + 2 * HALO
    c2 = lambda bi, t: (0, 0)
    single = dict(pipeline_mode=pl.Buffered(1))
    prev_blk = lambda t: jnp.maximum(t * hb - 1, 0)
    next_blk = lambda t, n_rows: jnp.minimum((t + 1) * hb, n_rows // HALO - 1)
    if dual:
        n_x = hs[0].shape[1]
        h_specs = [pl.BlockSpec((1, TM, d), lambda bi, t: (bi, jnp.maximum(t - 1, 0), 0)),
                   pl.BlockSpec((1, CTX_LEN, d), lambda bi, t: (bi, 0, 0)),
                   pl.BlockSpec((1, HALO, d), lambda bi, t: (bi, prev_blk(jnp.maximum(t - 1, 0)), 0)),
                   pl.BlockSpec((1, HALO, d), lambda bi, t: (bi, next_blk(jnp.maximum(t - 1, 0), n_x), 0))]
        h_args = [hs[0], hs[1], hs[0], hs[0]]
    else:
        n_h = hs[0].shape[1]
        h_specs = [pl.BlockSpec((1, TM, d), lambda bi, t: (bi, t + 1, 0)),
                   pl.BlockSpec((1, HALO, d), lambda bi, t: (bi, prev_blk(t + 1), 0)),
                   pl.BlockSpec((1, HALO, d), lambda bi, t: (bi, next_blk(t + 1, n_h), 0))]
        h_args = [hs[0], hs[0], hs[0]]

    def mixer_specs(arr, time_major):
        if time_major:
            w, n_rows = arr.shape[1] // b, arr.shape[0]
            return [pl.BlockSpec((TM, w), lambda bi, t: (t, bi)),
                    pl.BlockSpec((HALO, w), lambda bi, t: (prev_blk(t), bi)),
                    pl.BlockSpec((HALO, w), lambda bi, t: (next_blk(t, n_rows), bi))]
        w, n_rows = arr.shape[2], arr.shape[1]
        return [pl.BlockSpec((1, TM, w), lambda bi, t: (bi, t, 0)),
                pl.BlockSpec((1, HALO, w), lambda bi, t: (bi, prev_blk(t), 0)),
                pl.BlockSpec((1, HALO, w), lambda bi, t: (bi, next_blk(t, n_rows), 0))]

    return pl.pallas_call(
        functools.partial(_mix_ffn_kernel, dual=dual, a_time_major=a_time_major, first_tiles=first_tiles,
                          last_tiles=last_tiles, final=final),
        grid=(b, n_tiles),
        in_specs=h_specs + mixer_specs(a, a_time_major) + mixer_specs(bmix, False)
                 + [pl.BlockSpec((1, 1, 6, d), lambda bi, t: (bi, seg_of_tile(t), 0, 0)),
                    pl.BlockSpec(wa.shape, c2, **single),
                    pl.BlockSpec(wb.shape, c2, **single),
                    pl.BlockSpec(wup.shape, c2, **single),
                    pl.BlockSpec(cw.shape, c2, **single),
                    pl.BlockSpec(cb.shape, c2, **single),
                    pl.BlockSpec(wdn.shape, c2, **single),
                    pl.BlockSpec(gfin.shape, c2)],
        out_specs=pl.BlockSpec((1, TM, d), lambda bi, t: (bi, t, 0)),
        out_shape=jax.ShapeDtypeStruct((b, n_tiles * TM, d), F32),
        scratch_shapes=[pltpu.VMEM((ext, wa.shape[0]), BF16), pltpu.VMEM((ext, wb.shape[0]), BF16),
                        pltpu.VMEM((ext, d), F32), pltpu.VMEM((ext, d), BF16), pltpu.VMEM((TM, d), F32)],
        compiler_params=_cparams(("parallel", "parallel")),
        name="mix_ffn",
    )(*h_args, a, a, a, bmix, bmix, bmix, msel, wa, wb, wup, cw, cb, wdn, gfin)


def _proj_cd_kernel(h_ref, m_ref, win_ref, wgvt_ref, ggq_ref, ggk_ref, cos_ref, sa_ref, sb_ref,
                    rq_ref, rk_ref, rv_ref, rg_ref, gq_ref, gk_ref, gvt_ref):
    m = m_ref[0, 0]
    k_scale = RET_D ** -0.5
    for r in range(TM // SUB):
        rs = slice(SUB * r, SUB * (r + 1))
        xn = (_rms(h_ref[0, rs]) * (1.0 + m[1:2]) + m[0:1]).astype(BF16)
        z = _dot(xn, win_ref[...])
        gvt_ref[0, :, rs] = _dot_nt(wgvt_ref[...], xn).astype(BF16)
        cos, sa, sb = cos_ref[rs], sa_ref[rs], sb_ref[rs]
        rope = lambda x: _rope(x, cos, sa, sb, RET_D // 4)
        head = lambda base, h: z[:, base + 128 * h:base + 128 * (h + 1)]
        rq_ref[0, rs] = jnp.concatenate([rope(head(0, h)) for h in range(RET_HEADS)], -1).astype(BF16)
        rk_ref[0, rs] = jnp.concatenate([rope(head(512, h)) * k_scale for h in range(RET_HEADS)], -1).astype(BF16)
        rv_ref[0, rs] = z[:, 1024:1536].astype(BF16)
        rg_ref[0, rs] = z[:, 1536:2048]
        gq_ref[0, rs] = jnp.concatenate([rope(_rms(head(2048, h)) * ggq_ref[...]) * GQA_QSCALE
                                         for h in range(GQA_HEADS)], -1).astype(BF16)
        gk_ref[0, rs] = jnp.concatenate([rope(_rms(head(2560, h)) * ggk_ref[...]) for h in range(GQA_KV_HEADS)],
                                        -1).astype(BF16)


def _proj_cd(h, msel, win, wgvt, ggq, ggk, tabs):
    b, nt, d = h.shape
    const = lambda bi, t: (0, 0)
    row = lambda bi, t: (t, 0)
    tok = lambda bi, t: (bi, t, 0)
    widths = (512, 512, 512, 512, 512, 256)
    dtypes = (BF16, BF16, BF16, F32, BF16, BF16)
    n_gv = wgvt.shape[0]
    return pl.pallas_call(
        _proj_cd_kernel,
        grid=(b, nt // TM),
        in_specs=[pl.BlockSpec((1, TM, d), tok),
                  pl.BlockSpec((1, 1, 6, d), lambda bi, t: (bi, jnp.minimum(t, 1), 0, 0)),
                  pl.BlockSpec(win.shape, const),
                  pl.BlockSpec(wgvt.shape, const),
                  pl.BlockSpec(ggq.shape, const),
                  pl.BlockSpec(ggk.shape, const),
                  pl.BlockSpec((TM, 128), row),
                  pl.BlockSpec((TM, 128), row),
                  pl.BlockSpec((TM, 128), row)],
        out_specs=[pl.BlockSpec((1, TM, w), tok) for w in widths]
                  + [pl.BlockSpec((1, n_gv, TM), lambda bi, t: (bi, 0, t))],
        out_shape=[jax.ShapeDtypeStruct((b, nt, w), dt) for w, dt in zip(widths, dtypes)]
                  + [jax.ShapeDtypeStruct((b, n_gv, nt), BF16)],
        compiler_params=_cparams(("parallel", "parallel")),
        name="proj_cd",
    )(h, msel, win, wgvt, ggq, ggk, *tabs)


def _retention_kernel(q_ref, k_ref, v_ref, g_ref, dl_ref, o_ref, *, n_lat):
    c = RET_CHUNK
    nc = n_lat // c
    lg = jax.nn.log_sigmoid(dl_ref[0])
    lgf, lgb = lg[0:1], lg[1:2]
    tdot = lambda a, b: lax.dot_general(a, b, (((0,), (0,)), ((), ())), preferred_element_type=F32)
    k_all, v_all = k_ref[0], v_ref[0]

    mc = lax.broadcasted_iota(jnp.int32, (CTX_LEN, RET_D), 0).astype(F32)
    kc = k_all[:CTX_LEN].astype(F32)
    vc = v_all[:CTX_LEN]
    s_f = tdot((kc * jnp.exp((CTX_LEN - 1.0 - mc) * lgf)).astype(BF16), vc)
    s_b = tdot((kc * jnp.exp(mc * lgb)).astype(BF16), vc)

    i = lax.broadcasted_iota(jnp.int32, (c, RET_D), 0).astype(F32)
    kd_f = jnp.exp((c - 1.0 - i) * lgf)
    kd_b = jnp.exp(i * lgb)
    qd_f = jnp.exp((i + 1.0) * lgf)
    qd_b = jnp.exp((c - i) * lgb)
    cd_f = jnp.exp(c * lgf)
    cd_b = jnp.exp(c * lgb)
    ii = lax.broadcasted_iota(jnp.int32, (c, c), 0)
    jj = lax.broadcasted_iota(jnp.int32, (c, c), 1)
    dist = (ii - jj).astype(F32)
    lgf_c = jnp.concatenate([lgf] * (c // RET_D), axis=-1)
    lgb_c = jnp.concatenate([lgb] * (c // RET_D), axis=-1)
    dec = (jnp.where(ii >= jj, jnp.exp(jnp.maximum(dist, 0.0) * lgf_c), 0.0)
           + jnp.where(ii <= jj, jnp.exp(jnp.maximum(-dist, 0.0) * lgb_c), 0.0))

    ks = [k_all[CTX_LEN + c * n:CTX_LEN + c * (n + 1)] for n in range(nc)]
    vs = [v_all[CTX_LEN + c * n:CTX_LEN + c * (n + 1)] for n in range(nc)]
    sf = [s_f]
    for n in range(nc - 1):
        sf.append(cd_f * sf[n] + tdot((ks[n].astype(F32) * kd_f).astype(BF16), vs[n]))
    sb = [None] * nc
    sb[nc - 1] = s_b
    for n in range(nc - 1, 0, -1):
        sb[n - 1] = cd_b * sb[n] + tdot((ks[n].astype(F32) * kd_b).astype(BF16), vs[n])

    for n in range(nc):
        rs = slice(CTX_LEN + c * n, CTX_LEN + c * (n + 1))
        qn = q_ref[0, rs]
        att = lax.dot_general(qn, ks[n], (((1,), (1,)), ((), ())), preferred_element_type=F32) * dec
        o = (_dot(att.astype(BF16), vs[n])
             + _dot(qn, sf[n].astype(BF16)) * qd_f
             + _dot(qn, sb[n].astype(BF16)) * qd_b)
        gate = g_ref[0, rs]
        o_ref[0, c * n:c * (n + 1)] = (_rms(o) * (gate * jax.nn.sigmoid(gate))).astype(BF16)


def _retention(rq, rk, rv, rg, decay_logit, n_lat):
    b, nt, _ = rq.shape
    dl = jnp.broadcast_to(jnp.swapaxes(decay_logit, 0, 1)[:, :, None], (RET_HEADS, 2, 128))
    blk = pl.BlockSpec((1, nt, RET_D), lambda bi, h: (bi, 0, h))
    return pl.pallas_call(
        functools.partial(_retention_kernel, n_lat=n_lat),
        grid=(b, RET_HEADS),
        in_specs=[blk, blk, blk, blk, pl.BlockSpec((1, 2, 128), lambda bi, h: (h, 0, 0))],
        out_specs=pl.BlockSpec((1, n_lat, RET_D), lambda bi, h: (bi, 0, h)),
        out_shape=jax.ShapeDtypeStruct((b, n_lat, RET_HEADS * RET_D), BF16),
        compiler_params=_cparams(("parallel", "parallel")),
        name="retention",
    )(rq, rk, rv, rg, dl)


def kernel(x, c, ctx, c_ctx, w_mod, b_mod, w_in_ab, w_out_ab, s5_lam_re, s5_lam_im, s5_log_step, s5_b_re, s5_b_im, s5_c_re, s5_c_im, s5_d, s5_w_glu, s5_b_glu, mla_g_q, mla_w_uq, mla_g_kv, mla_w_ukv, w_in_cd, w_out_cd, ret_decay_logit, gqa_g_q, gqa_g_k, ffn_w_up, ffn_conv_w, ffn_conv_b, ffn_w_down, g_final):
    b, n_lat, d = x.shape
    nt = CTX_LEN + n_lat
    n_tiles = nt // TM

    rows = ((b + 1 + 7) // 8) * 8
    cond = jnp.concatenate([c, c_ctx[None], jnp.zeros((rows - b - 1, d), F32)], axis=0)
    mods = _modulation(cond, w_mod, b_mod).reshape(w_mod.shape[0], rows, 6, d)

    def mod_select(layer):
        mctx = jnp.broadcast_to(mods[layer, b][None], (b, 6, d))
        return jnp.stack([mctx, mods[layer, :b]], axis=1)

    msel = mod_select(0)
    win = jnp.pad(w_in_ab[0], ((0, 0), (0, 64))).astype(BF16)
    wuq = jnp.pad(mla_w_uq[0].reshape(MLA_Q_RANK, MLA_HEADS, MLA_NOPE + MLA_ROPE),
                  ((0, 0), (0, 0), (0, MLA_QPAD - MLA_NOPE - MLA_ROPE))).reshape(MLA_Q_RANK, -1).astype(BF16)
    wukv = mla_w_ukv[0].reshape(MLA_KV_RANK, MLA_HEADS, MLA_NOPE + MLA_V)
    wkn = wukv[:, :, :MLA_NOPE].reshape(MLA_KV_RANK, -1).astype(BF16)
    wvt = jnp.transpose(wukv[:, :, MLA_NOPE:].reshape(MLA_KV_RANK, -1)).astype(BF16)
    tabs64 = _rope_tables(n_lat, MLA_ROPE)
    u_tm, q, k, vt = _proj_ab(x, ctx, msel, win, mla_g_q[0][None], wuq, mla_g_kv[0][None], wkn, wvt, tabs64)

    lam_re, lam_im, bbt_re, bbt_im = _s5_prep(s5_lam_re[0], s5_lam_im[0], s5_log_step[0], s5_b_re[0], s5_b_im[0])
    lam = jnp.stack([lam_re.reshape(2, S5_CH), lam_im.reshape(2, S5_CH)], axis=1)
    bshape = (2, S5_COLS, 8, S5_GROUP, S5_STATE)
    bcat = jnp.concatenate([_block_diag(bbt_re.reshape(bshape)), _block_diag(bbt_im.reshape(bshape))],
                           axis=-1).astype(BF16)
    cshape = (2, S5_OUT_COLS, S5_GROUPS // S5_OUT_COLS, S5_GROUP, S5_STATE)
    cre = _block_diag(jnp.swapaxes(s5_c_re[0].reshape(cshape), -1, -2)).astype(BF16)
    cim = _block_diag(jnp.swapaxes(s5_c_im[0].reshape(cshape), -1, -2)).astype(BF16)
    s5_out = _s5_mixer(u_tm.reshape(nt * b, S5_WIDTH), b, bcat, lam, cre, cim,
                       s5_d[0].reshape(1, S5_WIDTH), s5_w_glu[0].astype(BF16), s5_b_glu[0][None])
    s5_out = s5_out.reshape(nt, b * S5_WIDTH)

    att = _attention(q, k, vt, heads=MLA_HEADS, kv_heads=MLA_HEADS, dq=MLA_QPAD, dv=MLA_V,
                     q_tile0=0, n_q_tiles=n_tiles, out_rows=nt)
    wo = w_out_ab[0].astype(BF16)
    seg0 = lambda t: jnp.minimum(t, 1)
    gfin = g_final[None]
    h = _mix_ffn((x, ctx), s5_out, att, msel, wo[:S5_WIDTH], wo[S5_WIDTH:], ffn_w_up[0].astype(BF16),
                 ffn_conv_w[0], ffn_conv_b[0][None], ffn_w_down[0].astype(BF16), gfin,
                 dual=True, a_time_major=True, n_tiles=n_tiles, first_tiles=(0, 1),
                 last_tiles=(0, n_tiles - 1), seg_of_tile=seg0, final=False)

    msel = mod_select(1)
    tabs128 = _rope_tables(n_lat, RET_D)
    n_gv = GQA_KV_HEADS * GQA_D
    wcd = w_in_cd[0]
    rq, rk, rv, rg, gq, gk, gvt = _proj_cd(h, msel, wcd[:, :-n_gv].astype(BF16),
                                           jnp.transpose(wcd[:, -n_gv:]).astype(BF16),
                                           gqa_g_q[0][None], gqa_g_k[0][None], tabs128)
    ret = _retention(rq, rk, rv, rg, ret_decay_logit[0], n_lat)
    att = _attention(gq, gk, gvt, heads=GQA_HEADS, kv_heads=GQA_KV_HEADS, dq=GQA_D, dv=GQA_D,
                     q_tile0=1, n_q_tiles=n_tiles - 1, out_rows=n_lat)
    wo = w_out_cd[0].astype(BF16)
    n_half = RET_HEADS * RET_D
    return _mix_ffn((h,), ret, att, msel, wo[:n_half], wo[n_half:], ffn_w_up[1].astype(BF16),
                    ffn_conv_w[1], ffn_conv_b[1][None], ffn_w_down[1].astype(BF16), gfin,
                    dual=False, a_time_major=False, n_tiles=n_tiles - 1, first_tiles=(0,),
                    last_tiles=(n_tiles - 2,), seg_of_tile=lambda t: 1, final=True)
```

```python
import functools
import math

import numpy as np
import jax
import jax.numpy as jnp
from jax import lax
from jax.experimental import pallas as pl
from jax.experimental.pallas import tpu as pltpu

F32 = jnp.float32
BF16 = jnp.bfloat16

D_MODEL = 1024
GRID_W = 64
CTX_LEN = 256
EPS = 1e-6
ROPE_THETA = 10000.0

S5_WIDTH = 512
S5_GROUP = 16
S5_GROUPS = 32
S5_STATE = 64
S5_CH = S5_GROUPS * S5_STATE
S5_COLS = 4
S5_OUT_COLS = 2
MLA_HEADS = 4
MLA_NOPE = 128
MLA_ROPE = 64
MLA_V = 128
MLA_Q_RANK = 384
MLA_KV_RANK = 256
MLA_QPAD = 256
RET_HEADS = 4
RET_D = 128
RET_CHUNK = 256
GQA_HEADS = 4
GQA_KV_HEADS = 2
GQA_D = 128
LOG2E = 1.4426950408889634
MLA_QSCALE = (MLA_NOPE + MLA_ROPE) ** -0.5 * LOG2E
GQA_QSCALE = GQA_D ** -0.5 * LOG2E
FFN_HIDDEN = 2816
FFN_CHUNK = 256
CONV_W = 3

TM = 256
HALO = 16
BP = 2
KEY_SPLIT = 2
SCORE_AHEAD = 2
S5_LC = 32
VMEM_LIMIT = 56 * 1024 * 1024


def _cparams(sem):
    return pltpu.CompilerParams(dimension_semantics=sem, vmem_limit_bytes=VMEM_LIMIT)


def _rms(x):
    return x * lax.rsqrt(jnp.mean(x * x, axis=-1, keepdims=True) + EPS)


def _dot(a, b):
    return jnp.dot(a, b, preferred_element_type=F32)


def _dot_nt(a, b):
    return lax.dot_general(a, b, (((1,), (1,)), ((), ())), preferred_element_type=F32)


def _rope(x, cos, sa, sb, quarter):
    return x * cos + pltpu.roll(x, quarter, 1) * sa + pltpu.roll(x, 128 - quarter, 1) * sb


def _rope_tables(n_lat, dim):
    quarter = dim // 4
    inv = (ROPE_THETA ** (-np.arange(quarter, dtype=np.float32) / quarter)).astype(np.float32)
    t = np.arange(n_lat)
    ang_r = (t // GRID_W).astype(np.float32)[:, None] * inv
    ang_c = (t % GRID_W).astype(np.float32)[:, None] * inv
    ang = np.concatenate([ang_r, ang_r, ang_c, ang_c], axis=-1).astype(np.float32)
    cos, sin = np.cos(ang), np.sin(ang)
    lane_q = (np.arange(dim) // quarter) % 2
    nt = CTX_LEN + n_lat
    cos_t = np.ones((nt, 128), np.float32)
    sa_t = np.zeros((nt, 128), np.float32)
    sb_t = np.zeros((nt, 128), np.float32)
    cos_t[CTX_LEN:, :dim] = cos
    sa_t[CTX_LEN:, :dim] = np.where(lane_q == 1, sin, 0.0)
    sb_t[CTX_LEN:, :dim] = np.where(lane_q == 0, -sin, 0.0)
    return jnp.asarray(cos_t), jnp.asarray(sa_t), jnp.asarray(sb_t)


def _mod_kernel(c_ref, w_ref, b_ref, o_ref):
    c = c_ref[...]
    s = c * jax.nn.sigmoid(c)
    o_ref[0] = _dot(s.astype(BF16), w_ref[0].astype(BF16)) + b_ref[0]


def _modulation(cond, w_mod, b_mod):
    depth, d, n6 = w_mod.shape
    rows = cond.shape[0]
    tn = 1024
    return pl.pallas_call(
        _mod_kernel,
        grid=(depth, n6 // tn),
        in_specs=[pl.BlockSpec((rows, d), lambda l, j: (0, 0)),
                  pl.BlockSpec((1, d, tn), lambda l, j: (l, 0, j)),
                  pl.BlockSpec((1, 1, tn), lambda l, j: (l, 0, j))],
        out_specs=pl.BlockSpec((1, rows, tn), lambda l, j: (l, 0, j)),
        out_shape=jax.ShapeDtypeStruct((depth, rows, n6), F32),
        compiler_params=_cparams(("arbitrary", "arbitrary")),
        name="modulation",
    )(cond, w_mod, b_mod.reshape(depth, 1, n6))


def _stream_rows(t, x_ref, ctx_ref, j):
    return jnp.where(t == 0, ctx_ref[j], x_ref[j])


def _proj_ab_kernel(x_ref, ctx_ref, m_ref, win_ref, gq_ref, wuq_ref, gkv_ref, wkn_ref, wvt_ref,
                    cos_ref, sa_ref, sb_ref, u_ref, q_ref, k_ref, vt_ref):
    t = pl.program_id(1)
    cos, sa, sb = cos_ref[...], sa_ref[...], sb_ref[...]
    ms = [m_ref[j, 0] for j in range(BP)]
    xns = [(_rms(_stream_rows(t, x_ref, ctx_ref, j)) * (1.0 + m[1:2]) + m[0:1]).astype(BF16)
           for j, m in enumerate(ms)]
    zs = [_dot(xn, win_ref[...]) for xn in xns]
    cqns = [(_rms(z[:, 512:896]) * gq_ref[...]).astype(BF16) for z in zs]
    ckvns = [(_rms(z[:, 896:1152]) * gkv_ref[...]).astype(BF16) for z in zs]
    qs = [_dot(cqn, wuq_ref[...]) * MLA_QSCALE for cqn in cqns]
    kns = [_dot(ckvn, wkn_ref[...]) for ckvn in ckvns]
    vts = [_dot_nt(wvt_ref[...], ckvn) for ckvn in ckvns]
    for j, (z, q, kn, vt) in enumerate(zip(zs, qs, kns, vts)):
        u_ref[:, S5_WIDTH * j:S5_WIDTH * (j + 1)] = z[:, :S5_WIDTH]
        krr = _rope(z[:, 1152:1280], cos, sa, sb, MLA_ROPE // 4)
        qparts, kparts = [], []
        for h in range(MLA_HEADS):
            o = MLA_QPAD * h
            qparts += [q[:, o:o + 128], _rope(q[:, o + 128:o + 256], cos, sa, sb, MLA_ROPE // 4)]
            kparts += [kn[:, 128 * h:128 * (h + 1)], krr]
        q_ref[j] = jnp.concatenate(qparts, axis=-1).astype(BF16)
        k_ref[j] = jnp.concatenate(kparts, axis=-1).astype(BF16)
        vt_ref[j] = vt.astype(BF16)


def _proj_ab(x, ctx, msel, win, gq, wuq, gkv, wkn, wvt, tabs):
    b, n_lat, d = x.shape
    nt = CTX_LEN + n_lat
    const = lambda bi, t: (0, 0)
    row = lambda bi, t: (t, 0)
    tok = lambda bi, t: (bi, t, 0)
    return pl.pallas_call(
        _proj_ab_kernel,
        grid=(b // BP, nt // TM),
        in_specs=[pl.BlockSpec((BP, TM, d), lambda bi, t: (bi, jnp.maximum(t - 1, 0), 0)),
                  pl.BlockSpec((BP, CTX_LEN, d), lambda bi, t: (bi, 0, 0)),
                  pl.BlockSpec((BP, 1, 6, d), lambda bi, t: (bi, jnp.minimum(t, 1), 0, 0)),
                  pl.BlockSpec(win.shape, const),
                  pl.BlockSpec(gq.shape, const),
                  pl.BlockSpec(wuq.shape, const),
                  pl.BlockSpec(gkv.shape, const),
                  pl.BlockSpec(wkn.shape, const),
                  pl.BlockSpec(wvt.shape, const),
                  pl.BlockSpec((TM, 128), row),
                  pl.BlockSpec((TM, 128), row),
                  pl.BlockSpec((TM, 128), row)],
        out_specs=[pl.BlockSpec((TM, BP * S5_WIDTH), lambda bi, t: (t, bi)),
                   pl.BlockSpec((BP, TM, MLA_HEADS * MLA_QPAD), tok),
                   pl.BlockSpec((BP, TM, MLA_HEADS * MLA_QPAD), tok),
                   pl.BlockSpec((BP, MLA_HEADS * MLA_V, TM), lambda bi, t: (bi, 0, t))],
        out_shape=[jax.ShapeDtypeStruct((nt, b * S5_WIDTH), F32),
                   jax.ShapeDtypeStruct((b, nt, MLA_HEADS * MLA_QPAD), BF16),
                   jax.ShapeDtypeStruct((b, nt, MLA_HEADS * MLA_QPAD), BF16),
                   jax.ShapeDtypeStruct((b, MLA_HEADS * MLA_V, nt), BF16)],
        compiler_params=_cparams(("parallel", "parallel")),
        name="proj_ab",
    )(x, ctx, msel, win, gq, wuq, gkv, wkn, wvt, *tabs)


def _s5_prep_kernel(lre_ref, lim_ref, ls_ref, bre_ref, bim_ref, ore_ref, oim_ref, obre_ref, obim_ref):
    lre, lim = lre_ref[...], lim_ref[...]
    dt = jnp.exp(ls_ref[...])
    ar, ai = lre * dt, lim * dt
    mag = jnp.exp(ar)
    lb_re, lb_im = mag * jnp.cos(ai), mag * jnp.sin(ai)
    den = lre * lre + lim * lim
    cf_re = ((lb_re - 1.0) * lre + lb_im * lim) / den
    cf_im = (lb_im * lre - (lb_re - 1.0) * lim) / den
    ore_ref[...] = lb_re
    oim_ref[...] = lb_im
    bre, bim = bre_ref[...], bim_ref[...]
    obre_ref[...] = cf_re * bre - cf_im * bim
    obim_ref[...] = cf_re * bim + cf_im * bre


def _s5_prep(lam_re, lam_im, log_step, b_re, b_im):
    g2 = 2 * S5_GROUPS
    args = (lam_re.reshape(g2, 1, S5_STATE), lam_im.reshape(g2, 1, S5_STATE), log_step.reshape(g2, 1, 1),
            jnp.swapaxes(b_re, -1, -2).reshape(g2, S5_GROUP, S5_STATE),
            jnp.swapaxes(b_im, -1, -2).reshape(g2, S5_GROUP, S5_STATE))
    full = lambda a: pl.BlockSpec(a.shape, lambda i: (0,) * a.ndim)
    lam_sds = jax.ShapeDtypeStruct((g2, 1, S5_STATE), F32)
    bb_sds = jax.ShapeDtypeStruct((g2, S5_GROUP, S5_STATE), F32)
    return pl.pallas_call(
        _s5_prep_kernel,
        grid=(1,),
        in_specs=[full(a) for a in args],
        out_specs=[full(lam_sds), full(lam_sds), full(bb_sds), full(bb_sds)],
        out_shape=[lam_sds, lam_sds, bb_sds, bb_sds],
        name="s5_prep",
    )(*args)


def _block_diag(blocks):
    n, r, c = blocks.shape[-3:]
    eye = jnp.eye(n, dtype=blocks.dtype)
    out = blocks[..., :, :, None, :] * eye[:, None, :, None]
    return out.reshape(blocks.shape[:-3] + (n * r, n * c))


def _s5_chunk(u_ref, bcat_ref, lam_ref, cre_ref, cim_ref, scratch, lc, nb, reverse):
    st_ref, bre_ref, bim_ref, sre_ref, sim_ref = scratch
    u = u_ref[...].astype(BF16)
    for c in range(S5_COLS):
        bu = _dot(u[:, 128 * c:128 * (c + 1)], bcat_ref[c])
        bre_ref[:, 512 * c:512 * (c + 1)] = bu[:, :512]
        bim_ref[:, 512 * c:512 * (c + 1)] = bu[:, 512:]
    order = range(lc - 1, -1, -1) if reverse else range(lc)
    for c in range(S5_COLS):
        cs = slice(512 * c, 512 * (c + 1))
        lre = jnp.broadcast_to(lam_ref[0:1, cs], (nb, 512))
        lim = jnp.broadcast_to(lam_ref[1:2, cs], (nb, 512))
        sre, sim = st_ref[0, :, cs], st_ref[1, :, cs]
        for t in order:
            rs = slice(t * nb, (t + 1) * nb)
            nre = lre * sre - lim * sim + bre_ref[rs, cs]
            nim = lre * sim + lim * sre + bim_ref[rs, cs]
            sre_ref[rs, cs] = nre.astype(BF16)
            sim_ref[rs, cs] = nim.astype(BF16)
            sre, sim = nre, nim
        st_ref[0, :, cs] = sre
        st_ref[1, :, cs] = sim
    ys = []
    for c in range(S5_OUT_COLS):
        cs = slice(1024 * c, 1024 * (c + 1))
        ys.append(_dot(sre_ref[:, cs], cre_ref[c]) - _dot(sim_ref[:, cs], cim_ref[c]))
    return jnp.concatenate(ys, axis=-1)


def _s5_fwd_kernel(u_ref, bcat_ref, lam_ref, cre_ref, cim_ref, y_ref, *scratch, lc, nb):
    @pl.when(pl.program_id(0) == 0)
    def _():
        scratch[0][...] = jnp.zeros(scratch[0].shape, F32)

    y_ref[...] = _s5_chunk(u_ref, bcat_ref, lam_ref, cre_ref, cim_ref, scratch, lc, nb, False)


def _s5_bwd_kernel(u_ref, yf_ref, bcat_ref, lam_ref, cre_ref, cim_ref, d_ref, wglu_ref, bglu_ref, o_ref,
                   *scratch, lc, nb):
    @pl.when(pl.program_id(0) == 0)
    def _():
        scratch[0][...] = jnp.zeros(scratch[0].shape, F32)

    yb = _s5_chunk(u_ref, bcat_ref, lam_ref, cre_ref, cim_ref, scratch, lc, nb, True)
    y = d_ref[...] * u_ref[...] + yf_ref[...] + yb
    g = jax.nn.gelu(y)
    o_ref[...] = (g * jax.nn.sigmoid(_dot(g.astype(BF16), wglu_ref[...]) + bglu_ref[...])).astype(BF16)


def _s5_mixer(u_tm, nb, bcat, lam, cre, cim, d_skip, w_glu, b_glu):
    rows_total = u_tm.shape[0]
    lc = S5_LC
    rows = lc * nb
    n_steps = rows_total // rows
    n_ctx = CTX_LEN // lc
    scratch = [pltpu.VMEM((2, nb, S5_CH), F32), pltpu.VMEM((rows, S5_CH), F32), pltpu.VMEM((rows, S5_CH), F32),
               pltpu.VMEM((rows, S5_CH), BF16), pltpu.VMEM((rows, S5_CH), BF16)]
    fwd_blk = pl.BlockSpec((rows, S5_WIDTH), lambda i: (i, 0))
    yf = pl.pallas_call(
        functools.partial(_s5_fwd_kernel, lc=lc, nb=nb),
        grid=(n_steps,),
        in_specs=[fwd_blk] + [pl.BlockSpec((None,) + a.shape[1:], lambda i, n=a.ndim: (0,) * n)
                              for a in (bcat, lam, cre, cim)],
        out_specs=fwd_blk,
        out_shape=jax.ShapeDtypeStruct((rows_total, S5_WIDTH), F32),
        scratch_shapes=scratch,
        compiler_params=_cparams(("arbitrary",)),
        name="s5_fwd",
    )(u_tm, bcat, lam, cre, cim)

    def chunk(i):
        return jnp.where(i < n_ctx, n_ctx - 1 - i, n_steps - 1 - (i - n_ctx))

    bwd_blk = pl.BlockSpec((rows, S5_WIDTH), lambda i: (chunk(i), 0))
    whole = lambda a: pl.BlockSpec(a.shape, lambda i, n=a.ndim: (0,) * n)
    return pl.pallas_call(
        functools.partial(_s5_bwd_kernel, lc=lc, nb=nb),
        grid=(n_steps,),
        in_specs=[bwd_blk, bwd_blk] + [pl.BlockSpec((None,) + a.shape[1:], lambda i, n=a.ndim: (1,) + (0,) * (n - 1))
                                       for a in (bcat, lam, cre, cim)]
                 + [whole(d_skip), whole(w_glu), whole(b_glu)],
        out_specs=bwd_blk,
        out_shape=jax.ShapeDtypeStruct((rows_total, S5_WIDTH), BF16),
        scratch_shapes=scratch,
        compiler_params=_cparams(("arbitrary",)),
        name="s5_bwd_glu",
    )(u_tm, yf, bcat, lam, cre, cim, d_skip, w_glu, b_glu)


def _attn_kernel(q_ref, k_ref, vt_ref, o_ref, *, heads, rep, dq, dv, n_ctx_keys, ctx_tile):
    def run(nk):
        nb = nk // KEY_SPLIT
        kb = [slice(nb * i, nb * (i + 1)) for i in range(KEY_SPLIT)]
        units = [(j, h) for j in range(BP) for h in range(heads)]

        def scores(j, h):
            g = h // rep
            q = q_ref[j, :, dq * h:dq * (h + 1)]
            return [_dot_nt(k_ref[j, r, dq * g:dq * (g + 1)], q) for r in kb]

        outs = [[] for _ in range(BP)]
        pending = [scores(*u) for u in units[:SCORE_AHEAD]]
        for i, (j, h) in enumerate(units):
            ss = pending.pop(0)
            if i + SCORE_AHEAD < len(units):
                pending.append(scores(*units[i + SCORE_AHEAD]))
            m = functools.reduce(jnp.maximum, [jnp.max(s, axis=0, keepdims=True) for s in ss])
            ps = [jnp.exp2(s - m) for s in ss]
            l = sum(jnp.sum(p, axis=0, keepdims=True) for p in ps)
            g = h // rep
            ot = sum(_dot(vt_ref[j, dv * g:dv * (g + 1), r], p.astype(BF16)) for r, p in zip(kb, ps))
            outs[j].append((ot / l).T)
        for j in range(BP):
            o_ref[j] = jnp.concatenate(outs[j], axis=-1).astype(o_ref.dtype)

    nk_all = k_ref.shape[1]
    if ctx_tile:
        qi = pl.program_id(1)

        @pl.when(qi == 0)
        def _():
            run(n_ctx_keys)

        @pl.when(qi > 0)
        def _():
            run(nk_all)
    else:
        run(nk_all)


def _attention(q, k, vt, *, heads, kv_heads, dq, dv, q_tile0, n_q_tiles, out_rows):
    b, nt, _ = q.shape
    return pl.pallas_call(
        functools.partial(_attn_kernel, heads=heads, rep=heads // kv_heads, dq=dq, dv=dv,
                          n_ctx_keys=CTX_LEN, ctx_tile=(q_tile0 == 0)),
        grid=(b // BP, n_q_tiles),
        in_specs=[pl.BlockSpec((BP, TM, heads * dq), lambda bi, qi: (bi, qi + q_tile0, 0)),
                  pl.BlockSpec((BP, nt, kv_heads * dq), lambda bi, qi: (bi, 0, 0)),
                  pl.BlockSpec((BP, kv_heads * dv, nt), lambda bi, qi: (bi, 0, 0))],
        out_specs=pl.BlockSpec((BP, TM, heads * dv), lambda bi, qi: (bi, qi, 0)),
        out_shape=jax.ShapeDtypeStruct((b, out_rows, heads * dv), BF16),
        compiler_params=_cparams(("parallel", "arbitrary")),
        name="attention",
    )(q, k, vt)


def _mix_ffn_kernel(*refs, dual, a_time_major, first_tiles, last_tiles, final):
    if dual:
        x_ref, ctx_ref, *refs = refs
    else:
        (x_ref, *refs), ctx_ref = refs, None
    (hp_ref, hq_ref, a_ref, ap_ref, aq_ref, b_ref, bp_ref, bq_ref, m_ref, wa_ref, wb_ref,
     wup_ref, cw_ref, cb_ref, wdn_ref, gfin_ref, o_ref, la_ref, lb_ref, h1_ref, xe_ref, acc_ref) = refs
    t = pl.program_id(1)
    has_prev = functools.reduce(jnp.logical_and, [t != ft for ft in first_tiles])
    has_next = functools.reduce(jnp.logical_and, [t != lt for lt in last_tiles])
    n_main = BP * TM
    main = lambda j: slice(TM * j, TM * (j + 1))
    prev = lambda j: slice(n_main + 2 * HALO * j, n_main + 2 * HALO * j + HALO)
    nxt_ = lambda j: slice(n_main + 2 * HALO * j + HALO, n_main + 2 * HALO * (j + 1))
    wa_w = wa_ref.shape[0]
    lda = (lambda r, j: r[:, wa_w * j:wa_w * (j + 1)]) if a_time_major else (lambda r, j: r[j])
    for j in range(BP):
        la_ref[main(j)], la_ref[prev(j)], la_ref[nxt_(j)] = lda(a_ref, j), lda(ap_ref, j), lda(aq_ref, j)
        lb_ref[main(j)], lb_ref[prev(j)], lb_ref[nxt_(j)] = b_ref[j], bp_ref[j], bq_ref[j]
    ms = [m_ref[j, 0] for j in range(BP)]
    y_lo = _dot(la_ref[:TM], wa_ref[...]) + _dot(lb_ref[:TM], wb_ref[...])
    y_hi = _dot(la_ref[TM:], wa_ref[...]) + _dot(lb_ref[TM:], wb_ref[...])
    y_rows = lambda r: y_lo[r] if r.stop <= TM else y_hi[r.start - TM:r.stop - TM]

    for j, m in enumerate(ms):
        h_main = _stream_rows(t, x_ref, ctx_ref, j) if dual else x_ref[j]
        for r, h_in in ((main(j), h_main), (prev(j), hp_ref[j]), (nxt_(j), hq_ref[j])):
            h1 = h_in + m[2:3] * y_rows(r)
            h1_ref[r] = h1
            xe_ref[r] = (_rms(h1) * (1.0 + m[4:5]) + m[3:4]).astype(BF16)

    xe = xe_ref[...]
    x = xe[:n_main]
    row = lax.broadcasted_iota(jnp.int32, (TM, FFN_CHUNK), 0)
    f = wdn_ref.shape[0]
    n_chunks = f // FFN_CHUNK
    cols = lambda c: slice(FFN_CHUNK * c, FFN_CHUNK * (c + 1))

    def up(c):
        return (_dot(xe, wup_ref[:, cols(c)]),
                _dot(x, wup_ref[:, f + FFN_CHUNK * c:f + FFN_CHUNK * (c + 1)]))

    pending = up(0)
    for c in range(n_chunks):
        ae, g = pending
        if c + 1 < n_chunks:
            pending = up(c + 1)
        cw = cw_ref[:, cols(c)]
        hids = []
        for j in range(BP):
            a = ae[main(j)]
            a_prev = jnp.where(has_prev, ae[prev(j).stop - 1:prev(j).stop], 0.0)
            a_next = jnp.where(has_next, ae[nxt_(j).start:nxt_(j).start + 1], 0.0)
            a_dn = jnp.where(row == 0, a_prev, pltpu.roll(a, 1, 0))
            a_up = jnp.where(row == TM - 1, a_next, pltpu.roll(a, TM - 1, 0))
            conv = cb_ref[:, cols(c)] + a_dn * cw[0:1] + a * cw[1:2] + a_up * cw[2:3]
            hids.append((jax.nn.gelu(conv) * g[main(j)]).astype(BF16))
        part = _dot(jnp.concatenate(hids, axis=0), wdn_ref[cols(c), :])
        if c == 0:
            acc_ref[...] = part
        else:
            acc_ref[...] += part
    for j, m in enumerate(ms):
        out = h1_ref[main(j)] + m[5:6] * acc_ref[main(j)]
        if final:
            out = _rms(out) * gfin_ref[...]
        o_ref[j] = out


def _mix_ffn(hs, a, bmix, msel, wa, wb, wup, cw, cb, wdn, gfin, *, dual, a_time_major, n_tiles,
             first_tiles, last_tiles, seg_of_tile, final):
    b, _, d = hs[0].shape
    hb = TM // HALO
    ext = BP * (TM + 2 * HALO)
    c2 = lambda bi, t: (0, 0)
    single = dict(pipeline_mode=pl.Buffered(1))
    prev_blk = lambda t: jnp.maximum(t * hb - 1, 0)
    next_blk = lambda t, n_rows: jnp.minimum((t + 1) * hb, n_rows // HALO - 1)
    if dual:
        n_x = hs[0].shape[1]
        h_specs = [pl.BlockSpec((BP, TM, d), lambda bi, t: (bi, jnp.maximum(t - 1, 0), 0)),
                   pl.BlockSpec((BP, CTX_LEN, d), lambda bi, t: (bi, 0, 0)),
                   pl.BlockSpec((BP, HALO, d), lambda bi, t: (bi, prev_blk(jnp.maximum(t - 1, 0)), 0)),
                   pl.BlockSpec((BP, HALO, d), lambda bi, t: (bi, next_blk(jnp.maximum(t - 1, 0), n_x), 0))]
        h_args = [hs[0], hs[1], hs[0], hs[0]]
    else:
        n_h = hs[0].shape[1]
        h_specs = [pl.BlockSpec((BP, TM, d), lambda bi, t: (bi, t + 1, 0)),
                   pl.BlockSpec((BP, HALO, d), lambda bi, t: (bi, prev_blk(t + 1), 0)),
                   pl.BlockSpec((BP, HALO, d), lambda bi, t: (bi, next_blk(t + 1, n_h), 0))]
        h_args = [hs[0], hs[0], hs[0]]

    def mixer_specs(arr, time_major):
        if time_major:
            w, n_rows = BP * (arr.shape[1] // b), arr.shape[0]
            return [pl.BlockSpec((TM, w), lambda bi, t: (t, bi)),
                    pl.BlockSpec((HALO, w), lambda bi, t: (prev_blk(t), bi)),
                    pl.BlockSpec((HALO, w), lambda bi, t: (next_blk(t, n_rows), bi))]
        w, n_rows = arr.shape[2], arr.shape[1]
        return [pl.BlockSpec((BP, TM, w), lambda bi, t: (bi, t, 0)),
                pl.BlockSpec((BP, HALO, w), lambda bi, t: (bi, prev_blk(t), 0)),
                pl.BlockSpec((BP, HALO, w), lambda bi, t: (bi, next_blk(t, n_rows), 0))]

    return pl.pallas_call(
        functools.partial(_mix_ffn_kernel, dual=dual, a_time_major=a_time_major, first_tiles=first_tiles,
                          last_tiles=last_tiles, final=final),
        grid=(b // BP, n_tiles),
        in_specs=h_specs + mixer_specs(a, a_time_major) + mixer_specs(bmix, False)
                 + [pl.BlockSpec((BP, 1, 6, d), lambda bi, t: (bi, seg_of_tile(t), 0, 0)),
                    pl.BlockSpec(wa.shape, c2, **single),
                    pl.BlockSpec(wb.shape, c2, **single),
                    pl.BlockSpec(wup.shape, c2, **single),
                    pl.BlockSpec(cw.shape, c2, **single),
                    pl.BlockSpec(cb.shape, c2, **single),
                    pl.BlockSpec(wdn.shape, c2, **single),
                    pl.BlockSpec(gfin.shape, c2)],
        out_specs=pl.BlockSpec((BP, TM, d), lambda bi, t: (bi, t, 0)),
        out_shape=jax.ShapeDtypeStruct((b, n_tiles * TM, d), F32),
        scratch_shapes=[pltpu.VMEM((ext, wa.shape[0]), BF16), pltpu.VMEM((ext, wb.shape[0]), BF16),
                        pltpu.VMEM((ext, d), F32), pltpu.VMEM((ext, d), BF16), pltpu.VMEM((BP * TM, d), F32)],
        compiler_params=_cparams(("parallel", "parallel")),
        name="mix_ffn",
    )(*h_args, a, a, a, bmix, bmix, bmix, msel, wa, wb, wup, cw, cb, wdn, gfin)


def _proj_cd_kernel(h_ref, m_ref, win_ref, wgvt_ref, ggq_ref, ggk_ref, cos_ref, sa_ref, sb_ref,
                    rq_ref, rk_ref, rv_ref, rg_ref, gq_ref, gk_ref, gvt_ref):
    k_scale = RET_D ** -0.5
    n_ret = 4 * RET_HEADS * RET_D
    rope = functools.partial(_rope, cos=cos_ref[...], sa=sa_ref[...], sb=sb_ref[...], quarter=RET_D // 4)
    head = lambda z, base, h: z[:, base + 128 * h:base + 128 * (h + 1)]
    ms = [m_ref[j, 0] for j in range(BP)]
    xns = [(_rms(h_ref[j]) * (1.0 + m[1:2]) + m[0:1]).astype(BF16) for j, m in enumerate(ms)]
    z_gqa = [_dot(xn, win_ref[:, n_ret:]) for xn in xns]
    z_rqk = [_dot(xn, win_ref[:, :n_ret // 2]) for xn in xns]
    z_rvg = [_dot(xn, win_ref[:, n_ret // 2:n_ret]) for xn in xns]
    gvts = [_dot_nt(wgvt_ref[...], xn) for xn in xns]
    for j, z in enumerate(z_gqa):
        gq_ref[j] = jnp.concatenate([rope(_rms(head(z, 0, h)) * ggq_ref[...]) * GQA_QSCALE
                                     for h in range(GQA_HEADS)], -1).astype(BF16)
        gk_ref[j] = jnp.concatenate([rope(_rms(head(z, 512, h)) * ggk_ref[...]) for h in range(GQA_KV_HEADS)],
                                    -1).astype(BF16)
    for j, z in enumerate(z_rqk):
        rq_ref[j] = jnp.concatenate([rope(head(z, 0, h)) for h in range(RET_HEADS)], -1).astype(BF16)
        rk_ref[j] = jnp.concatenate([rope(head(z, 512, h)) * k_scale for h in range(RET_HEADS)], -1).astype(BF16)
    for j, (z, gvt) in enumerate(zip(z_rvg, gvts)):
        rv_ref[j] = z[:, :512].astype(BF16)
        rg_ref[j] = z[:, 512:]
        gvt_ref[j] = gvt.astype(BF16)


def _proj_cd(h, msel, win, wgvt, ggq, ggk, tabs):
    b, nt, d = h.shape
    const = lambda bi, t: (0, 0)
    row = lambda bi, t: (t, 0)
    tok = lambda bi, t: (bi, t, 0)
    widths = (512, 512, 512, 512, 512, 256)
    dtypes = (BF16, BF16, BF16, F32, BF16, BF16)
    n_gv = wgvt.shape[0]
    return pl.pallas_call(
        _proj_cd_kernel,
        grid=(b // BP, nt // TM),
        in_specs=[pl.BlockSpec((BP, TM, d), tok),
                  pl.BlockSpec((BP, 1, 6, d), lambda bi, t: (bi, jnp.minimum(t, 1), 0, 0)),
                  pl.BlockSpec(win.shape, const),
                  pl.BlockSpec(wgvt.shape, const),
                  pl.BlockSpec(ggq.shape, const),
                  pl.BlockSpec(ggk.shape, const),
                  pl.BlockSpec((TM, 128), row),
                  pl.BlockSpec((TM, 128), row),
                  pl.BlockSpec((TM, 128), row)],
        out_specs=[pl.BlockSpec((BP, TM, w), tok) for w in widths]
                  + [pl.BlockSpec((BP, n_gv, TM), lambda bi, t: (bi, 0, t))],
        out_shape=[jax.ShapeDtypeStruct((b, nt, w), dt) for w, dt in zip(widths, dtypes)]
                  + [jax.ShapeDtypeStruct((b, n_gv, nt), BF16)],
        compiler_params=_cparams(("parallel", "parallel")),
        name="proj_cd",
    )(h, msel, win, wgvt, ggq, ggk, *tabs)


def _retention_kernel(q_ref, k_ref, v_ref, g_ref, dl_ref, o_ref, *, n_lat):
    c = RET_CHUNK
    nc = n_lat // c
    lg = jax.nn.log_sigmoid(dl_ref[0])
    lgf, lgb = lg[0:1], lg[1:2]
    tdot = lambda a, b: lax.dot_general(a, b, (((0,), (0,)), ((), ())), preferred_element_type=F32)
    k_all, v_all = k_ref[0], v_ref[0]

    mc = lax.broadcasted_iota(jnp.int32, (CTX_LEN, RET_D), 0).astype(F32)
    kc = k_all[:CTX_LEN].astype(F32)
    vc = v_all[:CTX_LEN]
    s_f = tdot((kc * jnp.exp((CTX_LEN - 1.0 - mc) * lgf)).astype(BF16), vc)
    s_b = tdot((kc * jnp.exp(mc * lgb)).astype(BF16), vc)

    i = lax.broadcasted_iota(jnp.int32, (c, RET_D), 0).astype(F32)
    kd_f = jnp.exp((c - 1.0 - i) * lgf)
    kd_b = jnp.exp(i * lgb)
    qd_f = jnp.exp((i + 1.0) * lgf)
    qd_b = jnp.exp((c - i) * lgb)
    cd_f = jnp.exp(c * lgf)
    cd_b = jnp.exp(c * lgb)
    ii = lax.broadcasted_iota(jnp.int32, (c, c), 0)
    jj = lax.broadcasted_iota(jnp.int32, (c, c), 1)
    dist = (ii - jj).astype(F32)
    lgf_c = jnp.concatenate([lgf] * (c // RET_D), axis=-1)
    lgb_c = jnp.concatenate([lgb] * (c // RET_D), axis=-1)
    dec = (jnp.where(ii >= jj, jnp.exp(jnp.maximum(dist, 0.0) * lgf_c), 0.0)
           + jnp.where(ii <= jj, jnp.exp(jnp.maximum(-dist, 0.0) * lgb_c), 0.0))

    ks = [k_all[CTX_LEN + c * n:CTX_LEN + c * (n + 1)] for n in range(nc)]
    vs = [v_all[CTX_LEN + c * n:CTX_LEN + c * (n + 1)] for n in range(nc)]
    sf = [s_f]
    for n in range(nc - 1):
        sf.append(cd_f * sf[n] + tdot((ks[n].astype(F32) * kd_f).astype(BF16), vs[n]))
    sb = [None] * nc
    sb[nc - 1] = s_b
    for n in range(nc - 1, 0, -1):
        sb[n - 1] = cd_b * sb[n] + tdot((ks[n].astype(F32) * kd_b).astype(BF16), vs[n])

    for n in range(nc):
        rs = slice(CTX_LEN + c * n, CTX_LEN + c * (n + 1))
        qn = q_ref[0, rs]
        att = lax.dot_general(qn, ks[n], (((1,), (1,)), ((), ())), preferred_element_type=F32) * dec
        o = (_dot(att.astype(BF16), vs[n])
             + _dot(qn, sf[n].astype(BF16)) * qd_f
             + _dot(qn, sb[n].astype(BF16)) * qd_b)
        gate = g_ref[0, rs]
        o_ref[0, c * n:c * (n + 1)] = (_rms(o) * (gate * jax.nn.sigmoid(gate))).astype(BF16)


def _retention(rq, rk, rv, rg, decay_logit, n_lat):
    b, nt, _ = rq.shape
    dl = jnp.broadcast_to(jnp.swapaxes(decay_logit, 0, 1)[:, :, None], (RET_HEADS, 2, 128))
    blk = pl.BlockSpec((1, nt, RET_D), lambda bi, h: (bi, 0, h))
    return pl.pallas_call(
        functools.partial(_retention_kernel, n_lat=n_lat),
        grid=(b, RET_HEADS),
        in_specs=[blk, blk, blk, blk, pl.BlockSpec((1, 2, 128), lambda bi, h: (h, 0, 0))],
        out_specs=pl.BlockSpec((1, n_lat, RET_D), lambda bi, h: (bi, 0, h)),
        out_shape=jax.ShapeDtypeStruct((b, n_lat, RET_HEADS * RET_D), BF16),
        compiler_params=_cparams(("parallel", "parallel")),
        name="retention",
    )(rq, rk, rv, rg, dl)


def kernel(x, c, ctx, c_ctx, w_mod, b_mod, w_in_ab, w_out_ab, s5_lam_re, s5_lam_im, s5_log_step, s5_b_re, s5_b_im, s5_c_re, s5_c_im, s5_d, s5_w_glu, s5_b_glu, mla_g_q, mla_w_uq, mla_g_kv, mla_w_ukv, w_in_cd, w_out_cd, ret_decay_logit, gqa_g_q, gqa_g_k, ffn_w_up, ffn_conv_w, ffn_conv_b, ffn_w_down, g_final):
    b, n_lat, d = x.shape
    nt = CTX_LEN + n_lat
    n_tiles = nt // TM

    rows = ((b + 1 + 7) // 8) * 8
    cond = jnp.concatenate([c, c_ctx[None], jnp.zeros((rows - b - 1, d), F32)], axis=0)
    mods = _modulation(cond, w_mod, b_mod).reshape(w_mod.shape[0], rows, 6, d)

    def mod_select(layer):
        mctx = jnp.broadcast_to(mods[layer, b][None], (b, 6, d))
        return jnp.stack([mctx, mods[layer, :b]], axis=1)

    msel = mod_select(0)
    win = jnp.pad(w_in_ab[0], ((0, 0), (0, 64))).astype(BF16)
    wuq = jnp.pad(mla_w_uq[0].reshape(MLA_Q_RANK, MLA_HEADS, MLA_NOPE + MLA_ROPE),
                  ((0, 0), (0, 0), (0, MLA_QPAD - MLA_NOPE - MLA_ROPE))).reshape(MLA_Q_RANK, -1).astype(BF16)
    wukv = mla_w_ukv[0].reshape(MLA_KV_RANK, MLA_HEADS, MLA_NOPE + MLA_V)
    wkn = wukv[:, :, :MLA_NOPE].reshape(MLA_KV_RANK, -1).astype(BF16)
    wvt = jnp.transpose(wukv[:, :, MLA_NOPE:].reshape(MLA_KV_RANK, -1)).astype(BF16)
    tabs64 = _rope_tables(n_lat, MLA_ROPE)
    u_tm, q, k, vt = _proj_ab(x, ctx, msel, win, mla_g_q[0][None], wuq, mla_g_kv[0][None], wkn, wvt, tabs64)

    lam_re, lam_im, bbt_re, bbt_im = _s5_prep(s5_lam_re[0], s5_lam_im[0], s5_log_step[0], s5_b_re[0], s5_b_im[0])
    lam = jnp.stack([lam_re.reshape(2, S5_CH), lam_im.reshape(2, S5_CH)], axis=1)
    bshape = (2, S5_COLS, 8, S5_GROUP, S5_STATE)
    bcat = jnp.concatenate([_block_diag(bbt_re.reshape(bshape)), _block_diag(bbt_im.reshape(bshape))],
                           axis=-1).astype(BF16)
    cshape = (2, S5_OUT_COLS, S5_GROUPS // S5_OUT_COLS, S5_GROUP, S5_STATE)
    cre = _block_diag(jnp.swapaxes(s5_c_re[0].reshape(cshape), -1, -2)).astype(BF16)
    cim = _block_diag(jnp.swapaxes(s5_c_im[0].reshape(cshape), -1, -2)).astype(BF16)
    s5_out = _s5_mixer(u_tm.reshape(nt * b, S5_WIDTH), b, bcat, lam, cre, cim,
                       s5_d[0].reshape(1, S5_WIDTH), s5_w_glu[0].astype(BF16), s5_b_glu[0][None])
    s5_out = s5_out.reshape(nt, b * S5_WIDTH)

    att = _attention(q, k, vt, heads=MLA_HEADS, kv_heads=MLA_HEADS, dq=MLA_QPAD, dv=MLA_V,
                     q_tile0=0, n_q_tiles=n_tiles, out_rows=nt)
    wo = w_out_ab[0].astype(BF16)
    seg0 = lambda t: jnp.minimum(t, 1)
    gfin = g_final[None]
    h = _mix_ffn((x, ctx), s5_out, att, msel, wo[:S5_WIDTH], wo[S5_WIDTH:], ffn_w_up[0].astype(BF16),
                 ffn_conv_w[0], ffn_conv_b[0][None], ffn_w_down[0].astype(BF16), gfin,
                 dual=True, a_time_major=True, n_tiles=n_tiles, first_tiles=(0, 1),
                 last_tiles=(0, n_tiles - 1), seg_of_tile=seg0, final=False)

    msel = mod_select(1)
    tabs128 = _rope_tables(n_lat, RET_D)
    n_gv = GQA_KV_HEADS * GQA_D
    wcd = w_in_cd[0]
    rq, rk, rv, rg, gq, gk, gvt = _proj_cd(h, msel, wcd[:, :-n_gv].astype(BF16),
                                           jnp.transpose(wcd[:, -n_gv:]).astype(BF16),
                                           gqa_g_q[0][None], gqa_g_k[0][None], tabs128)
    ret = _retention(rq, rk, rv, rg, ret_decay_logit[0], n_lat)
    att = _attention(gq, gk, gvt, heads=GQA_HEADS, kv_heads=GQA_KV_HEADS, dq=GQA_D, dv=GQA_D,
                     q_tile0=1, n_q_tiles=n_tiles - 1, out_rows=n_lat)
    wo = w_out_cd[0].astype(BF16)
    n_half = RET_HEADS * RET_D
    return _mix_ffn((h,), ret, att, msel, wo[:n_half], wo[n_half:], ffn_w_up[1].astype(BF16),
                    ffn_conv_w[1], ffn_conv_b[1][None], ffn_w_down[1].astype(BF16), gfin,
                    dual=False, a_time_major=False, n_tiles=n_tiles - 1, first_tiles=(0,),
                    last_tiles=(n_tiles - 2,), seg_of_tile=lambda t: 1, final=True)
```

```python
import functools
import math

import numpy as np
import jax
import jax.numpy as jnp
from jax import lax
from jax.experimental import pallas as pl
from jax.experimental.pallas import tpu as pltpu

F32 = jnp.float32
BF16 = jnp.bfloat16

D_MODEL = 1024
GRID_W = 64
CTX_LEN = 256
EPS = 1e-6
ROPE_THETA = 10000.0

S5_WIDTH = 512
S5_GROUP = 16
S5_GROUPS = 32
S5_STATE = 64
S5_CH = S5_GROUPS * S5_STATE
S5_COLS = 4
S5_OUT_COLS = 2
MLA_HEADS = 4
MLA_NOPE = 128
MLA_ROPE = 64
MLA_V = 128
MLA_Q_RANK = 384
MLA_KV_RANK = 256
MLA_QPAD = 256
RET_HEADS = 4
RET_D = 128
RET_CHUNK = 256
GQA_HEADS = 4
GQA_KV_HEADS = 2
GQA_D = 128
LOG2E = 1.4426950408889634
MLA_QSCALE = (MLA_NOPE + MLA_ROPE) ** -0.5 * LOG2E
GQA_QSCALE = GQA_D ** -0.5 * LOG2E
FFN_HIDDEN = 2816
FFN_CHUNK = 256
CONV_W = 3

TM = 256
HALO = 16
BP = 2
FFN_BP = 1
KEY_SPLIT = 2
SCORE_AHEAD = 2
S5_LC = 32
VMEM_LIMIT = 56 * 1024 * 1024


def _cparams(sem):
    return pltpu.CompilerParams(dimension_semantics=sem, vmem_limit_bytes=VMEM_LIMIT)


def _rms(x):
    return x * lax.rsqrt(jnp.mean(x * x, axis=-1, keepdims=True) + EPS)


def _dot(a, b):
    return jnp.dot(a, b, preferred_element_type=F32)


def _dot_nt(a, b):
    return lax.dot_general(a, b, (((1,), (1,)), ((), ())), preferred_element_type=F32)


def _rope(x, cos, sa, sb, quarter):
    return x * cos + pltpu.roll(x, quarter, 1) * sa + pltpu.roll(x, 128 - quarter, 1) * sb


def _rope_tables(n_lat, dim):
    quarter = dim // 4
    inv = (ROPE_THETA ** (-np.arange(quarter, dtype=np.float32) / quarter)).astype(np.float32)
    t = np.arange(n_lat)
    ang_r = (t // GRID_W).astype(np.float32)[:, None] * inv
    ang_c = (t % GRID_W).astype(np.float32)[:, None] * inv
    ang = np.concatenate([ang_r, ang_r, ang_c, ang_c], axis=-1).astype(np.float32)
    cos, sin = np.cos(ang), np.sin(ang)
    lane_q = (np.arange(dim) // quarter) % 2
    nt = CTX_LEN + n_lat
    cos_t = np.ones((nt, 128), np.float32)
    sa_t = np.zeros((nt, 128), np.float32)
    sb_t = np.zeros((nt, 128), np.float32)
    cos_t[CTX_LEN:, :dim] = cos
    sa_t[CTX_LEN:, :dim] = np.where(lane_q == 1, sin, 0.0)
    sb_t[CTX_LEN:, :dim] = np.where(lane_q == 0, -sin, 0.0)
    return jnp.asarray(cos_t), jnp.asarray(sa_t), jnp.asarray(sb_t)


def _mod_kernel(c_ref, w_ref, b_ref, o_ref):
    c = c_ref[...]
    s = c * jax.nn.sigmoid(c)
    o_ref[0] = _dot(s.astype(BF16), w_ref[0].astype(BF16)) + b_ref[0]


def _modulation(cond, w_mod, b_mod):
    depth, d, n6 = w_mod.shape
    rows = cond.shape[0]
    tn = 1024
    return pl.pallas_call(
        _mod_kernel,
        grid=(depth, n6 // tn),
        in_specs=[pl.BlockSpec((rows, d), lambda l, j: (0, 0)),
                  pl.BlockSpec((1, d, tn), lambda l, j: (l, 0, j)),
                  pl.BlockSpec((1, 1, tn), lambda l, j: (l, 0, j))],
        out_specs=pl.BlockSpec((1, rows, tn), lambda l, j: (l, 0, j)),
        out_shape=jax.ShapeDtypeStruct((depth, rows, n6), F32),
        compiler_params=_cparams(("arbitrary", "arbitrary")),
        name="modulation",
    )(cond, w_mod, b_mod.reshape(depth, 1, n6))


def _stream_rows(t, x_ref, ctx_ref, j):
    return jnp.where(t == 0, ctx_ref[j], x_ref[j])


def _proj_ab_kernel(x_ref, ctx_ref, m_ref, win_ref, gq_ref, wuq_ref, gkv_ref, wkn_ref, wvt_ref,
                    cos_ref, sa_ref, sb_ref, u_ref, q_ref, k_ref, vt_ref):
    t = pl.program_id(1)
    cos, sa, sb = cos_ref[...], sa_ref[...], sb_ref[...]
    ms = [m_ref[j, 0] for j in range(BP)]
    xns = [(_rms(_stream_rows(t, x_ref, ctx_ref, j)) * (1.0 + m[1:2]) + m[0:1]).astype(BF16)
           for j, m in enumerate(ms)]
    zs = [_dot(xn, win_ref[...]) for xn in xns]
    cqns = [(_rms(z[:, 512:896]) * gq_ref[...]).astype(BF16) for z in zs]
    ckvns = [(_rms(z[:, 896:1152]) * gkv_ref[...]).astype(BF16) for z in zs]
    qs = [_dot(cqn, wuq_ref[...]) * MLA_QSCALE for cqn in cqns]
    kns = [_dot(ckvn, wkn_ref[...]) for ckvn in ckvns]
    vts = [_dot_nt(wvt_ref[...], ckvn) for ckvn in ckvns]
    for j, (z, q, kn, vt) in enumerate(zip(zs, qs, kns, vts)):
        u_ref[:, S5_WIDTH * j:S5_WIDTH * (j + 1)] = z[:, :S5_WIDTH]
        krr = _rope(z[:, 1152:1280], cos, sa, sb, MLA_ROPE // 4)
        qparts, kparts = [], []
        for h in range(MLA_HEADS):
            o = MLA_QPAD * h
            qparts += [q[:, o:o + 128], _rope(q[:, o + 128:o + 256], cos, sa, sb, MLA_ROPE // 4)]
            kparts += [kn[:, 128 * h:128 * (h + 1)], krr]
        q_ref[j] = jnp.concatenate(qparts, axis=-1).astype(BF16)
        k_ref[j] = jnp.concatenate(kparts, axis=-1).astype(BF16)
        vt_ref[j] = vt.astype(BF16)


def _proj_ab(x, ctx, msel, win, gq, wuq, gkv, wkn, wvt, tabs):
    b, n_lat, d = x.shape
    nt = CTX_LEN + n_lat
    const = lambda bi, t: (0, 0)
    row = lambda bi, t: (t, 0)
    tok = lambda bi, t: (bi, t, 0)
    return pl.pallas_call(
        _proj_ab_kernel,
        grid=(b // BP, nt // TM),
        in_specs=[pl.BlockSpec((BP, TM, d), lambda bi, t: (bi, jnp.maximum(t - 1, 0), 0)),
                  pl.BlockSpec((BP, CTX_LEN, d), lambda bi, t: (bi, 0, 0)),
                  pl.BlockSpec((BP, 1, 6, d), lambda bi, t: (bi, jnp.minimum(t, 1), 0, 0)),
                  pl.BlockSpec(win.shape, const),
                  pl.BlockSpec(gq.shape, const),
                  pl.BlockSpec(wuq.shape, const),
                  pl.BlockSpec(gkv.shape, const),
                  pl.BlockSpec(wkn.shape, const),
                  pl.BlockSpec(wvt.shape, const),
                  pl.BlockSpec((TM, 128), row),
                  pl.BlockSpec((TM, 128), row),
                  pl.BlockSpec((TM, 128), row)],
        out_specs=[pl.BlockSpec((TM, BP * S5_WIDTH), lambda bi, t: (t, bi)),
                   pl.BlockSpec((BP, TM, MLA_HEADS * MLA_QPAD), tok),
                   pl.BlockSpec((BP, TM, MLA_HEADS * MLA_QPAD), tok),
                   pl.BlockSpec((BP, MLA_HEADS * MLA_V, TM), lambda bi, t: (bi, 0, t))],
        out_shape=[jax.ShapeDtypeStruct((nt, b * S5_WIDTH), F32),
                   jax.ShapeDtypeStruct((b, nt, MLA_HEADS * MLA_QPAD), BF16),
                   jax.ShapeDtypeStruct((b, nt, MLA_HEADS * MLA_QPAD), BF16),
                   jax.ShapeDtypeStruct((b, MLA_HEADS * MLA_V, nt), BF16)],
        compiler_params=_cparams(("parallel", "parallel")),
        name="proj_ab",
    )(x, ctx, msel, win, gq, wuq, gkv, wkn, wvt, *tabs)


def _s5_prep_kernel(lre_ref, lim_ref, ls_ref, bre_ref, bim_ref, ore_ref, oim_ref, obre_ref, obim_ref):
    lre, lim = lre_ref[...], lim_ref[...]
    dt = jnp.exp(ls_ref[...])
    ar, ai = lre * dt, lim * dt
    mag = jnp.exp(ar)
    lb_re, lb_im = mag * jnp.cos(ai), mag * jnp.sin(ai)
    den = lre * lre + lim * lim
    cf_re = ((lb_re - 1.0) * lre + lb_im * lim) / den
    cf_im = (lb_im * lre - (lb_re - 1.0) * lim) / den
    ore_ref[...] = lb_re
    oim_ref[...] = lb_im
    bre, bim = bre_ref[...], bim_ref[...]
    obre_ref[...] = cf_re * bre - cf_im * bim
    obim_ref[...] = cf_re * bim + cf_im * bre


def _s5_prep(lam_re, lam_im, log_step, b_re, b_im):
    g2 = 2 * S5_GROUPS
    args = (lam_re.reshape(g2, 1, S5_STATE), lam_im.reshape(g2, 1, S5_STATE), log_step.reshape(g2, 1, 1),
            jnp.swapaxes(b_re, -1, -2).reshape(g2, S5_GROUP, S5_STATE),
            jnp.swapaxes(b_im, -1, -2).reshape(g2, S5_GROUP, S5_STATE))
    full = lambda a: pl.BlockSpec(a.shape, lambda i: (0,) * a.ndim)
    lam_sds = jax.ShapeDtypeStruct((g2, 1, S5_STATE), F32)
    bb_sds = jax.ShapeDtypeStruct((g2, S5_GROUP, S5_STATE), F32)
    return pl.pallas_call(
        _s5_prep_kernel,
        grid=(1,),
        in_specs=[full(a) for a in args],
        out_specs=[full(lam_sds), full(lam_sds), full(bb_sds), full(bb_sds)],
        out_shape=[lam_sds, lam_sds, bb_sds, bb_sds],
        name="s5_prep",
    )(*args)


def _block_diag(blocks):
    n, r, c = blocks.shape[-3:]
    eye = jnp.eye(n, dtype=blocks.dtype)
    out = blocks[..., :, :, None, :] * eye[:, None, :, None]
    return out.reshape(blocks.shape[:-3] + (n * r, n * c))


def _s5_chunk(u_ref, bcat_ref, lam_ref, cre_ref, cim_ref, scratch, lc, nb, reverse):
    st_ref, bre_ref, bim_ref, sre_ref, sim_ref = scratch
    u = u_ref[...].astype(BF16)
    for c in range(S5_COLS):
        bu = _dot(u[:, 128 * c:128 * (c + 1)], bcat_ref[c])
        bre_ref[:, 512 * c:512 * (c + 1)] = bu[:, :512]
        bim_ref[:, 512 * c:512 * (c + 1)] = bu[:, 512:]
    order = range(lc - 1, -1, -1) if reverse else range(lc)
    for c in range(S5_COLS):
        cs = slice(512 * c, 512 * (c + 1))
        lre = jnp.broadcast_to(lam_ref[0:1, cs], (nb, 512))
        lim = jnp.broadcast_to(lam_ref[1:2, cs], (nb, 512))
        sre, sim = st_ref[0, :, cs], st_ref[1, :, cs]
        for t in order:
            rs = slice(t * nb, (t + 1) * nb)
            nre = lre * sre - lim * sim + bre_ref[rs, cs]
            nim = lre * sim + lim * sre + bim_ref[rs, cs]
            sre_ref[rs, cs] = nre.astype(BF16)
            sim_ref[rs, cs] = nim.astype(BF16)
            sre, sim = nre, nim
        st_ref[0, :, cs] = sre
        st_ref[1, :, cs] = sim
    ys = []
    for c in range(S5_OUT_COLS):
        cs = slice(1024 * c, 1024 * (c + 1))
        ys.append(_dot(sre_ref[:, cs], cre_ref[c]) - _dot(sim_ref[:, cs], cim_ref[c]))
    return jnp.concatenate(ys, axis=-1)


def _s5_fwd_kernel(u_ref, bcat_ref, lam_ref, cre_ref, cim_ref, y_ref, *scratch, lc, nb):
    @pl.when(pl.program_id(0) == 0)
    def _():
        scratch[0][...] = jnp.zeros(scratch[0].shape, F32)

    y_ref[...] = _s5_chunk(u_ref, bcat_ref, lam_ref, cre_ref, cim_ref, scratch, lc, nb, False)


def _s5_bwd_kernel(u_ref, yf_ref, bcat_ref, lam_ref, cre_ref, cim_ref, d_ref, wglu_ref, bglu_ref, o_ref,
                   *scratch, lc, nb):
    @pl.when(pl.program_id(0) == 0)
    def _():
        scratch[0][...] = jnp.zeros(scratch[0].shape, F32)

    yb = _s5_chunk(u_ref, bcat_ref, lam_ref, cre_ref, cim_ref, scratch, lc, nb, True)
    y = d_ref[...] * u_ref[...] + yf_ref[...] + yb
    g = jax.nn.gelu(y)
    o_ref[...] = (g * jax.nn.sigmoid(_dot(g.astype(BF16), wglu_ref[...]) + bglu_ref[...])).astype(BF16)


def _s5_mixer(u_tm, nb, bcat, lam, cre, cim, d_skip, w_glu, b_glu):
    rows_total = u_tm.shape[0]
    lc = S5_LC
    rows = lc * nb
    n_steps = rows_total // rows
    n_ctx = CTX_LEN // lc
    scratch = [pltpu.VMEM((2, nb, S5_CH), F32), pltpu.VMEM((rows, S5_CH), F32), pltpu.VMEM((rows, S5_CH), F32),
               pltpu.VMEM((rows, S5_CH), BF16), pltpu.VMEM((rows, S5_CH), BF16)]
    fwd_blk = pl.BlockSpec((rows, S5_WIDTH), lambda i: (i, 0))
    yf = pl.pallas_call(
        functools.partial(_s5_fwd_kernel, lc=lc, nb=nb),
        grid=(n_steps,),
        in_specs=[fwd_blk] + [pl.BlockSpec((None,) + a.shape[1:], lambda i, n=a.ndim: (0,) * n)
                              for a in (bcat, lam, cre, cim)],
        out_specs=fwd_blk,
        out_shape=jax.ShapeDtypeStruct((rows_total, S5_WIDTH), F32),
        scratch_shapes=scratch,
        compiler_params=_cparams(("arbitrary",)),
        name="s5_fwd",
    )(u_tm, bcat, lam, cre, cim)

    def chunk(i):
        return jnp.where(i < n_ctx, n_ctx - 1 - i, n_steps - 1 - (i - n_ctx))

    bwd_blk = pl.BlockSpec((rows, S5_WIDTH), lambda i: (chunk(i), 0))
    whole = lambda a: pl.BlockSpec(a.shape, lambda i, n=a.ndim: (0,) * n)
    return pl.pallas_call(
        functools.partial(_s5_bwd_kernel, lc=lc, nb=nb),
        grid=(n_steps,),
        in_specs=[bwd_blk, bwd_blk] + [pl.BlockSpec((None,) + a.shape[1:], lambda i, n=a.ndim: (1,) + (0,) * (n - 1))
                                       for a in (bcat, lam, cre, cim)]
                 + [whole(d_skip), whole(w_glu), whole(b_glu)],
        out_specs=bwd_blk,
        out_shape=jax.ShapeDtypeStruct((rows_total, S5_WIDTH), BF16),
        scratch_shapes=scratch,
        compiler_params=_cparams(("arbitrary",)),
        name="s5_bwd_glu",
    )(u_tm, yf, bcat, lam, cre, cim, d_skip, w_glu, b_glu)


def _attn_kernel(q_ref, k_ref, vt_ref, o_ref, *, heads, rep, dq, dv, n_ctx_keys, ctx_tile):
    def run(nk):
        nb = nk // KEY_SPLIT
        kb = [slice(nb * i, nb * (i + 1)) for i in range(KEY_SPLIT)]
        units = [(j, h) for j in range(BP) for h in range(heads)]

        def scores(j, h):
            g = h // rep
            q = q_ref[j, :, dq * h:dq * (h + 1)]
            return [_dot_nt(k_ref[j, r, dq * g:dq * (g + 1)], q) for r in kb]

        outs = [[] for _ in range(BP)]
        pending = [scores(*u) for u in units[:SCORE_AHEAD]]
        for i, (j, h) in enumerate(units):
            ss = pending.pop(0)
            if i + SCORE_AHEAD < len(units):
                pending.append(scores(*units[i + SCORE_AHEAD]))
            m = functools.reduce(jnp.maximum, [jnp.max(s, axis=0, keepdims=True) for s in ss])
            ps = [jnp.exp2(s - m) for s in ss]
            l = sum(jnp.sum(p, axis=0, keepdims=True) for p in ps)
            g = h // rep
            ot = sum(_dot(vt_ref[j, dv * g:dv * (g + 1), r], p.astype(BF16)) for r, p in zip(kb, ps))
            outs[j].append((ot / l).T)
        for j in range(BP):
            o_ref[j] = jnp.concatenate(outs[j], axis=-1).astype(o_ref.dtype)

    nk_all = k_ref.shape[1]
    if ctx_tile:
        qi = pl.program_id(1)

        @pl.when(qi == 0)
        def _():
            run(n_ctx_keys)

        @pl.when(qi > 0)
        def _():
            run(nk_all)
    else:
        run(nk_all)


def _attention(q, k, vt, *, heads, kv_heads, dq, dv, q_tile0, n_q_tiles, out_rows):
    b, nt, _ = q.shape
    return pl.pallas_call(
        functools.partial(_attn_kernel, heads=heads, rep=heads // kv_heads, dq=dq, dv=dv,
                          n_ctx_keys=CTX_LEN, ctx_tile=(q_tile0 == 0)),
        grid=(b // BP, n_q_tiles),
        in_specs=[pl.BlockSpec((BP, TM, heads * dq), lambda bi, qi: (bi, qi + q_tile0, 0)),
                  pl.BlockSpec((BP, nt, kv_heads * dq), lambda bi, qi: (bi, 0, 0)),
                  pl.BlockSpec((BP, kv_heads * dv, nt), lambda bi, qi: (bi, 0, 0))],
        out_specs=pl.BlockSpec((BP, TM, heads * dv), lambda bi, qi: (bi, qi, 0)),
        out_shape=jax.ShapeDtypeStruct((b, out_rows, heads * dv), BF16),
        compiler_params=_cparams(("parallel", "arbitrary")),
        name="attention",
    )(q, k, vt)


def _mix_ffn_kernel(*refs, dual, a_time_major, first_tiles, last_tiles, final):
    if dual:
        x_ref, ctx_ref, *refs = refs
    else:
        (x_ref, *refs), ctx_ref = refs, None
    (hp_ref, hq_ref, a_ref, ap_ref, aq_ref, b_ref, bprev_ref, bnext_ref, m_ref, wa_ref, wb_ref,
     wup_ref, cw_ref, cb_ref, wdn_ref, gfin_ref, o_ref, la_ref, lb_ref, h1_ref, xe_ref, acc_ref) = refs
    t = pl.program_id(1)
    has_prev = functools.reduce(jnp.logical_and, [t != ft for ft in first_tiles])
    has_next = functools.reduce(jnp.logical_and, [t != lt for lt in last_tiles])
    bp = FFN_BP
    n_main = bp * TM
    main = lambda j: slice(TM * j, TM * (j + 1))
    prev = lambda j: slice(n_main + 2 * HALO * j, n_main + 2 * HALO * j + HALO)
    nxt_ = lambda j: slice(n_main + 2 * HALO * j + HALO, n_main + 2 * HALO * (j + 1))
    wa_w = wa_ref.shape[0]
    lda = (lambda r, j: r[:, wa_w * j:wa_w * (j + 1)]) if a_time_major else (lambda r, j: r[j])
    for j in range(bp):
        la_ref[main(j)], la_ref[prev(j)], la_ref[nxt_(j)] = lda(a_ref, j), lda(ap_ref, j), lda(aq_ref, j)
        lb_ref[main(j)], lb_ref[prev(j)], lb_ref[nxt_(j)] = b_ref[j], bprev_ref[j], bnext_ref[j]
    ms = [m_ref[j, 0] for j in range(bp)]
    half = la_ref.shape[0] // 2
    for rs in (slice(0, half), slice(half, 2 * half)):
        h1_ref[rs] = _dot(la_ref[rs], wa_ref[...]) + _dot(lb_ref[rs], wb_ref[...])

    for j, m in enumerate(ms):
        h_main = _stream_rows(t, x_ref, ctx_ref, j) if dual else x_ref[j]
        for r, h_in in ((main(j), h_main), (prev(j), hp_ref[j]), (nxt_(j), hq_ref[j])):
            h1 = h_in + m[2:3] * h1_ref[r]
            h1_ref[r] = h1
            xe_ref[r] = (_rms(h1) * (1.0 + m[4:5]) + m[3:4]).astype(BF16)

    xe = xe_ref[...]
    x = xe[:n_main]
    row = lax.broadcasted_iota(jnp.int32, (TM, FFN_CHUNK), 0)
    f = wdn_ref.shape[0]
    n_chunks = f // FFN_CHUNK
    cols = lambda c: slice(FFN_CHUNK * c, FFN_CHUNK * (c + 1))

    def up(c):
        return (_dot(xe, wup_ref[:, cols(c)]),
                _dot(x, wup_ref[:, f + FFN_CHUNK * c:f + FFN_CHUNK * (c + 1)]))

    pending = up(0)
    for c in range(n_chunks):
        ae, g = pending
        if c + 1 < n_chunks:
            pending = up(c + 1)
        cw = cw_ref[:, cols(c)]
        hids = []
        for j in range(bp):
            a = ae[main(j)]
            a_prev = jnp.where(has_prev, ae[prev(j).stop - 1:prev(j).stop], 0.0)
            a_next = jnp.where(has_next, ae[nxt_(j).start:nxt_(j).start + 1], 0.0)
            a_dn = jnp.where(row == 0, a_prev, pltpu.roll(a, 1, 0))
            a_up = jnp.where(row == TM - 1, a_next, pltpu.roll(a, TM - 1, 0))
            conv = cb_ref[:, cols(c)] + a_dn * cw[0:1] + a * cw[1:2] + a_up * cw[2:3]
            hids.append((jax.nn.gelu(conv) * g[main(j)]).astype(BF16))
        part = _dot(jnp.concatenate(hids, axis=0), wdn_ref[cols(c), :])
        if c == 0:
            acc_ref[...] = part
        else:
            acc_ref[...] += part
    for j, m in enumerate(ms):
        out = h1_ref[main(j)] + m[5:6] * acc_ref[main(j)]
        if final:
            out = _rms(out) * gfin_ref[...]
        o_ref[j] = out


def _mix_ffn(hs, a, bmix, msel, wa, wb, wup, cw, cb, wdn, gfin, *, dual, a_time_major, n_tiles,
             first_tiles, last_tiles, seg_of_tile, final):
    b, _, d = hs[0].shape
    hb = TM // HALO
    ext = FFN_BP * (TM + 2 * HALO)
    c2 = lambda bi, t: (0, 0)
    single = dict(pipeline_mode=pl.Buffered(1))
    prev_blk = lambda t: jnp.maximum(t * hb - 1, 0)
    next_blk = lambda t, n_rows: jnp.minimum((t + 1) * hb, n_rows // HALO - 1)
    if dual:
        n_x = hs[0].shape[1]
        h_specs = [pl.BlockSpec((FFN_BP, TM, d), lambda bi, t: (bi, jnp.maximum(t - 1, 0), 0)),
                   pl.BlockSpec((FFN_BP, CTX_LEN, d), lambda bi, t: (bi, 0, 0)),
                   pl.BlockSpec((FFN_BP, HALO,d), lambda bi, t: (bi, prev_blk(jnp.maximum(t - 1, 0)), 0)),
                   pl.BlockSpec((FFN_BP, HALO,d), lambda bi, t: (bi, next_blk(jnp.maximum(t - 1, 0), n_x), 0))]
        h_args = [hs[0], hs[1], hs[0], hs[0]]
    else:
        n_h = hs[0].shape[1]
        h_specs = [pl.BlockSpec((FFN_BP, TM, d), lambda bi, t: (bi, t + 1, 0)),
                   pl.BlockSpec((FFN_BP, HALO,d), lambda bi, t: (bi, prev_blk(t + 1), 0)),
                   pl.BlockSpec((FFN_BP, HALO,d), lambda bi, t: (bi, next_blk(t + 1, n_h), 0))]
        h_args = [hs[0], hs[0], hs[0]]

    def mixer_specs(arr, time_major):
        if time_major:
            w, n_rows = FFN_BP * (arr.shape[1] // b), arr.shape[0]
            return [pl.BlockSpec((TM, w), lambda bi, t: (t, bi)),
                    pl.BlockSpec((HALO, w), lambda bi, t: (prev_blk(t), bi)),
                    pl.BlockSpec((HALO, w), lambda bi, t: (next_blk(t, n_rows), bi))]
        w, n_rows = arr.shape[2], arr.shape[1]
        return [pl.BlockSpec((FFN_BP, TM, w), lambda bi, t: (bi, t, 0)),
                pl.BlockSpec((FFN_BP, HALO,w), lambda bi, t: (bi, prev_blk(t), 0)),
                pl.BlockSpec((FFN_BP, HALO,w), lambda bi, t: (bi, next_blk(t, n_rows), 0))]

    return pl.pallas_call(
        functools.partial(_mix_ffn_kernel, dual=dual, a_time_major=a_time_major, first_tiles=first_tiles,
                          last_tiles=last_tiles, final=final),
        grid=(b // FFN_BP, n_tiles),
        in_specs=h_specs + mixer_specs(a, a_time_major) + mixer_specs(bmix, False)
                 + [pl.BlockSpec((FFN_BP, 1, 6, d), lambda bi, t: (bi, seg_of_tile(t), 0, 0)),
                    pl.BlockSpec(wa.shape, c2, **single),
                    pl.BlockSpec(wb.shape, c2, **single),
                    pl.BlockSpec(wup.shape, c2, **single),
                    pl.BlockSpec(cw.shape, c2, **single),
                    pl.BlockSpec(cb.shape, c2, **single),
                    pl.BlockSpec(wdn.shape, c2, **single),
                    pl.BlockSpec(gfin.shape, c2)],
        out_specs=pl.BlockSpec((FFN_BP, TM, d), lambda bi, t: (bi, t, 0)),
        out_shape=jax.ShapeDtypeStruct((b, n_tiles * TM, d), F32),
        scratch_shapes=[pltpu.VMEM((ext, wa.shape[0]), BF16), pltpu.VMEM((ext, wb.shape[0]), BF16),
                        pltpu.VMEM((ext, d), F32), pltpu.VMEM((ext, d), BF16), pltpu.VMEM((FFN_BP * TM, d), F32)],
        compiler_params=_cparams(("parallel", "parallel")),
        name="mix_ffn",
    )(*h_args, a, a, a, bmix, bmix, bmix, msel, wa, wb, wup, cw, cb, wdn, gfin)


def _proj_cd_kernel(h_ref, m_ref, win_ref, wgvt_ref, ggq_ref, ggk_ref, cos_ref, sa_ref, sb_ref,
                    rq_ref, rk_ref, rv_ref, rg_ref, gq_ref, gk_ref, gvt_ref):
    k_scale = RET_D ** -0.5
    n_ret = 4 * RET_HEADS * RET_D
    rope = functools.partial(_rope, cos=cos_ref[...], sa=sa_ref[...], sb=sb_ref[...], quarter=RET_D // 4)
    head = lambda z, base, h: z[:, base + 128 * h:base + 128 * (h + 1)]
    ms = [m_ref[j, 0] for j in range(BP)]
    xns = [(_rms(h_ref[j]) * (1.0 + m[1:2]) + m[0:1]).astype(BF16) for j, m in enumerate(ms)]
    z_gqa = [_dot(xn, win_ref[:, n_ret:]) for xn in xns]
    z_rqk = [_dot(xn, win_ref[:, :n_ret // 2]) for xn in xns]
    z_rvg = [_dot(xn, win_ref[:, n_ret // 2:n_ret]) for xn in xns]
    gvts = [_dot_nt(wgvt_ref[...], xn) for xn in xns]
    for j, z in enumerate(z_gqa):
        gq_ref[j] = jnp.concatenate([rope(_rms(head(z, 0, h)) * ggq_ref[...]) * GQA_QSCALE
                                     for h in range(GQA_HEADS)], -1).astype(BF16)
        gk_ref[j] = jnp.concatenate([rope(_rms(head(z, 512, h)) * ggk_ref[...]) for h in range(GQA_KV_HEADS)],
                                    -1).astype(BF16)
    for j, z in enumerate(z_rqk):
        rq_ref[j] = jnp.concatenate([rope(head(z, 0, h)) for h in range(RET_HEADS)], -1).astype(BF16)
        rk_ref[j] = jnp.concatenate([rope(head(z, 512, h)) * k_scale for h in range(RET_HEADS)], -1).astype(BF16)
    for j, (z, gvt) in enumerate(zip(z_rvg, gvts)):
        rv_ref[j] = z[:, :512].astype(BF16)
        rg_ref[j] = z[:, 512:]
        gvt_ref[j] = gvt.astype(BF16)


def _proj_cd(h, msel, win, wgvt, ggq, ggk, tabs):
    b, nt, d = h.shape
    const = lambda bi, t: (0, 0)
    row = lambda bi, t: (t, 0)
    tok = lambda bi, t: (bi, t, 0)
    widths = (512, 512, 512, 512, 512, 256)
    dtypes = (BF16, BF16, BF16, F32, BF16, BF16)
    n_gv = wgvt.shape[0]
    return pl.pallas_call(
        _proj_cd_kernel,
        grid=(b // BP, nt // TM),
        in_specs=[pl.BlockSpec((BP, TM, d), tok),
                  pl.BlockSpec((BP, 1, 6, d), lambda bi, t: (bi, jnp.minimum(t, 1), 0, 0)),
                  pl.BlockSpec(win.shape, const),
                  pl.BlockSpec(wgvt.shape, const),
                  pl.BlockSpec(ggq.shape, const),
                  pl.BlockSpec(ggk.shape, const),
                  pl.BlockSpec((TM, 128), row),
                  pl.BlockSpec((TM, 128), row),
                  pl.BlockSpec((TM, 128), row)],
        out_specs=[pl.BlockSpec((BP, TM, w), tok) for w in widths]
                  + [pl.BlockSpec((BP, n_gv, TM), lambda bi, t: (bi, 0, t))],
        out_shape=[jax.ShapeDtypeStruct((b, nt, w), dt) for w, dt in zip(widths, dtypes)]
                  + [jax.ShapeDtypeStruct((b, n_gv, nt), BF16)],
        compiler_params=_cparams(("parallel", "parallel")),
        name="proj_cd",
    )(h, msel, win, wgvt, ggq, ggk, *tabs)


def _retention_kernel(q_ref, k_ref, v_ref, g_ref, dl_ref, o_ref, *, n_lat):
    c = RET_CHUNK
    nc = n_lat // c
    lg = jax.nn.log_sigmoid(dl_ref[0])
    lgf, lgb = lg[0:1], lg[1:2]
    tdot = lambda a, b: lax.dot_general(a, b, (((0,), (0,)), ((), ())), preferred_element_type=F32)
    k_all, v_all = k_ref[0], v_ref[0]

    mc = lax.broadcasted_iota(jnp.int32, (CTX_LEN, RET_D), 0).astype(F32)
    kc = k_all[:CTX_LEN].astype(F32)
    vc = v_all[:CTX_LEN]
    s_f = tdot((kc * jnp.exp((CTX_LEN - 1.0 - mc) * lgf)).astype(BF16), vc)
    s_b = tdot((kc * jnp.exp(mc * lgb)).astype(BF16), vc)

    i = lax.broadcasted_iota(jnp.int32, (c, RET_D), 0).astype(F32)
    kd_f = jnp.exp((c - 1.0 - i) * lgf)
    kd_b = jnp.exp(i * lgb)
    qd_f = jnp.exp((i + 1.0) * lgf)
    qd_b = jnp.exp((c - i) * lgb)
    cd_f = jnp.exp(c * lgf)
    cd_b = jnp.exp(c * lgb)
    ii = lax.broadcasted_iota(jnp.int32, (c, c), 0)
    jj = lax.broadcasted_iota(jnp.int32, (c, c), 1)
    dist = (ii - jj).astype(F32)
    lgf_c = jnp.concatenate([lgf] * (c // RET_D), axis=-1)
    lgb_c = jnp.concatenate([lgb] * (c // RET_D), axis=-1)
    dec = (jnp.where(ii >= jj, jnp.exp(jnp.maximum(dist, 0.0) * lgf_c), 0.0)
           + jnp.where(ii <= jj, jnp.exp(jnp.maximum(-dist, 0.0) * lgb_c), 0.0))

    ks = [k_all[CTX_LEN + c * n:CTX_LEN + c * (n + 1)] for n in range(nc)]
    vs = [v_all[CTX_LEN + c * n:CTX_LEN + c * (n + 1)] for n in range(nc)]
    sf = [s_f]
    for n in range(nc - 1):
        sf.append(cd_f * sf[n] + tdot((ks[n].astype(F32) * kd_f).astype(BF16), vs[n]))
    sb = [None] * nc
    sb[nc - 1] = s_b
    for n in range(nc - 1, 0, -1):
        sb[n - 1] = cd_b * sb[n] + tdot((ks[n].astype(F32) * kd_b).astype(BF16), vs[n])

    for n in range(nc):
        rs = slice(CTX_LEN + c * n, CTX_LEN + c * (n + 1))
        qn = q_ref[0, rs]
        att = lax.dot_general(qn, ks[n], (((1,), (1,)), ((), ())), preferred_element_type=F32) * dec
        o = (_dot(att.astype(BF16), vs[n])
             + _dot(qn, sf[n].astype(BF16)) * qd_f
             + _dot(qn, sb[n].astype(BF16)) * qd_b)
        gate = g_ref[0, rs]
        o_ref[0, c * n:c * (n + 1)] = (_rms(o) * (gate * jax.nn.sigmoid(gate))).astype(BF16)


def _retention(rq, rk, rv, rg, decay_logit, n_lat):
    b, nt, _ = rq.shape
    dl = jnp.broadcast_to(jnp.swapaxes(decay_logit, 0, 1)[:, :, None], (RET_HEADS, 2, 128))
    blk = pl.BlockSpec((1, nt, RET_D), lambda bi, h: (bi, 0, h))
    return pl.pallas_call(
        functools.partial(_retention_kernel, n_lat=n_lat),
        grid=(b, RET_HEADS),
        in_specs=[blk, blk, blk, blk, pl.BlockSpec((1, 2, 128), lambda bi, h: (h, 0, 0))],
        out_specs=pl.BlockSpec((1, n_lat, RET_D), lambda bi, h: (bi, 0, h)),
        out_shape=jax.ShapeDtypeStruct((b, n_lat, RET_HEADS * RET_D), BF16),
        compiler_params=_cparams(("parallel", "parallel")),
        name="retention",
    )(rq, rk, rv, rg, dl)


def kernel(x, c, ctx, c_ctx, w_mod, b_mod, w_in_ab, w_out_ab, s5_lam_re, s5_lam_im, s5_log_step, s5_b_re, s5_b_im, s5_c_re, s5_c_im, s5_d, s5_w_glu, s5_b_glu, mla_g_q, mla_w_uq, mla_g_kv, mla_w_ukv, w_in_cd, w_out_cd, ret_decay_logit, gqa_g_q, gqa_g_k, ffn_w_up, ffn_conv_w, ffn_conv_b, ffn_w_down, g_final):
    b, n_lat, d = x.shape
    nt = CTX_LEN + n_lat
    n_tiles = nt // TM

    rows = ((b + 1 + 7) // 8) * 8
    cond = jnp.concatenate([c, c_ctx[None], jnp.zeros((rows - b - 1, d), F32)], axis=0)
    mods = _modulation(cond, w_mod, b_mod).reshape(w_mod.shape[0], rows, 6, d)

    def mod_select(layer):
        mctx = jnp.broadcast_to(mods[layer, b][None], (b, 6, d))
        return jnp.stack([mctx, mods[layer, :b]], axis=1)

    msel = mod_select(0)
    win = jnp.pad(w_in_ab[0], ((0, 0), (0, 64))).astype(BF16)
    wuq = jnp.pad(mla_w_uq[0].reshape(MLA_Q_RANK, MLA_HEADS, MLA_NOPE + MLA_ROPE),
                  ((0, 0), (0, 0), (0, MLA_QPAD - MLA_NOPE - MLA_ROPE))).reshape(MLA_Q_RANK, -1).astype(BF16)
    wukv = mla_w_ukv[0].reshape(MLA_KV_RANK, MLA_HEADS, MLA_NOPE + MLA_V)
    wkn = wukv[:, :, :MLA_NOPE].reshape(MLA_KV_RANK, -1).astype(BF16)
    wvt = jnp.transpose(wukv[:, :, MLA_NOPE:].reshape(MLA_KV_RANK, -1)).astype(BF16)
    tabs64 = _rope_tables(n_lat, MLA_ROPE)
    u_tm, q, k, vt = _proj_ab(x, ctx, msel, win, mla_g_q[0][None], wuq, mla_g_kv[0][None], wkn, wvt, tabs64)

    lam_re, lam_im, bbt_re, bbt_im = _s5_prep(s5_lam_re[0], s5_lam_im[0], s5_log_step[0], s5_b_re[0], s5_b_im[0])
    lam = jnp.stack([lam_re.reshape(2, S5_CH), lam_im.reshape(2, S5_CH)], axis=1)
    bshape = (2, S5_COLS, 8, S5_GROUP, S5_STATE)
    bcat = jnp.concatenate([_block_diag(bbt_re.reshape(bshape)), _block_diag(bbt_im.reshape(bshape))],
                           axis=-1).astype(BF16)
    cshape = (2, S5_OUT_COLS, S5_GROUPS // S5_OUT_COLS, S5_GROUP, S5_STATE)
    cre = _block_diag(jnp.swapaxes(s5_c_re[0].reshape(cshape), -1, -2)).astype(BF16)
    cim = _block_diag(jnp.swapaxes(s5_c_im[0].reshape(cshape), -1, -2)).astype(BF16)
    s5_out = _s5_mixer(u_tm.reshape(nt * b, S5_WIDTH), b, bcat, lam, cre, cim,
                       s5_d[0].reshape(1, S5_WIDTH), s5_w_glu[0].astype(BF16), s5_b_glu[0][None])
    s5_out = s5_out.reshape(nt, b * S5_WIDTH)

    att = _attention(q, k, vt, heads=MLA_HEADS, kv_heads=MLA_HEADS, dq=MLA_QPAD, dv=MLA_V,
                     q_tile0=0, n_q_tiles=n_tiles, out_rows=nt)
    wo = w_out_ab[0].astype(BF16)
    seg0 = lambda t: jnp.minimum(t, 1)
    gfin = g_final[None]
    h = _mix_ffn((x, ctx), s5_out, att, msel, wo[:S5_WIDTH], wo[S5_WIDTH:], ffn_w_up[0].astype(BF16),
                 ffn_conv_w[0], ffn_conv_b[0][None], ffn_w_down[0].astype(BF16), gfin,
                 dual=True, a_time_major=True, n_tiles=n_tiles, first_tiles=(0, 1),
                 last_tiles=(0, n_tiles - 1), seg_of_tile=seg0, final=False)

    msel = mod_select(1)
    tabs128 = _rope_tables(n_lat, RET_D)
    n_gv = GQA_KV_HEADS * GQA_D
    wcd = w_in_cd[0]
    rq, rk, rv, rg, gq, gk, gvt = _proj_cd(h, msel, wcd[:, :-n_gv].astype(BF16),
                                           jnp.transpose(wcd[:, -n_gv:]).astype(BF16),
                                           gqa_g_q[0][None], gqa_g_k[0][None], tabs128)
    ret = _retention(rq, rk, rv, rg, ret_decay_logit[0], n_lat)
    att = _attention(gq, gk, gvt, heads=GQA_HEADS, kv_heads=GQA_KV_HEADS, dq=GQA_D, dv=GQA_D,
                     q_tile0=1, n_q_tiles=n_tiles - 1, out_rows=n_lat)
    wo = w_out_cd[0].astype(BF16)
    n_half = RET_HEADS * RET_D
    return _mix_ffn((h,), ret, att, msel, wo[:n_half], wo[n_half:], ffn_w_up[1].astype(BF16),
                    ffn_conv_w[1], ffn_conv_b[1][None], ffn_w_down[1].astype(BF16), gfin,
                    dual=False, a_time_major=False, n_tiles=n_tiles - 1, first_tiles=(0,),
                    last_tiles=(n_tiles - 2,), seg_of_tile=lambda t: 1, final=True)
```

```python
import functools
import math

import numpy as np
import jax
import jax.numpy as jnp
from jax import lax
from jax.experimental import pallas as pl
from jax.experimental.pallas import tpu as pltpu

F32 = jnp.float32
BF16 = jnp.bfloat16

D_MODEL = 1024
GRID_W = 64
CTX_LEN = 256
EPS = 1e-6
ROPE_THETA = 10000.0

S5_WIDTH = 512
S5_GROUP = 16
S5_GROUPS = 32
S5_STATE = 64
S5_CH = S5_GROUPS * S5_STATE
S5_COLS = 4
S5_OUT_COLS = 2
MLA_HEADS = 4
MLA_NOPE = 128
MLA_ROPE = 64
MLA_V = 128
MLA_Q_RANK = 384
MLA_KV_RANK = 256
MLA_QPAD = 256
RET_HEADS = 4
RET_D = 128
RET_CHUNK = 256
GQA_HEADS = 4
GQA_KV_HEADS = 2
GQA_D = 128
LOG2E = 1.4426950408889634
MLA_QSCALE = (MLA_NOPE + MLA_ROPE) ** -0.5 * LOG2E
GQA_QSCALE = GQA_D ** -0.5 * LOG2E
FFN_HIDDEN = 2816
FFN_CHUNK = 256
CONV_W = 3

TM = 256
HALO = 16
BP = 2
KEY_SPLIT = 2
SCORE_AHEAD = 3
S5_LC = 32
S5_SPLIT = 2
VMEM_LIMIT = 56 * 1024 * 1024


def _cparams(sem):
    return pltpu.CompilerParams(dimension_semantics=sem, vmem_limit_bytes=VMEM_LIMIT)


def _rms(x):
    return x * lax.rsqrt(jnp.mean(x * x, axis=-1, keepdims=True) + EPS)


def _dot(a, b):
    return jnp.dot(a, b, preferred_element_type=F32)


def _dot_nt(a, b):
    return lax.dot_general(a, b, (((1,), (1,)), ((), ())), preferred_element_type=F32)


def _rope(x, cos, sa, sb, quarter):
    return x * cos + pltpu.roll(x, quarter, 1) * sa + pltpu.roll(x, 128 - quarter, 1) * sb


def _rope_tables(n_lat, dim):
    quarter = dim // 4
    inv = (ROPE_THETA ** (-np.arange(quarter, dtype=np.float32) / quarter)).astype(np.float32)
    t = np.arange(n_lat)
    ang_r = (t // GRID_W).astype(np.float32)[:, None] * inv
    ang_c = (t % GRID_W).astype(np.float32)[:, None] * inv
    ang = np.concatenate([ang_r, ang_r, ang_c, ang_c], axis=-1).astype(np.float32)
    cos, sin = np.cos(ang), np.sin(ang)
    lane_q = (np.arange(dim) // quarter) % 2
    nt = CTX_LEN + n_lat
    cos_t = np.ones((nt, 128), np.float32)
    sa_t = np.zeros((nt, 128), np.float32)
    sb_t = np.zeros((nt, 128), np.float32)
    cos_t[CTX_LEN:, :dim] = cos
    sa_t[CTX_LEN:, :dim] = np.where(lane_q == 1, sin, 0.0)
    sb_t[CTX_LEN:, :dim] = np.where(lane_q == 0, -sin, 0.0)
    return jnp.asarray(cos_t), jnp.asarray(sa_t), jnp.asarray(sb_t)


def _mod_kernel(c_ref, w_ref, b_ref, o_ref):
    c = c_ref[...]
    s = c * jax.nn.sigmoid(c)
    o_ref[0] = _dot(s.astype(BF16), w_ref[0].astype(BF16)) + b_ref[0]


def _modulation(cond, w_mod, b_mod):
    depth, d, n6 = w_mod.shape
    rows = cond.shape[0]
    tn = 1024
    return pl.pallas_call(
        _mod_kernel,
        grid=(depth, n6 // tn),
        in_specs=[pl.BlockSpec((rows, d), lambda l, j: (0, 0)),
                  pl.BlockSpec((1, d, tn), lambda l, j: (l, 0, j)),
                  pl.BlockSpec((1, 1, tn), lambda l, j: (l, 0, j))],
        out_specs=pl.BlockSpec((1, rows, tn), lambda l, j: (l, 0, j)),
        out_shape=jax.ShapeDtypeStruct((depth, rows, n6), F32),
        compiler_params=_cparams(("arbitrary", "arbitrary")),
        name="modulation",
    )(cond, w_mod, b_mod.reshape(depth, 1, n6))


def _stream_rows(t, x_ref, ctx_ref, j):
    return jnp.where(t == 0, ctx_ref[j], x_ref[j])


def _proj_ab_kernel(x_ref, ctx_ref, m_ref, win_ref, gq_ref, wuq_ref, gkv_ref, wkn_ref, wvt_ref,
                    cos_ref, sa_ref, sb_ref, u_ref, q_ref, k_ref, vt_ref):
    t = pl.program_id(1)
    cos, sa, sb = cos_ref[...], sa_ref[...], sb_ref[...]
    ms = [m_ref[j, 0] for j in range(BP)]
    xns = [(_rms(_stream_rows(t, x_ref, ctx_ref, j)) * (1.0 + m[1:2]) + m[0:1]).astype(BF16)
           for j, m in enumerate(ms)]
    zs = [_dot(xn, win_ref[...]) for xn in xns]
    cqns = [(_rms(z[:, 512:896]) * gq_ref[...]).astype(BF16) for z in zs]
    ckvns = [(_rms(z[:, 896:1152]) * gkv_ref[...]).astype(BF16) for z in zs]
    qs = [_dot(cqn, wuq_ref[...]) * MLA_QSCALE for cqn in cqns]
    kns = [_dot(ckvn, wkn_ref[...]) for ckvn in ckvns]
    vts = [_dot_nt(wvt_ref[...], ckvn) for ckvn in ckvns]
    for j, (z, q, kn, vt) in enumerate(zip(zs, qs, kns, vts)):
        u_ref[:, S5_WIDTH * j:S5_WIDTH * (j + 1)] = z[:, :S5_WIDTH]
        krr = _rope(z[:, 1152:1280], cos, sa, sb, MLA_ROPE // 4)
        qparts, kparts = [], []
        for h in range(MLA_HEADS):
            o = MLA_QPAD * h
            qparts += [q[:, o:o + 128], _rope(q[:, o + 128:o + 256], cos, sa, sb, MLA_ROPE // 4)]
            kparts += [kn[:, 128 * h:128 * (h + 1)], krr]
        q_ref[j] = jnp.concatenate(qparts, axis=-1).astype(BF16)
        k_ref[j] = jnp.concatenate(kparts, axis=-1).astype(BF16)
        vt_ref[j] = vt.astype(BF16)


def _proj_ab(x, ctx, msel, win, gq, wuq, gkv, wkn, wvt, tabs):
    b, n_lat, d = x.shape
    nt = CTX_LEN + n_lat
    const = lambda bi, t: (0, 0)
    row = lambda bi, t: (t, 0)
    tok = lambda bi, t: (bi, t, 0)
    return pl.pallas_call(
        _proj_ab_kernel,
        grid=(b // BP, nt // TM),
        in_specs=[pl.BlockSpec((BP, TM, d), lambda bi, t: (bi, jnp.maximum(t - 1, 0), 0)),
                  pl.BlockSpec((BP, CTX_LEN, d), lambda bi, t: (bi, 0, 0)),
                  pl.BlockSpec((BP, 1, 6, d), lambda bi, t: (bi, jnp.minimum(t, 1), 0, 0)),
                  pl.BlockSpec(win.shape, const),
                  pl.BlockSpec(gq.shape, const),
                  pl.BlockSpec(wuq.shape, const),
                  pl.BlockSpec(gkv.shape, const),
                  pl.BlockSpec(wkn.shape, const),
                  pl.BlockSpec(wvt.shape, const),
                  pl.BlockSpec((TM, 128), row),
                  pl.BlockSpec((TM, 128), row),
                  pl.BlockSpec((TM, 128), row)],
        out_specs=[pl.BlockSpec((TM, BP * S5_WIDTH), lambda bi, t: (t, bi)),
                   pl.BlockSpec((BP, TM, MLA_HEADS * MLA_QPAD), tok),
                   pl.BlockSpec((BP, TM, MLA_HEADS * MLA_QPAD), tok),
                   pl.BlockSpec((BP, MLA_HEADS * MLA_V, TM), lambda bi, t: (bi, 0, t))],
        out_shape=[jax.ShapeDtypeStruct((nt, b * S5_WIDTH), F32),
                   jax.ShapeDtypeStruct((b, nt, MLA_HEADS * MLA_QPAD), BF16),
                   jax.ShapeDtypeStruct((b, nt, MLA_HEADS * MLA_QPAD), BF16),
                   jax.ShapeDtypeStruct((b, MLA_HEADS * MLA_V, nt), BF16)],
        compiler_params=_cparams(("parallel", "parallel")),
        name="proj_ab",
    )(x, ctx, msel, win, gq, wuq, gkv, wkn, wvt, *tabs)


def _s5_prep_kernel(lre_ref, lim_ref, ls_ref, bre_ref, bim_ref, ore_ref, oim_ref, obre_ref, obim_ref):
    lre, lim = lre_ref[...], lim_ref[...]
    dt = jnp.exp(ls_ref[...])
    ar, ai = lre * dt, lim * dt
    mag = jnp.exp(ar)
    lb_re, lb_im = mag * jnp.cos(ai), mag * jnp.sin(ai)
    den = lre * lre + lim * lim
    cf_re = ((lb_re - 1.0) * lre + lb_im * lim) / den
    cf_im = (lb_im * lre - (lb_re - 1.0) * lim) / den
    ore_ref[...] = lb_re
    oim_ref[...] = lb_im
    bre, bim = bre_ref[...], bim_ref[...]
    obre_ref[...] = cf_re * bre - cf_im * bim
    obim_ref[...] = cf_re * bim + cf_im * bre


def _s5_prep(lam_re, lam_im, log_step, b_re, b_im):
    g2 = 2 * S5_GROUPS
    args = (lam_re.reshape(g2, 1, S5_STATE), lam_im.reshape(g2, 1, S5_STATE), log_step.reshape(g2, 1, 1),
            jnp.swapaxes(b_re, -1, -2).reshape(g2, S5_GROUP, S5_STATE),
            jnp.swapaxes(b_im, -1, -2).reshape(g2, S5_GROUP, S5_STATE))
    full = lambda a: pl.BlockSpec(a.shape, lambda i: (0,) * a.ndim)
    lam_sds = jax.ShapeDtypeStruct((g2, 1, S5_STATE), F32)
    bb_sds = jax.ShapeDtypeStruct((g2, S5_GROUP, S5_STATE), F32)
    return pl.pallas_call(
        _s5_prep_kernel,
        grid=(1,),
        in_specs=[full(a) for a in args],
        out_specs=[full(lam_sds), full(lam_sds), full(bb_sds), full(bb_sds)],
        out_shape=[lam_sds, lam_sds, bb_sds, bb_sds],
        name="s5_prep",
    )(*args)


def _block_diag(blocks):
    n, r, c = blocks.shape[-3:]
    eye = jnp.eye(n, dtype=blocks.dtype)
    out = blocks[..., :, :, None, :] * eye[:, None, :, None]
    return out.reshape(blocks.shape[:-3] + (n * r, n * c))


def _s5_chunk(u_ref, bcat_ref, lam_ref, cre_ref, cim_ref, scratch, lc, nb, reverse):
    st_ref, bre_ref, bim_ref = scratch
    gb = nb // S5_SPLIT
    rows_g = lc * gb
    u3 = u_ref[...].reshape(lc, nb, S5_WIDTH)
    order = range(lc - 1, -1, -1) if reverse else range(lc)
    ys = []
    us = [u3[:, gb * g:gb * (g + 1)].reshape(rows_g, S5_WIDTH).astype(BF16) for g in range(S5_SPLIT)]
    for g, u in enumerate(us):
        for c in range(S5_COLS):
            bu = _dot(u[:, 128 * c:128 * (c + 1)], bcat_ref[c])
            bre_ref[g, :, 512 * c:512 * (c + 1)] = bu[:, :512]
            bim_ref[g, :, 512 * c:512 * (c + 1)] = bu[:, 512:]
    for g in range(S5_SPLIT):
        bs = slice(gb * g, gb * (g + 1))
        for c in range(S5_COLS):
            cs = slice(512 * c, 512 * (c + 1))
            lre = jnp.broadcast_to(lam_ref[0:1, cs], (gb, 512))
            lim = jnp.broadcast_to(lam_ref[1:2, cs], (gb, 512))
            sre, sim = st_ref[0, bs, cs], st_ref[1, bs, cs]
            for t in order:
                rs = slice(t * gb, (t + 1) * gb)
                nre = lre * sre - lim * sim + bre_ref[g, rs, cs]
                nim = lre * sim + lim * sre + bim_ref[g, rs, cs]
                bre_ref[g, rs, cs] = nre
                bim_ref[g, rs, cs] = nim
                sre, sim = nre, nim
            st_ref[0, bs, cs] = sre
            st_ref[1, bs, cs] = sim
        yg = []
        for c in range(S5_OUT_COLS):
            cs = slice(1024 * c, 1024 * (c + 1))
            yg.append(_dot(bre_ref[g, :, cs].astype(BF16), cre_ref[c])
                      - _dot(bim_ref[g, :, cs].astype(BF16), cim_ref[c]))
        ys.append(jnp.concatenate(yg, axis=-1).reshape(lc, gb, S5_WIDTH))
    return jnp.concatenate(ys, axis=1).reshape(lc * nb, S5_WIDTH)


def _s5_fwd_kernel(u_ref, bcat_ref, lam_ref, cre_ref, cim_ref, y_ref, *scratch, lc, nb):
    @pl.when(pl.program_id(0) == 0)
    def _():
        scratch[0][...] = jnp.zeros(scratch[0].shape, F32)

    y_ref[...] = _s5_chunk(u_ref, bcat_ref, lam_ref, cre_ref, cim_ref, scratch, lc, nb, False)


def _s5_bwd_kernel(u_ref, yf_ref, bcat_ref, lam_ref, cre_ref, cim_ref, d_ref, wglu_ref, bglu_ref, o_ref,
                   *scratch, lc, nb):
    @pl.when(pl.program_id(0) == 0)
    def _():
        scratch[0][...] = jnp.zeros(scratch[0].shape, F32)

    yb = _s5_chunk(u_ref, bcat_ref, lam_ref, cre_ref, cim_ref, scratch, lc, nb, True)
    y = d_ref[...] * u_ref[...] + yf_ref[...] + yb
    g = jax.nn.gelu(y)
    o_ref[...] = (g * jax.nn.sigmoid(_dot(g.astype(BF16), wglu_ref[...]) + bglu_ref[...])).astype(BF16)


def _s5_mixer(u_tm, nb, bcat, lam, cre, cim, d_skip, w_glu, b_glu):
    rows_total = u_tm.shape[0]
    lc = S5_LC
    rows = lc * nb
    n_steps = rows_total // rows
    n_ctx = CTX_LEN // lc
    scratch = [pltpu.VMEM((2, nb, S5_CH), F32), pltpu.VMEM((S5_SPLIT, rows // S5_SPLIT, S5_CH), F32),
               pltpu.VMEM((S5_SPLIT, rows // S5_SPLIT, S5_CH), F32)]
    fwd_blk = pl.BlockSpec((rows, S5_WIDTH), lambda i: (i, 0))
    yf = pl.pallas_call(
        functools.partial(_s5_fwd_kernel, lc=lc, nb=nb),
        grid=(n_steps,),
        in_specs=[fwd_blk] + [pl.BlockSpec((None,) + a.shape[1:], lambda i, n=a.ndim: (0,) * n)
                              for a in (bcat, lam, cre, cim)],
        out_specs=fwd_blk,
        out_shape=jax.ShapeDtypeStruct((rows_total, S5_WIDTH), F32),
        scratch_shapes=scratch,
        compiler_params=_cparams(("arbitrary",)),
        name="s5_fwd",
    )(u_tm, bcat, lam, cre, cim)

    def chunk(i):
        return jnp.where(i < n_ctx, n_ctx - 1 - i, n_steps - 1 - (i - n_ctx))

    bwd_blk = pl.BlockSpec((rows, S5_WIDTH), lambda i: (chunk(i), 0))
    whole = lambda a: pl.BlockSpec(a.shape, lambda i, n=a.ndim: (0,) * n)
    return pl.pallas_call(
        functools.partial(_s5_bwd_kernel, lc=lc, nb=nb),
        grid=(n_steps,),
        in_specs=[bwd_blk, bwd_blk] + [pl.BlockSpec((None,) + a.shape[1:], lambda i, n=a.ndim: (1,) + (0,) * (n - 1))
                                       for a in (bcat, lam, cre, cim)]
                 + [whole(d_skip), whole(w_glu), whole(b_glu)],
        out_specs=bwd_blk,
        out_shape=jax.ShapeDtypeStruct((rows_total, S5_WIDTH), BF16),
        scratch_shapes=scratch,
        compiler_params=_cparams(("arbitrary",)),
        name="s5_bwd_glu",
    )(u_tm, yf, bcat, lam, cre, cim, d_skip, w_glu, b_glu)


def _attn_kernel(q_ref, k_ref, vt_ref, o_ref, *, heads, rep, dq, dv, n_ctx_keys, ctx_tile):
    def run(nk):
        nb = nk // KEY_SPLIT
        kb = [slice(nb * i, nb * (i + 1)) for i in range(KEY_SPLIT)]
        units = [(j, h) for j in range(BP) for h in range(heads)]

        def scores(j, h):
            g = h // rep
            q = q_ref[j, :, dq * h:dq * (h + 1)]
            return [_dot_nt(k_ref[j, r, dq * g:dq * (g + 1)], q) for r in kb]

        outs = [[] for _ in range(BP)]
        pending = [scores(*u) for u in units[:SCORE_AHEAD]]
        for i, (j, h) in enumerate(units):
            ss = pending.pop(0)
            if i + SCORE_AHEAD < len(units):
                pending.append(scores(*units[i + SCORE_AHEAD]))
            m = functools.reduce(jnp.maximum, [jnp.max(s, axis=0, keepdims=True) for s in ss])
            ps = [jnp.exp2(s - m) for s in ss]
            l = sum(jnp.sum(p, axis=0, keepdims=True) for p in ps)
            g = h // rep
            ot = sum(_dot(vt_ref[j, dv * g:dv * (g + 1), r], p.astype(BF16)) for r, p in zip(kb, ps))
            outs[j].append((ot / l).T)
        for j in range(BP):
            o_ref[j] = jnp.concatenate(outs[j], axis=-1).astype(o_ref.dtype)

    nk_all = k_ref.shape[1]
    if ctx_tile:
        qi = pl.program_id(1)

        @pl.when(qi == 0)
        def _():
            run(n_ctx_keys)

        @pl.when(qi > 0)
        def _():
            run(nk_all)
    else:
        run(nk_all)


def _attention(q, k, vt, *, heads, kv_heads, dq, dv, q_tile0, n_q_tiles, out_rows):
    b, nt, _ = q.shape
    return pl.pallas_call(
        functools.partial(_attn_kernel, heads=heads, rep=heads // kv_heads, dq=dq, dv=dv,
                          n_ctx_keys=CTX_LEN, ctx_tile=(q_tile0 == 0)),
        grid=(b // BP, n_q_tiles),
        in_specs=[pl.BlockSpec((BP, TM, heads * dq), lambda bi, qi: (bi, qi + q_tile0, 0)),
                  pl.BlockSpec((BP, nt, kv_heads * dq), lambda bi, qi: (bi, 0, 0)),
                  pl.BlockSpec((BP, kv_heads * dv, nt), lambda bi, qi: (bi, 0, 0))],
        out_specs=pl.BlockSpec((BP, TM, heads * dv), lambda bi, qi: (bi, qi, 0)),
        out_shape=jax.ShapeDtypeStruct((b, out_rows, heads * dv), BF16),
        compiler_params=_cparams(("parallel", "arbitrary")),
        name="attention",
    )(q, k, vt)


def _mix_ffn_kernel(*refs, dual, a_time_major, first_tiles, last_tiles, final):
    if dual:
        x_ref, ctx_ref, *refs = refs
    else:
        (x_ref, *refs), ctx_ref = refs, None
    (hp_ref, hq_ref, a_ref, ap_ref, aq_ref, b_ref, bprev_ref, bnext_ref, m_ref, wa_ref, wb_ref,
     wup_ref, cw_ref, cb_ref, wdn_ref, gfin_ref, o_ref, la_ref, lb_ref, h1_ref, xe_ref, acc_ref) = refs
    t = pl.program_id(1)
    has_prev = functools.reduce(jnp.logical_and, [t != ft for ft in first_tiles])
    has_next = functools.reduce(jnp.logical_and, [t != lt for lt in last_tiles])
    ext = TM + 2 * HALO
    main, prev, nxt_ = slice(0, TM), slice(TM, TM + HALO), slice(TM + HALO, ext)
    lda = (lambda r: r[...]) if a_time_major else (lambda r: r[0])
    la_ref[main], la_ref[prev], la_ref[nxt_] = lda(a_ref), lda(ap_ref), lda(aq_ref)
    lb_ref[main], lb_ref[prev], lb_ref[nxt_] = b_ref[0], bprev_ref[0], bnext_ref[0]
    m = m_ref[0, 0]
    h_main = _stream_rows(t, x_ref, ctx_ref, 0) if dual else x_ref[0]
    for rs in (slice(0, ext // 2), slice(ext // 2, ext)):
        h1_ref[rs] = _dot(la_ref[rs], wa_ref[...]) + _dot(lb_ref[rs], wb_ref[...])
    for r, h_in in ((main, h_main), (prev, hp_ref[0]), (nxt_, hq_ref[0])):
        h1 = h_in + m[2:3] * h1_ref[r]
        h1_ref[r] = h1
        xe_ref[r] = (_rms(h1) * (1.0 + m[4:5]) + m[3:4]).astype(BF16)

    xe = xe_ref[...]
    x = xe[:TM]
    row = lax.broadcasted_iota(jnp.int32, (TM, FFN_CHUNK), 0)
    f = wdn_ref.shape[0]
    n_chunks = f // FFN_CHUNK
    cols = lambda c: slice(FFN_CHUNK * c, FFN_CHUNK * (c + 1))

    def up(c):
        return (_dot(xe, wup_ref[:, cols(c)]),
                _dot(x, wup_ref[:, f + FFN_CHUNK * c:f + FFN_CHUNK * (c + 1)]))

    pending = up(0)
    for c in range(n_chunks):
        ae, g = pending
        if c + 1 < n_chunks:
            pending = up(c + 1)
        a = ae[:TM]
        a_prev = jnp.where(has_prev, ae[TM + HALO - 1:TM + HALO], 0.0)
        a_next = jnp.where(has_next, ae[TM + HALO:TM + HALO + 1], 0.0)
        a_dn = jnp.where(row == 0, a_prev, pltpu.roll(a, 1, 0))
        a_up = jnp.where(row == TM - 1, a_next, pltpu.roll(a, TM - 1, 0))
        cw = cw_ref[:, cols(c)]
        conv = cb_ref[:, cols(c)] + a_dn * cw[0:1] + a * cw[1:2] + a_up * cw[2:3]
        part = _dot((jax.nn.gelu(conv) * g).astype(BF16), wdn_ref[cols(c), :])
        if c == 0:
            acc_ref[...] = part
        else:
            acc_ref[...] += part
    out = h1_ref[main] + m[5:6] * acc_ref[...]
    if final:
        out = _rms(out) * gfin_ref[...]
    o_ref[0] = out


def _mix_ffn(hs, a, bmix, msel, wa, wb, wup, cw, cb, wdn, gfin, *, dual, a_time_major, n_tiles,
             first_tiles, last_tiles, seg_of_tile, final):
    b, _, d = hs[0].shape
    hb = TM // HALO
    ext = TM + 2 * HALO
    c2 = lambda bi, t: (0, 0)
    single = dict(pipeline_mode=pl.Buffered(1))
    prev_blk = lambda t: jnp.maximum(t * hb - 1, 0)
    next_blk = lambda t, n_rows: jnp.minimum((t + 1) * hb, n_rows // HALO - 1)
    if dual:
        n_x = hs[0].shape[1]
        lat = lambda t: jnp.maximum(t - 1, 0)
        h_specs = [pl.BlockSpec((1, TM, d), lambda bi, t: (bi, lat(t), 0)),
                   pl.BlockSpec((1, CTX_LEN, d), lambda bi, t: (bi, 0, 0)),
                   pl.BlockSpec((1, HALO, d), lambda bi, t: (bi, prev_blk(lat(t)), 0)),
                   pl.BlockSpec((1, HALO, d), lambda bi, t: (bi, next_blk(lat(t), n_x), 0))]
        h_args = [hs[0], hs[1], hs[0], hs[0]]
    else:
        n_h = hs[0].shape[1]
        h_specs = [pl.BlockSpec((1, TM, d), lambda bi, t: (bi, t + 1, 0)),
                   pl.BlockSpec((1, HALO, d), lambda bi, t: (bi, prev_blk(t + 1), 0)),
                   pl.BlockSpec((1, HALO, d), lambda bi, t: (bi, next_blk(t + 1, n_h), 0))]
        h_args = [hs[0], hs[0], hs[0]]

    def mixer_specs(arr, time_major):
        if time_major:
            w, n_rows = arr.shape[1] // b, arr.shape[0]
            return [pl.BlockSpec((TM, w), lambda bi, t: (t, bi)),
                    pl.BlockSpec((HALO, w), lambda bi, t: (prev_blk(t), bi)),
                    pl.BlockSpec((HALO, w), lambda bi, t: (next_blk(t, n_rows), bi))]
        w, n_rows = arr.shape[2], arr.shape[1]
        return [pl.BlockSpec((1, TM, w), lambda bi, t: (bi, t, 0)),
                pl.BlockSpec((1, HALO, w), lambda bi, t: (bi, prev_blk(t), 0)),
                pl.BlockSpec((1, HALO, w), lambda bi, t: (bi, next_blk(t, n_rows), 0))]

    return pl.pallas_call(
        functools.partial(_mix_ffn_kernel, dual=dual, a_time_major=a_time_major, first_tiles=first_tiles,
                          last_tiles=last_tiles, final=final),
        grid=(b, n_tiles),
        in_specs=h_specs + mixer_specs(a, a_time_major) + mixer_specs(bmix, False)
                 + [pl.BlockSpec((1, 1, 6, d), lambda bi, t: (bi, seg_of_tile(t), 0, 0)),
                    pl.BlockSpec(wa.shape, c2, **single),
                    pl.BlockSpec(wb.shape, c2, **single),
                    pl.BlockSpec(wup.shape, c2, **single),
                    pl.BlockSpec(cw.shape, c2, **single),
                    pl.BlockSpec(cb.shape, c2, **single),
                    pl.BlockSpec(wdn.shape, c2, **single),
                    pl.BlockSpec(gfin.shape, c2)],
        out_specs=pl.BlockSpec((1, TM, d), lambda bi, t: (bi, t, 0)),
        out_shape=jax.ShapeDtypeStruct((b, n_tiles * TM, d), F32),
        scratch_shapes=[pltpu.VMEM((ext, wa.shape[0]), BF16), pltpu.VMEM((ext, wb.shape[0]), BF16),
                        pltpu.VMEM((ext, d), F32), pltpu.VMEM((ext, d), BF16), pltpu.VMEM((TM, d), F32)],
        compiler_params=_cparams(("parallel", "parallel")),
        name="mix_ffn",
    )(*h_args, a, a, a, bmix, bmix, bmix, msel, wa, wb, wup, cw, cb, wdn, gfin)


def _proj_cd_kernel(h_ref, m_ref, win_ref, wgvt_ref, ggq_ref, ggk_ref, cos_ref, sa_ref, sb_ref,
                    rq_ref, rk_ref, rv_ref, rg_ref, gq_ref, gk_ref, gvt_ref):
    k_scale = RET_D ** -0.5
    n_ret = 4 * RET_HEADS * RET_D
    rope = functools.partial(_rope, cos=cos_ref[...], sa=sa_ref[...], sb=sb_ref[...], quarter=RET_D // 4)
    head = lambda z, base, h: z[:, base + 128 * h:base + 128 * (h + 1)]
    ms = [m_ref[j, 0] for j in range(BP)]
    xns = [(_rms(h_ref[j]) * (1.0 + m[1:2]) + m[0:1]).astype(BF16) for j, m in enumerate(ms)]
    z_gqa = [_dot(xn, win_ref[:, n_ret:]) for xn in xns]
    z_rqk = [_dot(xn, win_ref[:, :n_ret // 2]) for xn in xns]
    z_rvg = [_dot(xn, win_ref[:, n_ret // 2:n_ret]) for xn in xns]
    gvts = [_dot_nt(wgvt_ref[...], xn) for xn in xns]
    for j, z in enumerate(z_gqa):
        gq_ref[j] = jnp.concatenate([rope(_rms(head(z, 0, h)) * ggq_ref[...]) * GQA_QSCALE
                                     for h in range(GQA_HEADS)], -1).astype(BF16)
        gk_ref[j] = jnp.concatenate([rope(_rms(head(z, 512, h)) * ggk_ref[...]) for h in range(GQA_KV_HEADS)],
                                    -1).astype(BF16)
    for j, z in enumerate(z_rqk):
        rq_ref[j] = jnp.concatenate([rope(head(z, 0, h)) for h in range(RET_HEADS)], -1).astype(BF16)
        rk_ref[j] = jnp.concatenate([rope(head(z, 512, h)) * k_scale for h in range(RET_HEADS)], -1).astype(BF16)
    for j, (z, gvt) in enumerate(zip(z_rvg, gvts)):
        rv_ref[j] = z[:, :512].astype(BF16)
        rg_ref[j] = z[:, 512:]
        gvt_ref[j] = gvt.astype(BF16)


def _proj_cd(h, msel, win, wgvt, ggq, ggk, tabs):
    b, nt, d = h.shape
    const = lambda bi, t: (0, 0)
    row = lambda bi, t: (t, 0)
    tok = lambda bi, t: (bi, t, 0)
    widths = (512, 512, 512, 512, 512, 256)
    dtypes = (BF16, BF16, BF16, F32, BF16, BF16)
    n_gv = wgvt.shape[0]
    return pl.pallas_call(
        _proj_cd_kernel,
        grid=(b // BP, nt // TM),
        in_specs=[pl.BlockSpec((BP, TM, d), tok),
                  pl.BlockSpec((BP, 1, 6, d), lambda bi, t: (bi, jnp.minimum(t, 1), 0, 0)),
                  pl.BlockSpec(win.shape, const),
                  pl.BlockSpec(wgvt.shape, const),
                  pl.BlockSpec(ggq.shape, const),
                  pl.BlockSpec(ggk.shape, const),
                  pl.BlockSpec((TM, 128), row),
                  pl.BlockSpec((TM, 128), row),
                  pl.BlockSpec((TM, 128), row)],
        out_specs=[pl.BlockSpec((BP, TM, w), tok) for w in widths]
                  + [pl.BlockSpec((BP, n_gv, TM), lambda bi, t: (bi, 0, t))],
        out_shape=[jax.ShapeDtypeStruct((b, nt, w), dt) for w, dt in zip(widths, dtypes)]
                  + [jax.ShapeDtypeStruct((b, n_gv, nt), BF16)],
        compiler_params=_cparams(("parallel", "parallel")),
        name="proj_cd",
    )(h, msel, win, wgvt, ggq, ggk, *tabs)


def _retention_kernel(q_ref, k_ref, v_ref, g_ref, dl_ref, o_ref, *, n_lat):
    c = RET_CHUNK
    nc = n_lat // c
    lg = jax.nn.log_sigmoid(dl_ref[0])
    lgf, lgb = lg[0:1], lg[1:2]
    tdot = lambda a, b: lax.dot_general(a, b, (((0,), (0,)), ((), ())), preferred_element_type=F32)
    k_all, v_all = k_ref[0], v_ref[0]

    mc = lax.broadcasted_iota(jnp.int32, (CTX_LEN, RET_D), 0).astype(F32)
    kc = k_all[:CTX_LEN].astype(F32)
    vc = v_all[:CTX_LEN]
    s_f = tdot((kc * jnp.exp((CTX_LEN - 1.0 - mc) * lgf)).astype(BF16), vc)
    s_b = tdot((kc * jnp.exp(mc * lgb)).astype(BF16), vc)

    i = lax.broadcasted_iota(jnp.int32, (c, RET_D), 0).astype(F32)
    kd_f = jnp.exp((c - 1.0 - i) * lgf)
    kd_b = jnp.exp(i * lgb)
    qd_f = jnp.exp((i + 1.0) * lgf)
    qd_b = jnp.exp((c - i) * lgb)
    cd_f = jnp.exp(c * lgf)
    cd_b = jnp.exp(c * lgb)
    ii = lax.broadcasted_iota(jnp.int32, (c, c), 0)
    jj = lax.broadcasted_iota(jnp.int32, (c, c), 1)
    dist = (ii - jj).astype(F32)
    lgf_c = jnp.concatenate([lgf] * (c // RET_D), axis=-1)
    lgb_c = jnp.concatenate([lgb] * (c // RET_D), axis=-1)
    dec = (jnp.where(ii >= jj, jnp.exp(jnp.maximum(dist, 0.0) * lgf_c), 0.0)
           + jnp.where(ii <= jj, jnp.exp(jnp.maximum(-dist, 0.0) * lgb_c), 0.0))

    ks = [k_all[CTX_LEN + c * n:CTX_LEN + c * (n + 1)] for n in range(nc)]
    vs = [v_all[CTX_LEN + c * n:CTX_LEN + c * (n + 1)] for n in range(nc)]
    sf = [s_f]
    for n in range(nc - 1):
        sf.append(cd_f * sf[n] + tdot((ks[n].astype(F32) * kd_f).astype(BF16), vs[n]))
    sb = [None] * nc
    sb[nc - 1] = s_b
    for n in range(nc - 1, 0, -1):
        sb[n - 1] = cd_b * sb[n] + tdot((ks[n].astype(F32) * kd_b).astype(BF16), vs[n])

    for n in range(nc):
        rs = slice(CTX_LEN + c * n, CTX_LEN + c * (n + 1))
        qn = q_ref[0, rs]
        att = _dot_nt(qn, ks[n]) * dec
        o = (_dot(att.astype(BF16), vs[n])
             + _dot(qn, sf[n].astype(BF16)) * qd_f
             + _dot(qn, sb[n].astype(BF16)) * qd_b)
        gate = g_ref[0, rs]
        o_ref[0, c * n:c * (n + 1)] = (_rms(o) * (gate * jax.nn.sigmoid(gate))).astype(BF16)


def _retention(rq, rk, rv, rg, decay_logit, n_lat):
    b, nt, _ = rq.shape
    dl = jnp.broadcast_to(jnp.swapaxes(decay_logit, 0, 1)[:, :, None], (RET_HEADS, 2, 128))
    blk = pl.BlockSpec((1, nt, RET_D), lambda bi, h: (bi, 0, h))
    return pl.pallas_call(
        functools.partial(_retention_kernel, n_lat=n_lat),
        grid=(b, RET_HEADS),
        in_specs=[blk, blk, blk, blk, pl.BlockSpec((1, 2, 128), lambda bi, h: (h, 0, 0))],
        out_specs=pl.BlockSpec((1, n_lat, RET_D), lambda bi, h: (bi, 0, h)),
        out_shape=jax.ShapeDtypeStruct((b, n_lat, RET_HEADS * RET_D), BF16),
        compiler_params=_cparams(("parallel", "parallel")),
        name="retention",
    )(rq, rk, rv, rg, dl)


def kernel(x, c, ctx, c_ctx, w_mod, b_mod, w_in_ab, w_out_ab, s5_lam_re, s5_lam_im, s5_log_step, s5_b_re, s5_b_im, s5_c_re, s5_c_im, s5_d, s5_w_glu, s5_b_glu, mla_g_q, mla_w_uq, mla_g_kv, mla_w_ukv, w_in_cd, w_out_cd, ret_decay_logit, gqa_g_q, gqa_g_k, ffn_w_up, ffn_conv_w, ffn_conv_b, ffn_w_down, g_final):
    b, n_lat, d = x.shape
    nt = CTX_LEN + n_lat
    n_tiles = nt // TM

    rows = ((b + 1 + 7) // 8) * 8
    cond = jnp.concatenate([c, c_ctx[None], jnp.zeros((rows - b - 1, d), F32)], axis=0)
    mods = _modulation(cond, w_mod, b_mod).reshape(w_mod.shape[0], rows, 6, d)

    def mod_select(layer):
        mctx = jnp.broadcast_to(mods[layer, b][None], (b, 6, d))
        return jnp.stack([mctx, mods[layer, :b]], axis=1)

    msel = mod_select(0)
    win = jnp.pad(w_in_ab[0], ((0, 0), (0, 64))).astype(BF16)
    wuq = jnp.pad(mla_w_uq[0].reshape(MLA_Q_RANK, MLA_HEADS, MLA_NOPE + MLA_ROPE),
                  ((0, 0), (0, 0), (0, MLA_QPAD - MLA_NOPE - MLA_ROPE))).reshape(MLA_Q_RANK, -1).astype(BF16)
    wukv = mla_w_ukv[0].reshape(MLA_KV_RANK, MLA_HEADS, MLA_NOPE + MLA_V)
    wkn = wukv[:, :, :MLA_NOPE].reshape(MLA_KV_RANK, -1).astype(BF16)
    wvt = jnp.transpose(wukv[:, :, MLA_NOPE:].reshape(MLA_KV_RANK, -1)).astype(BF16)
    tabs64 = _rope_tables(n_lat, MLA_ROPE)
    u_tm, q, k, vt = _proj_ab(x, ctx, msel, win, mla_g_q[0][None], wuq, mla_g_kv[0][None], wkn, wvt, tabs64)

    lam_re, lam_im, bbt_re, bbt_im = _s5_prep(s5_lam_re[0], s5_lam_im[0], s5_log_step[0], s5_b_re[0], s5_b_im[0])
    lam = jnp.stack([lam_re.reshape(2, S5_CH), lam_im.reshape(2, S5_CH)], axis=1)
    bshape = (2, S5_COLS, 8, S5_GROUP, S5_STATE)
    bcat = jnp.concatenate([_block_diag(bbt_re.reshape(bshape)), _block_diag(bbt_im.reshape(bshape))],
                           axis=-1).astype(BF16)
    cshape = (2, S5_OUT_COLS, S5_GROUPS // S5_OUT_COLS, S5_GROUP, S5_STATE)
    cre = _block_diag(jnp.swapaxes(s5_c_re[0].reshape(cshape), -1, -2)).astype(BF16)
    cim = _block_diag(jnp.swapaxes(s5_c_im[0].reshape(cshape), -1, -2)).astype(BF16)
    s5_out = _s5_mixer(u_tm.reshape(nt * b, S5_WIDTH), b, bcat, lam, cre, cim,
                       s5_d[0].reshape(1, S5_WIDTH), s5_w_glu[0].astype(BF16), s5_b_glu[0][None])
    s5_out = s5_out.reshape(nt, b * S5_WIDTH)

    att = _attention(q, k, vt, heads=MLA_HEADS, kv_heads=MLA_HEADS, dq=MLA_QPAD, dv=MLA_V,
                     q_tile0=0, n_q_tiles=n_tiles, out_rows=nt)
    wo = w_out_ab[0].astype(BF16)
    seg0 = lambda t: jnp.minimum(t, 1)
    gfin = g_final[None]
    h = _mix_ffn((x, ctx), s5_out, att, msel, wo[:S5_WIDTH], wo[S5_WIDTH:], ffn_w_up[0].astype(BF16),
                 ffn_conv_w[0], ffn_conv_b[0][None], ffn_w_down[0].astype(BF16), gfin,
                 dual=True, a_time_major=True, n_tiles=n_tiles, first_tiles=(0, 1),
                 last_tiles=(0, n_tiles - 1), seg_of_tile=seg0, final=False)

    msel = mod_select(1)
    tabs128 = _rope_tables(n_lat, RET_D)
    n_gv = GQA_KV_HEADS * GQA_D
    wcd = w_in_cd[0]
    rq, rk, rv, rg, gq, gk, gvt = _proj_cd(h, msel, wcd[:, :-n_gv].astype(BF16),
                                           jnp.transpose(wcd[:, -n_gv:]).astype(BF16),
                                           gqa_g_q[0][None], gqa_g_k[0][None], tabs128)
    ret = _retention(rq, rk, rv, rg, ret_decay_logit[0], n_lat)
    att = _attention(gq, gk, gvt, heads=GQA_HEADS, kv_heads=GQA_KV_HEADS, dq=GQA_D, dv=GQA_D,
                     q_tile0=1, n_q_tiles=n_tiles - 1, out_rows=n_lat)
    wo = w_out_cd[0].astype(BF16)
    n_half = RET_HEADS * RET_D
    return _mix_ffn((h,), ret, att, msel, wo[:n_half], wo[n_half:], ffn_w_up[1].astype(BF16),
                    ffn_conv_w[1], ffn_conv_b[1][None], ffn_w_down[1].astype(BF16), gfin,
                    dual=False, a_time_major=False, n_tiles=n_tiles - 1, first_tiles=(0,),
                    last_tiles=(n_tiles - 2,), seg_of_tile=lambda t: 1, final=True)
```

```python
import functools
import math

import numpy as np
import jax
import jax.numpy as jnp
from jax import lax
from jax.experimental import pallas as pl
from jax.experimental.pallas import tpu as pltpu

F32 = jnp.float32
BF16 = jnp.bfloat16

D_MODEL = 1024
GRID_W = 64
CTX_LEN = 256
EPS = 1e-6
ROPE_THETA = 10000.0

S5_WIDTH = 512
S5_GROUP = 16
S5_GROUPS = 32
S5_STATE = 64
S5_CH = S5_GROUPS * S5_STATE
S5_COLS = 4
S5_OUT_COLS = 2
MLA_HEADS = 4
MLA_NOPE = 128
MLA_ROPE = 64
MLA_V = 128
MLA_Q_RANK = 384
MLA_KV_RANK = 256
MLA_QPAD = 256
RET_HEADS = 4
RET_D = 128
RET_CHUNK = 256
GQA_HEADS = 4
GQA_KV_HEADS = 2
GQA_D = 128
LOG2E = 1.4426950408889634
MLA_QSCALE = (MLA_NOPE + MLA_ROPE) ** -0.5 * LOG2E
GQA_QSCALE = GQA_D ** -0.5 * LOG2E
FFN_HIDDEN = 2816
FFN_CHUNK = 256
CONV_W = 3

TM = 256
HALO = 16
BP = 2
KEY_SPLIT = 2
SCORE_AHEAD = 3
S5_LC = 32
S5_SPLIT = 2
SLABS = S5_WIDTH // 128
VMEM_LIMIT = 56 * 1024 * 1024


def _cparams(sem):
    return pltpu.CompilerParams(dimension_semantics=sem, vmem_limit_bytes=VMEM_LIMIT)


def _rms(x):
    return x * lax.rsqrt(jnp.mean(x * x, axis=-1, keepdims=True) + EPS)


def _dot(a, b):
    return jnp.dot(a, b, preferred_element_type=F32)


def _dot_nt(a, b):
    return lax.dot_general(a, b, (((1,), (1,)), ((), ())), preferred_element_type=F32)


def _rope(x, cos, sa, sb, quarter):
    return x * cos + pltpu.roll(x, quarter, 1) * sa + pltpu.roll(x, 128 - quarter, 1) * sb


def _rope_tables(n_lat, dim):
    quarter = dim // 4
    inv = (ROPE_THETA ** (-np.arange(quarter, dtype=np.float32) / quarter)).astype(np.float32)
    t = np.arange(n_lat)
    ang_r = (t // GRID_W).astype(np.float32)[:, None] * inv
    ang_c = (t % GRID_W).astype(np.float32)[:, None] * inv
    ang = np.concatenate([ang_r, ang_r, ang_c, ang_c], axis=-1).astype(np.float32)
    cos, sin = np.cos(ang), np.sin(ang)
    lane_q = (np.arange(dim) // quarter) % 2
    nt = CTX_LEN + n_lat
    cos_t = np.ones((nt, 128), np.float32)
    sa_t = np.zeros((nt, 128), np.float32)
    sb_t = np.zeros((nt, 128), np.float32)
    cos_t[CTX_LEN:, :dim] = cos
    sa_t[CTX_LEN:, :dim] = np.where(lane_q == 1, sin, 0.0)
    sb_t[CTX_LEN:, :dim] = np.where(lane_q == 0, -sin, 0.0)
    return jnp.asarray(cos_t), jnp.asarray(sa_t), jnp.asarray(sb_t)


def _mod_kernel(c_ref, w_ref, b_ref, o_ref):
    c = c_ref[...]
    s = c * jax.nn.sigmoid(c)
    o_ref[0] = _dot(s.astype(BF16), w_ref[0].astype(BF16)) + b_ref[0]


def _modulation(cond, w_mod, b_mod):
    depth, d, n6 = w_mod.shape
    rows = cond.shape[0]
    tn = 1024
    return pl.pallas_call(
        _mod_kernel,
        grid=(depth, n6 // tn),
        in_specs=[pl.BlockSpec((rows, d), lambda l, j: (0, 0)),
                  pl.BlockSpec((1, d, tn), lambda l, j: (l, 0, j)),
                  pl.BlockSpec((1, 1, tn), lambda l, j: (l, 0, j))],
        out_specs=pl.BlockSpec((1, rows, tn), lambda l, j: (l, 0, j)),
        out_shape=jax.ShapeDtypeStruct((depth, rows, n6), F32),
        compiler_params=_cparams(("arbitrary", "arbitrary")),
        name="modulation",
    )(cond, w_mod, b_mod.reshape(depth, 1, n6))


def _stream_rows(t, x_ref, ctx_ref, j):
    return jnp.where(t == 0, ctx_ref[j], x_ref[j])


def _proj_ab_kernel(x_ref, ctx_ref, m_ref, win_ref, gq_ref, wuq_ref, gkv_ref, wkn_ref, wvt_ref,
                    cos_ref, sa_ref, sb_ref, u_ref, q_ref, k_ref, vt_ref):
    t, pair = pl.program_id(0), pl.program_id(1)
    nb = u_ref.shape[1] // TM
    cos, sa, sb = cos_ref[...], sa_ref[...], sb_ref[...]
    ms = [m_ref[j, 0] for j in range(BP)]
    xns = [(_rms(_stream_rows(t, x_ref, ctx_ref, j)) * (1.0 + m[1:2]) + m[0:1]).astype(BF16)
           for j, m in enumerate(ms)]
    zs = [_dot(xn, win_ref[...]) for xn in xns]
    cqns = [(_rms(z[:, 512:896]) * gq_ref[...]).astype(BF16) for z in zs]
    ckvns = [(_rms(z[:, 896:1152]) * gkv_ref[...]).astype(BF16) for z in zs]
    qs = [_dot(cqn, wuq_ref[...]) * MLA_QSCALE for cqn in cqns]
    kns = [_dot(ckvn, wkn_ref[...]) for ckvn in ckvns]
    vts = [_dot_nt(wvt_ref[...], ckvn) for ckvn in ckvns]
    for j, (z, q, kn, vt) in enumerate(zip(zs, qs, kns, vts)):
        for sl in range(SLABS):
            u_ref[sl, pl.ds(pair * BP + j, TM, stride=nb), :] = z[:, 128 * sl:128 * (sl + 1)]
        krr = _rope(z[:, 1152:1280], cos, sa, sb, MLA_ROPE // 4)
        qparts, kparts = [], []
        for h in range(MLA_HEADS):
            o = MLA_QPAD * h
            qparts += [q[:, o:o + 128], _rope(q[:, o + 128:o + 256], cos, sa, sb, MLA_ROPE // 4)]
            kparts += [kn[:, 128 * h:128 * (h + 1)], krr]
        q_ref[j] = jnp.concatenate(qparts, axis=-1).astype(BF16)
        k_ref[j] = jnp.concatenate(kparts, axis=-1).astype(BF16)
        vt_ref[j] = vt.astype(BF16)


def _proj_ab(x, ctx, msel, win, gq, wuq, gkv, wkn, wvt, tabs):
    b, n_lat, d = x.shape
    nt = CTX_LEN + n_lat
    const = lambda t, bi: (0, 0)
    row = lambda t, bi: (t, 0)
    tok = lambda t, bi: (bi, t, 0)
    return pl.pallas_call(
        _proj_ab_kernel,
        grid=(nt // TM, b // BP),
        in_specs=[pl.BlockSpec((BP, TM, d), lambda t, bi: (bi, jnp.maximum(t - 1, 0), 0)),
                  pl.BlockSpec((BP, CTX_LEN, d), lambda t, bi: (jnp.where(t == 0, bi, 0), 0, 0)),
                  pl.BlockSpec((BP, 1, 6, d), lambda t, bi: (bi, jnp.minimum(t, 1), 0, 0)),
                  pl.BlockSpec(win.shape, const),
                  pl.BlockSpec(gq.shape, const),
                  pl.BlockSpec(wuq.shape, const),
                  pl.BlockSpec(gkv.shape, const),
                  pl.BlockSpec(wkn.shape, const),
                  pl.BlockSpec(wvt.shape, const),
                  pl.BlockSpec((TM, 128), row),
                  pl.BlockSpec((TM, 128), row),
                  pl.BlockSpec((TM, 128), row)],
        out_specs=[pl.BlockSpec((SLABS, TM * b, 128), lambda t, bi: (0, t, 0)),
                   pl.BlockSpec((BP, TM, MLA_HEADS * MLA_QPAD), tok),
                   pl.BlockSpec((BP, TM, MLA_HEADS * MLA_QPAD), tok),
                   pl.BlockSpec((BP, MLA_HEADS * MLA_V, TM), lambda t, bi: (bi, 0, t))],
        out_shape=[jax.ShapeDtypeStruct((SLABS, nt * b, 128), F32),
                   jax.ShapeDtypeStruct((b, nt, MLA_HEADS * MLA_QPAD), BF16),
                   jax.ShapeDtypeStruct((b, nt, MLA_HEADS * MLA_QPAD), BF16),
                   jax.ShapeDtypeStruct((b, MLA_HEADS * MLA_V, nt), BF16)],
        compiler_params=_cparams(("arbitrary", "arbitrary")),
        name="proj_ab",
    )(x, ctx, msel, win, gq, wuq, gkv, wkn, wvt, *tabs)


def _s5_prep_kernel(lre_ref, lim_ref, ls_ref, bre_ref, bim_ref, ore_ref, oim_ref, obre_ref, obim_ref):
    lre, lim = lre_ref[...], lim_ref[...]
    dt = jnp.exp(ls_ref[...])
    ar, ai = lre * dt, lim * dt
    mag = jnp.exp(ar)
    lb_re, lb_im = mag * jnp.cos(ai), mag * jnp.sin(ai)
    den = lre * lre + lim * lim
    cf_re = ((lb_re - 1.0) * lre + lb_im * lim) / den
    cf_im = (lb_im * lre - (lb_re - 1.0) * lim) / den
    ore_ref[...] = lb_re
    oim_ref[...] = lb_im
    bre, bim = bre_ref[...], bim_ref[...]
    obre_ref[...] = cf_re * bre - cf_im * bim
    obim_ref[...] = cf_re * bim + cf_im * bre


def _s5_prep(lam_re, lam_im, log_step, b_re, b_im):
    g2 = 2 * S5_GROUPS
    args = (lam_re.reshape(g2, 1, S5_STATE), lam_im.reshape(g2, 1, S5_STATE), log_step.reshape(g2, 1, 1),
            jnp.swapaxes(b_re, -1, -2).reshape(g2, S5_GROUP, S5_STATE),
            jnp.swapaxes(b_im, -1, -2).reshape(g2, S5_GROUP, S5_STATE))
    full = lambda a: pl.BlockSpec(a.shape, lambda i: (0,) * a.ndim)
    lam_sds = jax.ShapeDtypeStruct((g2, 1, S5_STATE), F32)
    bb_sds = jax.ShapeDtypeStruct((g2, S5_GROUP, S5_STATE), F32)
    return pl.pallas_call(
        _s5_prep_kernel,
        grid=(1,),
        in_specs=[full(a) for a in args],
        out_specs=[full(lam_sds), full(lam_sds), full(bb_sds), full(bb_sds)],
        out_shape=[lam_sds, lam_sds, bb_sds, bb_sds],
        name="s5_prep",
    )(*args)


def _block_diag(blocks):
    n, r, c = blocks.shape[-3:]
    eye = jnp.eye(n, dtype=blocks.dtype)
    out = blocks[..., :, :, None, :] * eye[:, None, :, None]
    return out.reshape(blocks.shape[:-3] + (n * r, n * c))


def _unslab(ref):
    return jnp.concatenate([ref[sl] for sl in range(SLABS)], axis=-1)


def _s5_chunk(u_ref, bcat_ref, lam_ref, cre_ref, cim_ref, scratch, lc, nb, reverse):
    st_ref, bre_ref, bim_ref = scratch
    gb = nb // S5_SPLIT
    rows_g = lc * gb
    u3 = _unslab(u_ref).reshape(lc, nb, S5_WIDTH)
    order = range(lc - 1, -1, -1) if reverse else range(lc)
    ys = []
    us = [u3[:, gb * g:gb * (g + 1)].reshape(rows_g, S5_WIDTH).astype(BF16) for g in range(S5_SPLIT)]
    for g, u in enumerate(us):
        for c in range(S5_COLS):
            bu = _dot(u[:, 128 * c:128 * (c + 1)], bcat_ref[c])
            bre_ref[g, :, 512 * c:512 * (c + 1)] = bu[:, :512]
            bim_ref[g, :, 512 * c:512 * (c + 1)] = bu[:, 512:]
    for g in range(S5_SPLIT):
        bs = slice(gb * g, gb * (g + 1))
        for c in range(S5_COLS):
            cs = slice(512 * c, 512 * (c + 1))
            lre = jnp.broadcast_to(lam_ref[0:1, cs], (gb, 512))
            lim = jnp.broadcast_to(lam_ref[1:2, cs], (gb, 512))
            sre, sim = st_ref[0, bs, cs], st_ref[1, bs, cs]
            for t in order:
                rs = slice(t * gb, (t + 1) * gb)
                nre = lre * sre - lim * sim + bre_ref[g, rs, cs]
                nim = lre * sim + lim * sre + bim_ref[g, rs, cs]
                bre_ref[g, rs, cs] = nre
                bim_ref[g, rs, cs] = nim
                sre, sim = nre, nim
            st_ref[0, bs, cs] = sre
            st_ref[1, bs, cs] = sim
        yg = []
        for c in range(S5_OUT_COLS):
            cs = slice(1024 * c, 1024 * (c + 1))
            yg.append(_dot(bre_ref[g, :, cs].astype(BF16), cre_ref[c])
                      - _dot(bim_ref[g, :, cs].astype(BF16), cim_ref[c]))
        ys.append(jnp.concatenate(yg, axis=-1).reshape(lc, gb, S5_WIDTH))
    return jnp.concatenate(ys, axis=1).reshape(lc * nb, S5_WIDTH)


def _s5_fwd_kernel(u_ref, bcat_ref, lam_ref, cre_ref, cim_ref, y_ref, *scratch, lc, nb):
    @pl.when(pl.program_id(0) == 0)
    def _():
        scratch[0][...] = jnp.zeros(scratch[0].shape, F32)

    y_ref[...] = _s5_chunk(u_ref, bcat_ref, lam_ref, cre_ref, cim_ref, scratch, lc, nb, False)


def _s5_bwd_kernel(u_ref, yf_ref, bcat_ref, lam_ref, cre_ref, cim_ref, d_ref, wglu_ref, bglu_ref, o_ref,
                   *scratch, lc, nb):
    @pl.when(pl.program_id(0) == 0)
    def _():
        scratch[0][...] = jnp.zeros(scratch[0].shape, F32)

    yb = _s5_chunk(u_ref, bcat_ref, lam_ref, cre_ref, cim_ref, scratch, lc, nb, True)
    y = d_ref[...] * _unslab(u_ref) + yf_ref[...] + yb
    g = jax.nn.gelu(y)
    out = g * jax.nn.sigmoid(_dot(g.astype(BF16), wglu_ref[...]) + bglu_ref[...])
    for sl in range(SLABS):
        o_ref[sl] = out[:, 128 * sl:128 * (sl + 1)]


def _s5_mixer(u_tm, nb, bcat, lam, cre, cim, d_skip, w_glu, b_glu):
    rows_total = u_tm.shape[1]
    lc = S5_LC
    rows = lc * nb
    n_steps = rows_total // rows
    n_ctx = CTX_LEN // lc
    scratch = [pltpu.VMEM((2, nb, S5_CH), F32), pltpu.VMEM((S5_SPLIT, rows // S5_SPLIT, S5_CH), F32),
               pltpu.VMEM((S5_SPLIT, rows // S5_SPLIT, S5_CH), F32)]
    fwd_blk = pl.BlockSpec((rows, S5_WIDTH), lambda i: (i, 0))
    fwd_slab = pl.BlockSpec((SLABS, rows, 128), lambda i: (0, i, 0))
    yf = pl.pallas_call(
        functools.partial(_s5_fwd_kernel, lc=lc, nb=nb),
        grid=(n_steps,),
        in_specs=[fwd_slab] + [pl.BlockSpec((None,) + a.shape[1:], lambda i, n=a.ndim: (0,) * n)
                              for a in (bcat, lam, cre, cim)],
        out_specs=fwd_blk,
        out_shape=jax.ShapeDtypeStruct((rows_total, S5_WIDTH), F32),
        scratch_shapes=scratch,
        compiler_params=_cparams(("arbitrary",)),
        name="s5_fwd",
    )(u_tm, bcat, lam, cre, cim)

    def chunk(i):
        return jnp.where(i < n_ctx, n_ctx - 1 - i, n_steps - 1 - (i - n_ctx))

    bwd_blk = pl.BlockSpec((rows, S5_WIDTH), lambda i: (chunk(i), 0))
    bwd_slab = pl.BlockSpec((SLABS, rows, 128), lambda i: (0, chunk(i), 0))
    whole = lambda a: pl.BlockSpec(a.shape, lambda i, n=a.ndim: (0,) * n)
    return pl.pallas_call(
        functools.partial(_s5_bwd_kernel, lc=lc, nb=nb),
        grid=(n_steps,),
        in_specs=[bwd_slab, bwd_blk] + [pl.BlockSpec((None,) + a.shape[1:], lambda i, n=a.ndim: (1,) + (0,) * (n - 1))
                                       for a in (bcat, lam, cre, cim)]
                 + [whole(d_skip), whole(w_glu), whole(b_glu)],
        out_specs=bwd_slab,
        out_shape=jax.ShapeDtypeStruct((SLABS, rows_total, 128), F32),
        scratch_shapes=scratch,
        compiler_params=_cparams(("arbitrary",)),
        name="s5_bwd_glu",
    )(u_tm, yf, bcat, lam, cre, cim, d_skip, w_glu, b_glu)


def _attn_kernel(q_ref, k_ref, vt_ref, o_ref, *, heads, rep, dq, dv, n_ctx_keys, ctx_tile):
    def run(nk):
        nb = nk // KEY_SPLIT
        kb = [slice(nb * i, nb * (i + 1)) for i in range(KEY_SPLIT)]
        units = [(j, h) for j in range(BP) for h in range(heads)]

        def scores(j, h):
            g = h // rep
            q = q_ref[j, :, dq * h:dq * (h + 1)]
            return [_dot_nt(k_ref[j, r, dq * g:dq * (g + 1)], q) for r in kb]

        outs = [[] for _ in range(BP)]
        pending = [scores(*u) for u in units[:SCORE_AHEAD]]
        for i, (j, h) in enumerate(units):
            ss = pending.pop(0)
            if i + SCORE_AHEAD < len(units):
                pending.append(scores(*units[i + SCORE_AHEAD]))
            m = functools.reduce(jnp.maximum, [jnp.max(s, axis=0, keepdims=True) for s in ss])
            ps = [jnp.exp2(s - m) for s in ss]
            l = sum(jnp.sum(p, axis=0, keepdims=True) for p in ps)
            g = h // rep
            ot = sum(_dot(vt_ref[j, dv * g:dv * (g + 1), r], p.astype(BF16)) for r, p in zip(kb, ps))
            outs[j].append((ot / l).T)
        for j in range(BP):
            o_ref[j] = jnp.concatenate(outs[j], axis=-1).astype(o_ref.dtype)

    nk_all = k_ref.shape[1]
    if ctx_tile:
        qi = pl.program_id(1)

        @pl.when(qi == 0)
        def _():
            run(n_ctx_keys)

        @pl.when(qi > 0)
        def _():
            run(nk_all)
    else:
        run(nk_all)


def _attention(q, k, vt, *, heads, kv_heads, dq, dv, q_tile0, n_q_tiles, out_rows):
    b, nt, _ = q.shape
    return pl.pallas_call(
        functools.partial(_attn_kernel, heads=heads, rep=heads // kv_heads, dq=dq, dv=dv,
                          n_ctx_keys=CTX_LEN, ctx_tile=(q_tile0 == 0)),
        grid=(b // BP, n_q_tiles),
        in_specs=[pl.BlockSpec((BP, TM, heads * dq), lambda bi, qi: (bi, qi + q_tile0, 0)),
                  pl.BlockSpec((BP, nt, kv_heads * dq), lambda bi, qi: (bi, 0, 0)),
                  pl.BlockSpec((BP, kv_heads * dv, nt), lambda bi, qi: (bi, 0, 0))],
        out_specs=pl.BlockSpec((BP, TM, heads * dv), lambda bi, qi: (bi, qi, 0)),
        out_shape=jax.ShapeDtypeStruct((b, out_rows, heads * dv), BF16),
        compiler_params=_cparams(("parallel", "arbitrary")),
        name="attention",
    )(q, k, vt)


def _mix_ffn_kernel(*refs, dual, a_slabs, first_tiles, last_tiles, final):
    if dual:
        x_ref, ctx_ref, *refs = refs
    else:
        (x_ref, *refs), ctx_ref = refs, None
    (hp_ref, hq_ref, a_ref, ap_ref, aq_ref, b_ref, bprev_ref, bnext_ref, m_ref, wa_ref, wb_ref,
     wup_ref, cw_ref, cb_ref, wdn_ref, gfin_ref, o_ref, la_ref, lb_ref, h1_ref, xe_ref, acc_ref) = refs
    t, bi = pl.program_id(0), pl.program_id(1)
    has_prev = functools.reduce(jnp.logical_and, [t != ft for ft in first_tiles])
    has_next = functools.reduce(jnp.logical_and, [t != lt for lt in last_tiles])
    ext = TM + 2 * HALO
    main, prev, nxt_ = slice(0, TM), slice(TM, TM + HALO), slice(TM + HALO, ext)
    if a_slabs:
        lda = lambda r, n: jnp.concatenate(
            [r[sl, pl.ds(bi, n, stride=r.shape[1] // n), :] for sl in range(SLABS)], axis=-1).astype(BF16)
    else:
        lda = lambda r, n: r[0]
    la_ref[main], la_ref[prev], la_ref[nxt_] = lda(a_ref, TM), lda(ap_ref, HALO), lda(aq_ref, HALO)
    lb_ref[main], lb_ref[prev], lb_ref[nxt_] = b_ref[0], bprev_ref[0], bnext_ref[0]
    m = m_ref[0, 0]
    h_main = _stream_rows(t, x_ref, ctx_ref, 0) if dual else x_ref[0]
    for rs in (slice(0, ext // 2), slice(ext // 2, ext)):
        h1_ref[rs] = _dot(la_ref[rs], wa_ref[...]) + _dot(lb_ref[rs], wb_ref[...])
    for r, h_in in ((main, h_main), (prev, hp_ref[0]), (nxt_, hq_ref[0])):
        h1 = h_in + m[2:3] * h1_ref[r]
        h1_ref[r] = h1
        xe_ref[r] = (_rms(h1) * (1.0 + m[4:5]) + m[3:4]).astype(BF16)

    xe = xe_ref[...]
    x = xe[:TM]
    row = lax.broadcasted_iota(jnp.int32, (TM, FFN_CHUNK), 0)
    f = wdn_ref.shape[0]
    n_chunks = f // FFN_CHUNK
    cols = lambda c: slice(FFN_CHUNK * c, FFN_CHUNK * (c + 1))

    def up(c):
        return (_dot(xe, wup_ref[:, cols(c)]),
                _dot(x, wup_ref[:, f + FFN_CHUNK * c:f + FFN_CHUNK * (c + 1)]))

    pending = up(0)
    for c in range(n_chunks):
        ae, g = pending
        if c + 1 < n_chunks:
            pending = up(c + 1)
        a = ae[:TM]
        a_prev = jnp.where(has_prev, ae[TM + HALO - 1:TM + HALO], 0.0)
        a_next = jnp.where(has_next, ae[TM + HALO:TM + HALO + 1], 0.0)
        a_dn = jnp.where(row == 0, a_prev, pltpu.roll(a, 1, 0))
        a_up = jnp.where(row == TM - 1, a_next, pltpu.roll(a, TM - 1, 0))
        cw = cw_ref[:, cols(c)]
        conv = cb_ref[:, cols(c)] + a_dn * cw[0:1] + a * cw[1:2] + a_up * cw[2:3]
        part = _dot((jax.nn.gelu(conv) * g).astype(BF16), wdn_ref[cols(c), :])
        if c == 0:
            acc_ref[...] = part
        else:
            acc_ref[...] += part
    out = h1_ref[main] + m[5:6] * acc_ref[...]
    if final:
        out = _rms(out) * gfin_ref[...]
    o_ref[0] = out


def _mix_ffn(hs, a, bmix, msel, wa, wb, wup, cw, cb, wdn, gfin, *, dual, a_slabs, n_tiles,
             first_tiles, last_tiles, seg_of_tile, final):
    b, _, d = hs[0].shape
    hb = TM // HALO
    ext = TM + 2 * HALO
    c2 = lambda t, bi: (0, 0)
    single = dict(pipeline_mode=pl.Buffered(1))
    prev_blk = lambda t: jnp.maximum(t * hb - 1, 0)
    next_blk = lambda t, n_rows: jnp.minimum((t + 1) * hb, n_rows // HALO - 1)
    if dual:
        n_x = hs[0].shape[1]
        lat = lambda t: jnp.maximum(t - 1, 0)
        h_specs = [pl.BlockSpec((1, TM, d), lambda t, bi: (bi, lat(t), 0)),
                   pl.BlockSpec((1, CTX_LEN, d), lambda t, bi: (jnp.where(t == 0, bi, 0), 0, 0)),
                   pl.BlockSpec((1, HALO, d), lambda t, bi: (bi, prev_blk(lat(t)), 0)),
                   pl.BlockSpec((1, HALO, d), lambda t, bi: (bi, next_blk(lat(t), n_x), 0))]
        h_args = [hs[0], hs[1], hs[0], hs[0]]
    else:
        n_h = hs[0].shape[1]
        h_specs = [pl.BlockSpec((1, TM, d), lambda t, bi: (bi, t + 1, 0)),
                   pl.BlockSpec((1, HALO, d), lambda t, bi: (bi, prev_blk(t + 1), 0)),
                   pl.BlockSpec((1, HALO, d), lambda t, bi: (bi, next_blk(t + 1, n_h), 0))]
        h_args = [hs[0], hs[0], hs[0]]

    def mixer_specs(arr, slabs):
        if slabs:
            n_rows = arr.shape[1] // b
            return [pl.BlockSpec((SLABS, TM * b, 128), lambda t, bi: (0, t, 0), **single),
                    pl.BlockSpec((SLABS, HALO * b, 128), lambda t, bi: (0, prev_blk(t), 0)),
                    pl.BlockSpec((SLABS, HALO * b, 128), lambda t, bi: (0, next_blk(t, n_rows), 0))]
        w, n_rows = arr.shape[2], arr.shape[1]
        return [pl.BlockSpec((1, TM, w), lambda t, bi: (bi, t, 0)),
                pl.BlockSpec((1, HALO, w), lambda t, bi: (bi, prev_blk(t), 0)),
                pl.BlockSpec((1, HALO, w), lambda t, bi: (bi, next_blk(t, n_rows), 0))]

    return pl.pallas_call(
        functools.partial(_mix_ffn_kernel, dual=dual, a_slabs=a_slabs, first_tiles=first_tiles,
                          last_tiles=last_tiles, final=final),
        grid=(n_tiles, b),
        in_specs=h_specs + mixer_specs(a, a_slabs) + mixer_specs(bmix, False)
                 + [pl.BlockSpec((1, 1, 6, d), lambda t, bi: (bi, seg_of_tile(t), 0, 0)),
                    pl.BlockSpec(wa.shape, c2, **single),
                    pl.BlockSpec(wb.shape, c2, **single),
                    pl.BlockSpec(wup.shape, c2, **single),
                    pl.BlockSpec(cw.shape, c2, **single),
                    pl.BlockSpec(cb.shape, c2, **single),
                    pl.BlockSpec(wdn.shape, c2, **single),
                    pl.BlockSpec(gfin.shape, c2)],
        out_specs=pl.BlockSpec((1, TM, d), lambda t, bi: (bi, t, 0)),
        out_shape=jax.ShapeDtypeStruct((b, n_tiles * TM, d), F32),
        scratch_shapes=[pltpu.VMEM((ext, wa.shape[0]), BF16), pltpu.VMEM((ext, wb.shape[0]), BF16),
                        pltpu.VMEM((ext, d), F32), pltpu.VMEM((ext, d), BF16), pltpu.VMEM((TM, d), F32)],
        compiler_params=_cparams(("parallel", "parallel")),
        name="mix_ffn",
    )(*h_args, a, a, a, bmix, bmix, bmix, msel, wa, wb, wup, cw, cb, wdn, gfin)


def _proj_cd_kernel(h_ref, m_ref, win_ref, wgvt_ref, ggq_ref, ggk_ref, cos_ref, sa_ref, sb_ref,
                    rq_ref, rk_ref, rv_ref, rg_ref, gq_ref, gk_ref, gvt_ref):
    k_scale = RET_D ** -0.5
    n_ret = 4 * RET_HEADS * RET_D
    rope = functools.partial(_rope, cos=cos_ref[...], sa=sa_ref[...], sb=sb_ref[...], quarter=RET_D // 4)
    head = lambda z, base, h: z[:, base + 128 * h:base + 128 * (h + 1)]
    ms = [m_ref[j, 0] for j in range(BP)]
    xns = [(_rms(h_ref[j]) * (1.0 + m[1:2]) + m[0:1]).astype(BF16) for j, m in enumerate(ms)]
    z_gqa = [_dot(xn, win_ref[:, n_ret:]) for xn in xns]
    z_rqk = [_dot(xn, win_ref[:, :n_ret // 2]) for xn in xns]
    z_rvg = [_dot(xn, win_ref[:, n_ret // 2:n_ret]) for xn in xns]
    gvts = [_dot_nt(wgvt_ref[...], xn) for xn in xns]
    for j, z in enumerate(z_gqa):
        gq_ref[j] = jnp.concatenate([rope(_rms(head(z, 0, h)) * ggq_ref[...]) * GQA_QSCALE
                                     for h in range(GQA_HEADS)], -1).astype(BF16)
        gk_ref[j] = jnp.concatenate([rope(_rms(head(z, 512, h)) * ggk_ref[...]) for h in range(GQA_KV_HEADS)],
                                    -1).astype(BF16)
    for j, z in enumerate(z_rqk):
        rq_ref[j] = jnp.concatenate([rope(head(z, 0, h)) for h in range(RET_HEADS)], -1).astype(BF16)
        rk_ref[j] = jnp.concatenate([rope(head(z, 512, h)) * k_scale for h in range(RET_HEADS)], -1).astype(BF16)
    for j, (z, gvt) in enumerate(zip(z_rvg, gvts)):
        rv_ref[j] = z[:, :512].astype(BF16)
        rg_ref[j] = z[:, 512:]
        gvt_ref[j] = gvt.astype(BF16)


def _proj_cd(h, msel, win, wgvt, ggq, ggk, tabs):
    b, nt, d = h.shape
    const = lambda bi, t: (0, 0)
    row = lambda bi, t: (t, 0)
    tok = lambda bi, t: (bi, t, 0)
    widths = (512, 512, 512, 512, 512, 256)
    dtypes = (BF16, BF16, BF16, F32, BF16, BF16)
    n_gv = wgvt.shape[0]
    return pl.pallas_call(
        _proj_cd_kernel,
        grid=(b // BP, nt // TM),
        in_specs=[pl.BlockSpec((BP, TM, d), tok),
                  pl.BlockSpec((BP, 1, 6, d), lambda bi, t: (bi, jnp.minimum(t, 1), 0, 0)),
                  pl.BlockSpec(win.shape, const),
                  pl.BlockSpec(wgvt.shape, const),
                  pl.BlockSpec(ggq.shape, const),
                  pl.BlockSpec(ggk.shape, const),
                  pl.BlockSpec((TM, 128), row),
                  pl.BlockSpec((TM, 128), row),
                  pl.BlockSpec((TM, 128), row)],
        out_specs=[pl.BlockSpec((BP, TM, w), tok) for w in widths]
                  + [pl.BlockSpec((BP, n_gv, TM), lambda bi, t: (bi, 0, t))],
        out_shape=[jax.ShapeDtypeStruct((b, nt, w), dt) for w, dt in zip(widths, dtypes)]
                  + [jax.ShapeDtypeStruct((b, n_gv, nt), BF16)],
        compiler_params=_cparams(("parallel", "parallel")),
        name="proj_cd",
    )(h, msel, win, wgvt, ggq, ggk, *tabs)


def _retention_kernel(q_ref, k_ref, v_ref, g_ref, dl_ref, o_ref, *, n_lat):
    c = RET_CHUNK
    nc = n_lat // c
    lg = jax.nn.log_sigmoid(dl_ref[0])
    lgf, lgb = lg[0:1], lg[1:2]
    tdot = lambda a, b: lax.dot_general(a, b, (((0,), (0,)), ((), ())), preferred_element_type=F32)
    k_all, v_all = k_ref[0], v_ref[0]

    mc = lax.broadcasted_iota(jnp.int32, (CTX_LEN, RET_D), 0).astype(F32)
    kc = k_all[:CTX_LEN].astype(F32)
    vc = v_all[:CTX_LEN]
    s_f = tdot((kc * jnp.exp((CTX_LEN - 1.0 - mc) * lgf)).astype(BF16), vc)
    s_b = tdot((kc * jnp.exp(mc * lgb)).astype(BF16), vc)

    i = lax.broadcasted_iota(jnp.int32, (c, RET_D), 0).astype(F32)
    kd_f = jnp.exp((c - 1.0 - i) * lgf)
    kd_b = jnp.exp(i * lgb)
    qd_f = jnp.exp((i + 1.0) * lgf)
    qd_b = jnp.exp((c - i) * lgb)
    cd_f = jnp.exp(c * lgf)
    cd_b = jnp.exp(c * lgb)
    ii = lax.broadcasted_iota(jnp.int32, (c, c), 0)
    jj = lax.broadcasted_iota(jnp.int32, (c, c), 1)
    dist = (ii - jj).astype(F32)
    lgf_c = jnp.concatenate([lgf] * (c // RET_D), axis=-1)
    lgb_c = jnp.concatenate([lgb] * (c // RET_D), axis=-1)
    dec = (jnp.where(ii >= jj, jnp.exp(jnp.maximum(dist, 0.0) * lgf_c), 0.0)
           + jnp.where(ii <= jj, jnp.exp(jnp.maximum(-dist, 0.0) * lgb_c), 0.0))

    ks = [k_all[CTX_LEN + c * n:CTX_LEN + c * (n + 1)] for n in range(nc)]
    vs = [v_all[CTX_LEN + c * n:CTX_LEN + c * (n + 1)] for n in range(nc)]
    sf = [s_f]
    for n in range(nc - 1):
        sf.append(cd_f * sf[n] + tdot((ks[n].astype(F32) * kd_f).astype(BF16), vs[n]))
    sb = [None] * nc
    sb[nc - 1] = s_b
    for n in range(nc - 1, 0, -1):
        sb[n - 1] = cd_b * sb[n] + tdot((ks[n].astype(F32) * kd_b).astype(BF16), vs[n])

    for n in range(nc):
        rs = slice(CTX_LEN + c * n, CTX_LEN + c * (n + 1))
        qn = q_ref[0, rs]
        att = _dot_nt(qn, ks[n]) * dec
        o = (_dot(att.astype(BF16), vs[n])
             + _dot(qn, sf[n].astype(BF16)) * qd_f
             + _dot(qn, sb[n].astype(BF16)) * qd_b)
        gate = g_ref[0, rs]
        o_ref[0, c * n:c * (n + 1)] = (_rms(o) * (gate * jax.nn.sigmoid(gate))).astype(BF16)


def _retention(rq, rk, rv, rg, decay_logit, n_lat):
    b, nt, _ = rq.shape
    dl = jnp.broadcast_to(jnp.swapaxes(decay_logit, 0, 1)[:, :, None], (RET_HEADS, 2, 128))
    blk = pl.BlockSpec((1, nt, RET_D), lambda bi, h: (bi, 0, h))
    return pl.pallas_call(
        functools.partial(_retention_kernel, n_lat=n_lat),
        grid=(b, RET_HEADS),
        in_specs=[blk, blk, blk, blk, pl.BlockSpec((1, 2, 128), lambda bi, h: (h, 0, 0))],
        out_specs=pl.BlockSpec((1, n_lat, RET_D), lambda bi, h: (bi, 0, h)),
        out_shape=jax.ShapeDtypeStruct((b, n_lat, RET_HEADS * RET_D), BF16),
        compiler_params=_cparams(("parallel", "parallel")),
        name="retention",
    )(rq, rk, rv, rg, dl)


def kernel(x, c, ctx, c_ctx, w_mod, b_mod, w_in_ab, w_out_ab, s5_lam_re, s5_lam_im, s5_log_step, s5_b_re, s5_b_im, s5_c_re, s5_c_im, s5_d, s5_w_glu, s5_b_glu, mla_g_q, mla_w_uq, mla_g_kv, mla_w_ukv, w_in_cd, w_out_cd, ret_decay_logit, gqa_g_q, gqa_g_k, ffn_w_up, ffn_conv_w, ffn_conv_b, ffn_w_down, g_final):
    b, n_lat, d = x.shape
    nt = CTX_LEN + n_lat
    n_tiles = nt // TM

    rows = ((b + 1 + 7) // 8) * 8
    cond = jnp.concatenate([c, c_ctx[None], jnp.zeros((rows - b - 1, d), F32)], axis=0)
    mods = _modulation(cond, w_mod, b_mod).reshape(w_mod.shape[0], rows, 6, d)

    def mod_select(layer):
        mctx = jnp.broadcast_to(mods[layer, b][None], (b, 6, d))
        return jnp.stack([mctx, mods[layer, :b]], axis=1)

    msel = mod_select(0)
    win = jnp.pad(w_in_ab[0], ((0, 0), (0, 64))).astype(BF16)
    wuq = jnp.pad(mla_w_uq[0].reshape(MLA_Q_RANK, MLA_HEADS, MLA_NOPE + MLA_ROPE),
                  ((0, 0), (0, 0), (0, MLA_QPAD - MLA_NOPE - MLA_ROPE))).reshape(MLA_Q_RANK, -1).astype(BF16)
    wukv = mla_w_ukv[0].reshape(MLA_KV_RANK, MLA_HEADS, MLA_NOPE + MLA_V)
    wkn = wukv[:, :, :MLA_NOPE].reshape(MLA_KV_RANK, -1).astype(BF16)
    wvt = jnp.transpose(wukv[:, :, MLA_NOPE:].reshape(MLA_KV_RANK, -1)).astype(BF16)
    tabs64 = _rope_tables(n_lat, MLA_ROPE)
    u_tm, q, k, vt = _proj_ab(x, ctx, msel, win, mla_g_q[0][None], wuq, mla_g_kv[0][None], wkn, wvt, tabs64)

    lam_re, lam_im, bbt_re, bbt_im = _s5_prep(s5_lam_re[0], s5_lam_im[0], s5_log_step[0], s5_b_re[0], s5_b_im[0])
    lam = jnp.stack([lam_re.reshape(2, S5_CH), lam_im.reshape(2, S5_CH)], axis=1)
    bshape = (2, S5_COLS, 8, S5_GROUP, S5_STATE)
    bcat = jnp.concatenate([_block_diag(bbt_re.reshape(bshape)), _block_diag(bbt_im.reshape(bshape))],
                           axis=-1).astype(BF16)
    cshape = (2, S5_OUT_COLS, S5_GROUPS // S5_OUT_COLS, S5_GROUP, S5_STATE)
    cre = _block_diag(jnp.swapaxes(s5_c_re[0].reshape(cshape), -1, -2)).astype(BF16)
    cim = _block_diag(jnp.swapaxes(s5_c_im[0].reshape(cshape), -1, -2)).astype(BF16)
    s5_out = _s5_mixer(u_tm, b, bcat, lam, cre, cim,
                       s5_d[0].reshape(1, S5_WIDTH), s5_w_glu[0].astype(BF16), s5_b_glu[0][None])

    att = _attention(q, k, vt, heads=MLA_HEADS, kv_heads=MLA_HEADS, dq=MLA_QPAD, dv=MLA_V,
                     q_tile0=0, n_q_tiles=n_tiles, out_rows=nt)
    wo = w_out_ab[0].astype(BF16)
    seg0 = lambda t: jnp.minimum(t, 1)
    gfin = g_final[None]
    h = _mix_ffn((x, ctx), s5_out, att, msel, wo[:S5_WIDTH], wo[S5_WIDTH:], ffn_w_up[0].astype(BF16),
                 ffn_conv_w[0], ffn_conv_b[0][None], ffn_w_down[0].astype(BF16), gfin,
                 dual=True, a_slabs=True, n_tiles=n_tiles, first_tiles=(0, 1),
                 last_tiles=(0, n_tiles - 1), seg_of_tile=seg0, final=False)

    msel = mod_select(1)
    tabs128 = _rope_tables(n_lat, RET_D)
    n_gv = GQA_KV_HEADS * GQA_D
    wcd = w_in_cd[0]
    rq, rk, rv, rg, gq, gk, gvt = _proj_cd(h, msel, wcd[:, :-n_gv].astype(BF16),
                                           jnp.transpose(wcd[:, -n_gv:]).astype(BF16),
                                           gqa_g_q[0][None], gqa_g_k[0][None], tabs128)
    ret = _retention(rq, rk, rv, rg, ret_decay_logit[0], n_lat)
    att = _attention(gq, gk, gvt, heads=GQA_HEADS, kv_heads=GQA_KV_HEADS, dq=GQA_D, dv=GQA_D,
                     q_tile0=1, n_q_tiles=n_tiles - 1, out_rows=n_lat)
    wo = w_out_cd[0].astype(BF16)
    n_half = RET_HEADS * RET_D
    return _mix_ffn((h,), ret, att, msel, wo[:n_half], wo[n_half:], ffn_w_up[1].astype(BF16),
                    ffn_conv_w[1], ffn_conv_b[1][None], ffn_w_down[1].astype(BF16), gfin,
                    dual=False, a_slabs=False, n_tiles=n_tiles - 1, first_tiles=(0,),
                    last_tiles=(n_tiles - 2,), seg_of_tile=lambda t: 1, final=True)
```

```python
import functools
import math

import numpy as np
import jax
import jax.numpy as jnp
from jax import lax
from jax.experimental import pallas as pl
from jax.experimental.pallas import tpu as pltpu

F32 = jnp.float32
BF16 = jnp.bfloat16

D_MODEL = 1024
GRID_W = 64
CTX_LEN = 256
EPS = 1e-6
ROPE_THETA = 10000.0

S5_WIDTH = 512
S5_GROUP = 16
S5_GROUPS = 32
S5_STATE = 64
S5_CH = S5_GROUPS * S5_STATE
S5_COLS = 4
S5_OUT_COLS = 2
MLA_HEADS = 4
MLA_NOPE = 128
MLA_ROPE = 64
MLA_V = 128
MLA_Q_RANK = 384
MLA_KV_RANK = 256
MLA_QPAD = 256
RET_HEADS = 4
RET_D = 128
RET_CHUNK = 256
GQA_HEADS = 4
GQA_KV_HEADS = 2
GQA_D = 128
LOG2E = 1.4426950408889634
MLA_QSCALE = (MLA_NOPE + MLA_ROPE) ** -0.5 * LOG2E
GQA_QSCALE = GQA_D ** -0.5 * LOG2E
FFN_HIDDEN = 2816
FFN_CHUNK = 256
CONV_W = 3

TM = 256
HALO = 16
BP = 2
KEY_SPLIT = 2
SCORE_AHEAD = 3
S5_LC = 32
S5_SPLIT = 2
SLABS = S5_WIDTH // 128
VMEM_LIMIT = 56 * 1024 * 1024


def _cparams(sem):
    return pltpu.CompilerParams(dimension_semantics=sem, vmem_limit_bytes=VMEM_LIMIT)


def _rms(x):
    return x * lax.rsqrt(jnp.mean(x * x, axis=-1, keepdims=True) + EPS)


def _dot(a, b):
    return jnp.dot(a, b, preferred_element_type=F32)


def _dot_nt(a, b):
    return lax.dot_general(a, b, (((1,), (1,)), ((), ())), preferred_element_type=F32)


def _rope(x, cos, sa, sb, quarter):
    return x * cos + pltpu.roll(x, quarter, 1) * sa + pltpu.roll(x, 128 - quarter, 1) * sb


def _rope_tables(n_lat, dim):
    quarter = dim // 4
    inv = (ROPE_THETA ** (-np.arange(quarter, dtype=np.float32) / quarter)).astype(np.float32)
    t = np.arange(n_lat)
    ang_r = (t // GRID_W).astype(np.float32)[:, None] * inv
    ang_c = (t % GRID_W).astype(np.float32)[:, None] * inv
    ang = np.concatenate([ang_r, ang_r, ang_c, ang_c], axis=-1).astype(np.float32)
    cos, sin = np.cos(ang), np.sin(ang)
    lane_q = (np.arange(dim) // quarter) % 2
    nt = CTX_LEN + n_lat
    cos_t = np.ones((nt, 128), np.float32)
    sa_t = np.zeros((nt, 128), np.float32)
    sb_t = np.zeros((nt, 128), np.float32)
    cos_t[CTX_LEN:, :dim] = cos
    sa_t[CTX_LEN:, :dim] = np.where(lane_q == 1, sin, 0.0)
    sb_t[CTX_LEN:, :dim] = np.where(lane_q == 0, -sin, 0.0)
    return jnp.asarray(cos_t), jnp.asarray(sa_t), jnp.asarray(sb_t)


def _mod_kernel(c_ref, w_ref, b_ref, o_ref):
    c = c_ref[...]
    s = c * jax.nn.sigmoid(c)
    o_ref[0] = _dot(s.astype(BF16), w_ref[0].astype(BF16)) + b_ref[0]


def _modulation(cond, w_mod, b_mod):
    depth, d, n6 = w_mod.shape
    rows = cond.shape[0]
    tn = 1024
    return pl.pallas_call(
        _mod_kernel,
        grid=(depth, n6 // tn),
        in_specs=[pl.BlockSpec((rows, d), lambda l, j: (0, 0)),
                  pl.BlockSpec((1, d, tn), lambda l, j: (l, 0, j)),
                  pl.BlockSpec((1, 1, tn), lambda l, j: (l, 0, j))],
        out_specs=pl.BlockSpec((1, rows, tn), lambda l, j: (l, 0, j)),
        out_shape=jax.ShapeDtypeStruct((depth, rows, n6), F32),
        compiler_params=_cparams(("arbitrary", "arbitrary")),
        name="modulation",
    )(cond, w_mod, b_mod.reshape(depth, 1, n6))


def _stream_rows(t, x_ref, ctx_ref, j):
    return jnp.where(t == 0, ctx_ref[j], x_ref[j])


def _proj_ab_kernel(x_ref, ctx_ref, m_ref, win_ref, gq_ref, wuq_ref, gkv_ref, wkn_ref, wvt_ref,
                    cos_ref, sa_ref, sb_ref, u_ref, q_ref, k_ref, vt_ref):
    t, pair = pl.program_id(0), pl.program_id(1)
    nb = u_ref.shape[1] // TM
    cos, sa, sb = cos_ref[...], sa_ref[...], sb_ref[...]
    ms = [m_ref[j, 0] for j in range(BP)]
    xns = [(_rms(_stream_rows(t, x_ref, ctx_ref, j)) * (1.0 + m[1:2]) + m[0:1]).astype(BF16)
           for j, m in enumerate(ms)]
    zs = [_dot(xn, win_ref[...]) for xn in xns]
    cqns = [(_rms(z[:, 512:896]) * gq_ref[...]).astype(BF16) for z in zs]
    ckvns = [(_rms(z[:, 896:1152]) * gkv_ref[...]).astype(BF16) for z in zs]
    qs = [_dot(cqn, wuq_ref[...]) * MLA_QSCALE for cqn in cqns]
    kns = [_dot(ckvn, wkn_ref[...]) for ckvn in ckvns]
    vts = [_dot_nt(wvt_ref[...], ckvn) for ckvn in ckvns]
    for j, (z, q, kn, vt) in enumerate(zip(zs, qs, kns, vts)):
        for sl in range(SLABS):
            u_ref[sl, pl.ds(pair * BP + j, TM, stride=nb), :] = z[:, 128 * sl:128 * (sl + 1)]
        krr = _rope(z[:, 1152:1280], cos, sa, sb, MLA_ROPE // 4)
        qparts, kparts = [], []
        for h in range(MLA_HEADS):
            o = MLA_QPAD * h
            qparts += [q[:, o:o + 128], _rope(q[:, o + 128:o + 256], cos, sa, sb, MLA_ROPE // 4)]
            kparts += [kn[:, 128 * h:128 * (h + 1)], krr]
        q_ref[j] = jnp.concatenate(qparts, axis=-1).astype(BF16)
        k_ref[j] = jnp.concatenate(kparts, axis=-1).astype(BF16)
        vt_ref[j] = vt.astype(BF16)


def _proj_ab(x, ctx, msel, win, gq, wuq, gkv, wkn, wvt, tabs):
    b, n_lat, d = x.shape
    nt = CTX_LEN + n_lat
    const = lambda t, bi: (0, 0)
    row = lambda t, bi: (t, 0)
    tok = lambda t, bi: (bi, t, 0)
    return pl.pallas_call(
        _proj_ab_kernel,
        grid=(nt // TM, b // BP),
        in_specs=[pl.BlockSpec((BP, TM, d), lambda t, bi: (bi, jnp.maximum(t - 1, 0), 0)),
                  pl.BlockSpec((BP, CTX_LEN, d), lambda t, bi: (jnp.where(t == 0, bi, 0), 0, 0)),
                  pl.BlockSpec((BP, 1, 6, d), lambda t, bi: (bi, jnp.minimum(t, 1), 0, 0)),
                  pl.BlockSpec(win.shape, const),
                  pl.BlockSpec(gq.shape, const),
                  pl.BlockSpec(wuq.shape, const),
                  pl.BlockSpec(gkv.shape, const),
                  pl.BlockSpec(wkn.shape, const),
                  pl.BlockSpec(wvt.shape, const),
                  pl.BlockSpec((TM, 128), row),
                  pl.BlockSpec((TM, 128), row),
                  pl.BlockSpec((TM, 128), row)],
        out_specs=[pl.BlockSpec((SLABS, TM * b, 128), lambda t, bi: (0, t, 0)),
                   pl.BlockSpec((BP, TM, MLA_HEADS * MLA_QPAD), tok),
                   pl.BlockSpec((BP, TM, MLA_HEADS * MLA_QPAD), tok),
                   pl.BlockSpec((BP, MLA_HEADS * MLA_V, TM), lambda t, bi: (bi, 0, t))],
        out_shape=[jax.ShapeDtypeStruct((SLABS, nt * b, 128), F32),
                   jax.ShapeDtypeStruct((b, nt, MLA_HEADS * MLA_QPAD), BF16),
                   jax.ShapeDtypeStruct((b, nt, MLA_HEADS * MLA_QPAD), BF16),
                   jax.ShapeDtypeStruct((b, MLA_HEADS * MLA_V, nt), BF16)],
        compiler_params=_cparams(("arbitrary", "arbitrary")),
        name="proj_ab",
    )(x, ctx, msel, win, gq, wuq, gkv, wkn, wvt, *tabs)


def _s5_prep_kernel(lre_ref, lim_ref, ls_ref, bre_ref, bim_ref, ore_ref, oim_ref, obre_ref, obim_ref):
    lre, lim = lre_ref[...], lim_ref[...]
    dt = jnp.exp(ls_ref[...])
    ar, ai = lre * dt, lim * dt
    mag = jnp.exp(ar)
    lb_re, lb_im = mag * jnp.cos(ai), mag * jnp.sin(ai)
    den = lre * lre + lim * lim
    cf_re = ((lb_re - 1.0) * lre + lb_im * lim) / den
    cf_im = (lb_im * lre - (lb_re - 1.0) * lim) / den
    ore_ref[...] = lb_re
    oim_ref[...] = lb_im
    bre, bim = bre_ref[...], bim_ref[...]
    obre_ref[...] = cf_re * bre - cf_im * bim
    obim_ref[...] = cf_re * bim + cf_im * bre


def _s5_prep(lam_re, lam_im, log_step, b_re, b_im):
    g2 = 2 * S5_GROUPS
    args = (lam_re.reshape(g2, 1, S5_STATE), lam_im.reshape(g2, 1, S5_STATE), log_step.reshape(g2, 1, 1),
            jnp.swapaxes(b_re, -1, -2).reshape(g2, S5_GROUP, S5_STATE),
            jnp.swapaxes(b_im, -1, -2).reshape(g2, S5_GROUP, S5_STATE))
    full = lambda a: pl.BlockSpec(a.shape, lambda i: (0,) * a.ndim)
    lam_sds = jax.ShapeDtypeStruct((g2, 1, S5_STATE), F32)
    bb_sds = jax.ShapeDtypeStruct((g2, S5_GROUP, S5_STATE), F32)
    return pl.pallas_call(
        _s5_prep_kernel,
        grid=(1,),
        in_specs=[full(a) for a in args],
        out_specs=[full(lam_sds), full(lam_sds), full(bb_sds), full(bb_sds)],
        out_shape=[lam_sds, lam_sds, bb_sds, bb_sds],
        name="s5_prep",
    )(*args)


def _block_diag(blocks):
    n, r, c = blocks.shape[-3:]
    eye = jnp.eye(n, dtype=blocks.dtype)
    out = blocks[..., :, :, None, :] * eye[:, None, :, None]
    return out.reshape(blocks.shape[:-3] + (n * r, n * c))


def _unslab(ref):
    return jnp.concatenate([ref[sl] for sl in range(SLABS)], axis=-1)


def _s5_chunk(u_ref, bcat_ref, lam_ref, cre_ref, cim_ref, scratch, lc, nb, reverse):
    st_ref, bre_ref, bim_ref = scratch
    gb = nb // S5_SPLIT
    rows_g = lc * gb
    u3 = _unslab(u_ref).reshape(lc, nb, S5_WIDTH)
    order = range(lc - 1, -1, -1) if reverse else range(lc)
    ys = []
    us = [u3[:, gb * g:gb * (g + 1)].reshape(rows_g, S5_WIDTH).astype(BF16) for g in range(S5_SPLIT)]
    for g, u in enumerate(us):
        for c in range(S5_COLS):
            bu = _dot(u[:, 128 * c:128 * (c + 1)], bcat_ref[c])
            bre_ref[g, :, 512 * c:512 * (c + 1)] = bu[:, :512]
            bim_ref[g, :, 512 * c:512 * (c + 1)] = bu[:, 512:]
    for g in range(S5_SPLIT):
        bs = slice(gb * g, gb * (g + 1))
        for c in range(S5_COLS):
            cs = slice(512 * c, 512 * (c + 1))
            lre = jnp.broadcast_to(lam_ref[0:1, cs], (gb, 512))
            lim = jnp.broadcast_to(lam_ref[1:2, cs], (gb, 512))
            sre, sim = st_ref[0, bs, cs], st_ref[1, bs, cs]
            for t in order:
                rs = slice(t * gb, (t + 1) * gb)
                nre = lre * sre - lim * sim + bre_ref[g, rs, cs]
                nim = lre * sim + lim * sre + bim_ref[g, rs, cs]
                bre_ref[g, rs, cs] = nre
                bim_ref[g, rs, cs] = nim
                sre, sim = nre, nim
            st_ref[0, bs, cs] = sre
            st_ref[1, bs, cs] = sim
        yg = []
        for c in range(S5_OUT_COLS):
            cs = slice(1024 * c, 1024 * (c + 1))
            yg.append(_dot(bre_ref[g, :, cs].astype(BF16), cre_ref[c])
                      - _dot(bim_ref[g, :, cs].astype(BF16), cim_ref[c]))
        ys.append(jnp.concatenate(yg, axis=-1).reshape(lc, gb, S5_WIDTH))
    return jnp.concatenate(ys, axis=1).reshape(lc * nb, S5_WIDTH)


def _s5_fwd_kernel(u_ref, bcat_ref, lam_ref, cre_ref, cim_ref, y_ref, *scratch, lc, nb):
    @pl.when(pl.program_id(0) == 0)
    def _():
        scratch[0][...] = jnp.zeros(scratch[0].shape, F32)

    y_ref[...] = _s5_chunk(u_ref, bcat_ref, lam_ref, cre_ref, cim_ref, scratch, lc, nb, False)


def _s5_bwd_kernel(u_ref, yf_ref, bcat_ref, lam_ref, cre_ref, cim_ref, d_ref, wglu_ref, bglu_ref, o_ref,
                   oslab_ref, *scratch, lc, nb):
    @pl.when(pl.program_id(0) == 0)
    def _():
        scratch[0][...] = jnp.zeros(scratch[0].shape, F32)

    yb = _s5_chunk(u_ref, bcat_ref, lam_ref, cre_ref, cim_ref, scratch, lc, nb, True)
    y = d_ref[...] * _unslab(u_ref) + yf_ref[...] + yb
    g = jax.nn.gelu(y)
    out = g * jax.nn.sigmoid(_dot(g.astype(BF16), wglu_ref[...]) + bglu_ref[...])
    for sl in range(SLABS):
        oslab_ref[sl] = out[:, 128 * sl:128 * (sl + 1)]
    for j in range(nb):
        o_ref[j] = jnp.concatenate([oslab_ref[sl, pl.ds(j, lc, stride=nb), :] for sl in range(SLABS)],
                                   axis=-1).astype(BF16)


def _s5_mixer(u_tm, nb, bcat, lam, cre, cim, d_skip, w_glu, b_glu):
    rows_total = u_tm.shape[1]
    lc = S5_LC
    rows = lc * nb
    n_steps = rows_total // rows
    n_ctx = CTX_LEN // lc
    scratch = [pltpu.VMEM((2, nb, S5_CH), F32), pltpu.VMEM((S5_SPLIT, rows // S5_SPLIT, S5_CH), F32),
               pltpu.VMEM((S5_SPLIT, rows // S5_SPLIT, S5_CH), F32)]
    fwd_blk = pl.BlockSpec((rows, S5_WIDTH), lambda i: (i, 0))
    fwd_slab = pl.BlockSpec((SLABS, rows, 128), lambda i: (0, i, 0))
    yf = pl.pallas_call(
        functools.partial(_s5_fwd_kernel, lc=lc, nb=nb),
        grid=(n_steps,),
        in_specs=[fwd_slab] + [pl.BlockSpec((None,) + a.shape[1:], lambda i, n=a.ndim: (0,) * n)
                              for a in (bcat, lam, cre, cim)],
        out_specs=fwd_blk,
        out_shape=jax.ShapeDtypeStruct((rows_total, S5_WIDTH), F32),
        scratch_shapes=scratch,
        compiler_params=_cparams(("arbitrary",)),
        name="s5_fwd",
    )(u_tm, bcat, lam, cre, cim)

    def chunk(i):
        return jnp.where(i < n_ctx, n_ctx - 1 - i, n_steps - 1 - (i - n_ctx))

    bwd_blk = pl.BlockSpec((rows, S5_WIDTH), lambda i: (chunk(i), 0))
    bwd_slab = pl.BlockSpec((SLABS, rows, 128), lambda i: (0, chunk(i), 0))
    whole = lambda a: pl.BlockSpec(a.shape, lambda i, n=a.ndim: (0,) * n)
    return pl.pallas_call(
        functools.partial(_s5_bwd_kernel, lc=lc, nb=nb),
        grid=(n_steps,),
        in_specs=[bwd_slab, bwd_blk] + [pl.BlockSpec((None,) + a.shape[1:], lambda i, n=a.ndim: (1,) + (0,) * (n - 1))
                                       for a in (bcat, lam, cre, cim)]
                 + [whole(d_skip), whole(w_glu), whole(b_glu)],
        out_specs=pl.BlockSpec((nb, lc, S5_WIDTH), lambda i: (0, chunk(i), 0)),
        out_shape=jax.ShapeDtypeStruct((nb, rows_total // nb, S5_WIDTH), BF16),
        scratch_shapes=[pltpu.VMEM((SLABS, rows, 128), F32)] + scratch,
        compiler_params=_cparams(("arbitrary",)),
        name="s5_bwd_glu",
    )(u_tm, yf, bcat, lam, cre, cim, d_skip, w_glu, b_glu)


def _attn_kernel(q_ref, k_ref, vt_ref, o_ref, *, heads, rep, dq, dv, n_ctx_keys, ctx_tile):
    def run(nk):
        nb = nk // KEY_SPLIT
        kb = [slice(nb * i, nb * (i + 1)) for i in range(KEY_SPLIT)]
        units = [(j, h) for j in range(BP) for h in range(heads)]

        def scores(j, h):
            g = h // rep
            q = q_ref[j, :, dq * h:dq * (h + 1)]
            return [_dot_nt(k_ref[j, r, dq * g:dq * (g + 1)], q) for r in kb]

        outs = [[] for _ in range(BP)]
        pending = [scores(*u) for u in units[:SCORE_AHEAD]]
        for i, (j, h) in enumerate(units):
            ss = pending.pop(0)
            if i + SCORE_AHEAD < len(units):
                pending.append(scores(*units[i + SCORE_AHEAD]))
            m = functools.reduce(jnp.maximum, [jnp.max(s, axis=0, keepdims=True) for s in ss])
            ps = [jnp.exp2(s - m) for s in ss]
            l = sum(jnp.sum(p, axis=0, keepdims=True) for p in ps)
            g = h // rep
            ot = sum(_dot(vt_ref[j, dv * g:dv * (g + 1), r], p.astype(BF16)) for r, p in zip(kb, ps))
            outs[j].append((ot / l).T)
        for j in range(BP):
            o_ref[j] = jnp.concatenate(outs[j], axis=-1).astype(o_ref.dtype)

    nk_all = k_ref.shape[1]
    if ctx_tile:
        qi = pl.program_id(1)

        @pl.when(qi == 0)
        def _():
            run(n_ctx_keys)

        @pl.when(qi > 0)
        def _():
            run(nk_all)
    else:
        run(nk_all)


def _attention(q, k, vt, *, heads, kv_heads, dq, dv, q_tile0, n_q_tiles, out_rows):
    b, nt, _ = q.shape
    return pl.pallas_call(
        functools.partial(_attn_kernel, heads=heads, rep=heads // kv_heads, dq=dq, dv=dv,
                          n_ctx_keys=CTX_LEN, ctx_tile=(q_tile0 == 0)),
        grid=(b // BP, n_q_tiles),
        in_specs=[pl.BlockSpec((BP, TM, heads * dq), lambda bi, qi: (bi, qi + q_tile0, 0)),
                  pl.BlockSpec((BP, nt, kv_heads * dq), lambda bi, qi: (bi, 0, 0)),
                  pl.BlockSpec((BP, kv_heads * dv, nt), lambda bi, qi: (bi, 0, 0))],
        out_specs=pl.BlockSpec((BP, TM, heads * dv), lambda bi, qi: (bi, qi, 0)),
        out_shape=jax.ShapeDtypeStruct((b, out_rows, heads * dv), BF16),
        compiler_params=_cparams(("parallel", "arbitrary")),
        name="attention",
    )(q, k, vt)


def _mix_ffn_kernel(*refs, dual, first_tiles, last_tiles, final):
    if dual:
        x_ref, ctx_ref, *refs = refs
    else:
        (x_ref, *refs), ctx_ref = refs, None
    (hp_ref, hq_ref, a_ref, ap_ref, aq_ref, b_ref, bprev_ref, bnext_ref, m_ref, wa_ref, wb_ref,
     wup_ref, cw_ref, cb_ref, wdn_ref, gfin_ref, o_ref, la_ref, lb_ref, h1_ref, xe_ref, acc_ref) = refs
    t = pl.program_id(0)
    has_prev = functools.reduce(jnp.logical_and, [t != ft for ft in first_tiles])
    has_next = functools.reduce(jnp.logical_and, [t != lt for lt in last_tiles])
    ext = TM + 2 * HALO
    main, prev, nxt_ = slice(0, TM), slice(TM, TM + HALO), slice(TM + HALO, ext)
    la_ref[main], la_ref[prev], la_ref[nxt_] = a_ref[0], ap_ref[0], aq_ref[0]
    lb_ref[main], lb_ref[prev], lb_ref[nxt_] = b_ref[0], bprev_ref[0], bnext_ref[0]
    m = m_ref[0, 0]
    h_main = _stream_rows(t, x_ref, ctx_ref, 0) if dual else x_ref[0]
    for rs in (slice(0, ext // 2), slice(ext // 2, ext)):
        h1_ref[rs] = _dot(la_ref[rs], wa_ref[...]) + _dot(lb_ref[rs], wb_ref[...])
    for r, h_in in ((main, h_main), (prev, hp_ref[0]), (nxt_, hq_ref[0])):
        h1 = h_in + m[2:3] * h1_ref[r]
        h1_ref[r] = h1
        xe_ref[r] = (_rms(h1) * (1.0 + m[4:5]) + m[3:4]).astype(BF16)

    xe = xe_ref[...]
    x = xe[:TM]
    row = lax.broadcasted_iota(jnp.int32, (TM, FFN_CHUNK), 0)
    f = wdn_ref.shape[0]
    n_chunks = f // FFN_CHUNK
    cols = lambda c: slice(FFN_CHUNK * c, FFN_CHUNK * (c + 1))

    def up(c):
        return (_dot(xe, wup_ref[:, cols(c)]),
                _dot(x, wup_ref[:, f + FFN_CHUNK * c:f + FFN_CHUNK * (c + 1)]))

    pending = up(0)
    for c in range(n_chunks):
        ae, g = pending
        if c + 1 < n_chunks:
            pending = up(c + 1)
        a = ae[:TM]
        a_prev = jnp.where(has_prev, ae[TM + HALO - 1:TM + HALO], 0.0)
        a_next = jnp.where(has_next, ae[TM + HALO:TM + HALO + 1], 0.0)
        a_dn = jnp.where(row == 0, a_prev, pltpu.roll(a, 1, 0))
        a_up = jnp.where(row == TM - 1, a_next, pltpu.roll(a, TM - 1, 0))
        cw = cw_ref[:, cols(c)]
        conv = cb_ref[:, cols(c)] + a_dn * cw[0:1] + a * cw[1:2] + a_up * cw[2:3]
        part = _dot((jax.nn.gelu(conv) * g).astype(BF16), wdn_ref[cols(c), :])
        if c == 0:
            acc_ref[...] = part
        else:
            acc_ref[...] += part
    out = h1_ref[main] + m[5:6] * acc_ref[...]
    if final:
        out = _rms(out) * gfin_ref[...]
    o_ref[0] = out


def _mix_ffn(hs, a, bmix, msel, wa, wb, wup, cw, cb, wdn, gfin, *, dual, n_tiles,
             first_tiles, last_tiles, seg_of_tile, final):
    b, _, d = hs[0].shape
    hb = TM // HALO
    ext = TM + 2 * HALO
    c2 = lambda t, bi: (0, 0)
    single = dict(pipeline_mode=pl.Buffered(1))
    prev_blk = lambda t: jnp.maximum(t * hb - 1, 0)
    next_blk = lambda t, n_rows: jnp.minimum((t + 1) * hb, n_rows // HALO - 1)
    if dual:
        n_x = hs[0].shape[1]
        lat = lambda t: jnp.maximum(t - 1, 0)
        h_specs = [pl.BlockSpec((1, TM, d), lambda t, bi: (bi, lat(t), 0)),
                   pl.BlockSpec((1, CTX_LEN, d), lambda t, bi: (jnp.where(t == 0, bi, 0), 0, 0)),
                   pl.BlockSpec((1, HALO, d), lambda t, bi: (bi, prev_blk(lat(t)), 0)),
                   pl.BlockSpec((1, HALO, d), lambda t, bi: (bi, next_blk(lat(t), n_x), 0))]
        h_args = [hs[0], hs[1], hs[0], hs[0]]
    else:
        n_h = hs[0].shape[1]
        h_specs = [pl.BlockSpec((1, TM, d), lambda t, bi: (bi, t + 1, 0)),
                   pl.BlockSpec((1, HALO, d), lambda t, bi: (bi, prev_blk(t + 1), 0)),
                   pl.BlockSpec((1, HALO, d), lambda t, bi: (bi, next_blk(t + 1, n_h), 0))]
        h_args = [hs[0], hs[0], hs[0]]

    def mixer_specs(arr):
        w, n_rows = arr.shape[2], arr.shape[1]
        return [pl.BlockSpec((1, TM, w), lambda t, bi: (bi, t, 0)),
                pl.BlockSpec((1, HALO, w), lambda t, bi: (bi, prev_blk(t), 0)),
                pl.BlockSpec((1, HALO, w), lambda t, bi: (bi, next_blk(t, n_rows), 0))]

    return pl.pallas_call(
        functools.partial(_mix_ffn_kernel, dual=dual, first_tiles=first_tiles, last_tiles=last_tiles, final=final),
        grid=(n_tiles, b),
        in_specs=h_specs + mixer_specs(a) + mixer_specs(bmix)
                 + [pl.BlockSpec((1, 1, 6, d), lambda t, bi: (bi, seg_of_tile(t), 0, 0)),
                    pl.BlockSpec(wa.shape, c2, **single),
                    pl.BlockSpec(wb.shape, c2, **single),
                    pl.BlockSpec(wup.shape, c2, **single),
                    pl.BlockSpec(cw.shape, c2, **single),
                    pl.BlockSpec(cb.shape, c2, **single),
                    pl.BlockSpec(wdn.shape, c2, **single),
                    pl.BlockSpec(gfin.shape, c2)],
        out_specs=pl.BlockSpec((1, TM, d), lambda t, bi: (bi, t, 0)),
        out_shape=jax.ShapeDtypeStruct((b, n_tiles * TM, d), F32),
        scratch_shapes=[pltpu.VMEM((ext, wa.shape[0]), BF16), pltpu.VMEM((ext, wb.shape[0]), BF16),
                        pltpu.VMEM((ext, d), F32), pltpu.VMEM((ext, d), BF16), pltpu.VMEM((TM, d), F32)],
        compiler_params=_cparams(("parallel", "parallel")),
        name="mix_ffn",
    )(*h_args, a, a, a, bmix, bmix, bmix, msel, wa, wb, wup, cw, cb, wdn, gfin)


def _proj_cd_kernel(h_ref, m_ref, win_ref, wgvt_ref, ggq_ref, ggk_ref, cos_ref, sa_ref, sb_ref,
                    rq_ref, rk_ref, rv_ref, rg_ref, gq_ref, gk_ref, gvt_ref):
    k_scale = RET_D ** -0.5
    n_ret = 4 * RET_HEADS * RET_D
    rope = functools.partial(_rope, cos=cos_ref[...], sa=sa_ref[...], sb=sb_ref[...], quarter=RET_D // 4)
    head = lambda z, base, h: z[:, base + 128 * h:base + 128 * (h + 1)]
    ms = [m_ref[j, 0] for j in range(BP)]
    xns = [(_rms(h_ref[j]) * (1.0 + m[1:2]) + m[0:1]).astype(BF16) for j, m in enumerate(ms)]
    z_gqa = [_dot(xn, win_ref[:, n_ret:]) for xn in xns]
    z_rqk = [_dot(xn, win_ref[:, :n_ret // 2]) for xn in xns]
    z_rvg = [_dot(xn, win_ref[:, n_ret // 2:n_ret]) for xn in xns]
    gvts = [_dot_nt(wgvt_ref[...], xn) for xn in xns]
    for j, z in enumerate(z_gqa):
        gq_ref[j] = jnp.concatenate([rope(_rms(head(z, 0, h)) * ggq_ref[...]) * GQA_QSCALE
                                     for h in range(GQA_HEADS)], -1).astype(BF16)
        gk_ref[j] = jnp.concatenate([rope(_rms(head(z, 512, h)) * ggk_ref[...]) for h in range(GQA_KV_HEADS)],
                                    -1).astype(BF16)
    for j, z in enumerate(z_rqk):
        rq_ref[j] = jnp.concatenate([rope(head(z, 0, h)) for h in range(RET_HEADS)], -1).astype(BF16)
        rk_ref[j] = jnp.concatenate([rope(head(z, 512, h)) * k_scale for h in range(RET_HEADS)], -1).astype(BF16)
    for j, (z, gvt) in enumerate(zip(z_rvg, gvts)):
        rv_ref[j] = z[:, :512].astype(BF16)
        rg_ref[j] = z[:, 512:]
        gvt_ref[j] = gvt.astype(BF16)


def _proj_cd(h, msel, win, wgvt, ggq, ggk, tabs):
    b, nt, d = h.shape
    const = lambda bi, t: (0, 0)
    row = lambda bi, t: (t, 0)
    tok = lambda bi, t: (bi, t, 0)
    widths = (512, 512, 512, 512, 512, 256)
    dtypes = (BF16, BF16, BF16, F32, BF16, BF16)
    n_gv = wgvt.shape[0]
    return pl.pallas_call(
        _proj_cd_kernel,
        grid=(b // BP, nt // TM),
        in_specs=[pl.BlockSpec((BP, TM, d), tok),
                  pl.BlockSpec((BP, 1, 6, d), lambda bi, t: (bi, jnp.minimum(t, 1), 0, 0)),
                  pl.BlockSpec(win.shape, const),
                  pl.BlockSpec(wgvt.shape, const),
                  pl.BlockSpec(ggq.shape, const),
                  pl.BlockSpec(ggk.shape, const),
                  pl.BlockSpec((TM, 128), row),
                  pl.BlockSpec((TM, 128), row),
                  pl.BlockSpec((TM, 128), row)],
        out_specs=[pl.BlockSpec((BP, TM, w), tok) for w in widths]
                  + [pl.BlockSpec((BP, n_gv, TM), lambda bi, t: (bi, 0, t))],
        out_shape=[jax.ShapeDtypeStruct((b, nt, w), dt) for w, dt in zip(widths, dtypes)]
                  + [jax.ShapeDtypeStruct((b, n_gv, nt), BF16)],
        compiler_params=_cparams(("parallel", "parallel")),
        name="proj_cd",
    )(h, msel, win, wgvt, ggq, ggk, *tabs)


def _retention_kernel(q_ref, k_ref, v_ref, g_ref, dl_ref, o_ref, *, n_lat):
    c = RET_CHUNK
    nc = n_lat // c
    lg = jax.nn.log_sigmoid(dl_ref[0])
    lgf, lgb = lg[0:1], lg[1:2]
    tdot = lambda a, b: lax.dot_general(a, b, (((0,), (0,)), ((), ())), preferred_element_type=F32)
    k_all, v_all = k_ref[0], v_ref[0]

    mc = lax.broadcasted_iota(jnp.int32, (CTX_LEN, RET_D), 0).astype(F32)
    kc = k_all[:CTX_LEN].astype(F32)
    vc = v_all[:CTX_LEN]
    s_f = tdot((kc * jnp.exp((CTX_LEN - 1.0 - mc) * lgf)).astype(BF16), vc)
    s_b = tdot((kc * jnp.exp(mc * lgb)).astype(BF16), vc)

    i = lax.broadcasted_iota(jnp.int32, (c, RET_D), 0).astype(F32)
    kd_f = jnp.exp((c - 1.0 - i) * lgf)
    kd_b = jnp.exp(i * lgb)
    qd_f = jnp.exp((i + 1.0) * lgf)
    qd_b = jnp.exp((c - i) * lgb)
    cd_f = jnp.exp(c * lgf)
    cd_b = jnp.exp(c * lgb)
    ii = lax.broadcasted_iota(jnp.int32, (c, c), 0)
    jj = lax.broadcasted_iota(jnp.int32, (c, c), 1)
    dist = (ii - jj).astype(F32)
    lgf_c = jnp.concatenate([lgf] * (c // RET_D), axis=-1)
    lgb_c = jnp.concatenate([lgb] * (c // RET_D), axis=-1)
    dec = (jnp.where(ii >= jj, jnp.exp(jnp.maximum(dist, 0.0) * lgf_c), 0.0)
           + jnp.where(ii <= jj, jnp.exp(jnp.maximum(-dist, 0.0) * lgb_c), 0.0))

    ks = [k_all[CTX_LEN + c * n:CTX_LEN + c * (n + 1)] for n in range(nc)]
    vs = [v_all[CTX_LEN + c * n:CTX_LEN + c * (n + 1)] for n in range(nc)]
    sf = [s_f]
    for n in range(nc - 1):
        sf.append(cd_f * sf[n] + tdot((ks[n].astype(F32) * kd_f).astype(BF16), vs[n]))
    sb = [None] * nc
    sb[nc - 1] = s_b
    for n in range(nc - 1, 0, -1):
        sb[n - 1] = cd_b * sb[n] + tdot((ks[n].astype(F32) * kd_b).astype(BF16), vs[n])

    for n in range(nc):
        rs = slice(CTX_LEN + c * n, CTX_LEN + c * (n + 1))
        qn = q_ref[0, rs]
        att = _dot_nt(qn, ks[n]) * dec
        o = (_dot(att.astype(BF16), vs[n])
             + _dot(qn, sf[n].astype(BF16)) * qd_f
             + _dot(qn, sb[n].astype(BF16)) * qd_b)
        gate = g_ref[0, rs]
        o_ref[0, c * n:c * (n + 1)] = (_rms(o) * (gate * jax.nn.sigmoid(gate))).astype(BF16)


def _retention(rq, rk, rv, rg, decay_logit, n_lat):
    b, nt, _ = rq.shape
    dl = jnp.broadcast_to(jnp.swapaxes(decay_logit, 0, 1)[:, :, None], (RET_HEADS, 2, 128))
    blk = pl.BlockSpec((1, nt, RET_D), lambda bi, h: (bi, 0, h))
    return pl.pallas_call(
        functools.partial(_retention_kernel, n_lat=n_lat),
        grid=(b, RET_HEADS),
        in_specs=[blk, blk, blk, blk, pl.BlockSpec((1, 2, 128), lambda bi, h: (h, 0, 0))],
        out_specs=pl.BlockSpec((1, n_lat, RET_D), lambda bi, h: (bi, 0, h)),
        out_shape=jax.ShapeDtypeStruct((b, n_lat, RET_HEADS * RET_D), BF16),
        compiler_params=_cparams(("parallel", "parallel")),
        name="retention",
    )(rq, rk, rv, rg, dl)


def kernel(x, c, ctx, c_ctx, w_mod, b_mod, w_in_ab, w_out_ab, s5_lam_re, s5_lam_im, s5_log_step, s5_b_re, s5_b_im, s5_c_re, s5_c_im, s5_d, s5_w_glu, s5_b_glu, mla_g_q, mla_w_uq, mla_g_kv, mla_w_ukv, w_in_cd, w_out_cd, ret_decay_logit, gqa_g_q, gqa_g_k, ffn_w_up, ffn_conv_w, ffn_conv_b, ffn_w_down, g_final):
    b, n_lat, d = x.shape
    nt = CTX_LEN + n_lat
    n_tiles = nt // TM

    rows = ((b + 1 + 7) // 8) * 8
    cond = jnp.concatenate([c, c_ctx[None], jnp.zeros((rows - b - 1, d), F32)], axis=0)
    mods = _modulation(cond, w_mod, b_mod).reshape(w_mod.shape[0], rows, 6, d)

    def mod_select(layer):
        mctx = jnp.broadcast_to(mods[layer, b][None], (b, 6, d))
        return jnp.stack([mctx, mods[layer, :b]], axis=1)

    msel = mod_select(0)
    win = jnp.pad(w_in_ab[0], ((0, 0), (0, 64))).astype(BF16)
    wuq = jnp.pad(mla_w_uq[0].reshape(MLA_Q_RANK, MLA_HEADS, MLA_NOPE + MLA_ROPE),
                  ((0, 0), (0, 0), (0, MLA_QPAD - MLA_NOPE - MLA_ROPE))).reshape(MLA_Q_RANK, -1).astype(BF16)
    wukv = mla_w_ukv[0].reshape(MLA_KV_RANK, MLA_HEADS, MLA_NOPE + MLA_V)
    wkn = wukv[:, :, :MLA_NOPE].reshape(MLA_KV_RANK, -1).astype(BF16)
    wvt = jnp.transpose(wukv[:, :, MLA_NOPE:].reshape(MLA_KV_RANK, -1)).astype(BF16)
    tabs64 = _rope_tables(n_lat, MLA_ROPE)
    u_tm, q, k, vt = _proj_ab(x, ctx, msel, win, mla_g_q[0][None], wuq, mla_g_kv[0][None], wkn, wvt, tabs64)

    lam_re, lam_im, bbt_re, bbt_im = _s5_prep(s5_lam_re[0], s5_lam_im[0], s5_log_step[0], s5_b_re[0], s5_b_im[0])
    lam = jnp.stack([lam_re.reshape(2, S5_CH), lam_im.reshape(2, S5_CH)], axis=1)
    bshape = (2, S5_COLS, 8, S5_GROUP, S5_STATE)
    bcat = jnp.concatenate([_block_diag(bbt_re.reshape(bshape)), _block_diag(bbt_im.reshape(bshape))],
                           axis=-1).astype(BF16)
    cshape = (2, S5_OUT_COLS, S5_GROUPS // S5_OUT_COLS, S5_GROUP, S5_STATE)
    cre = _block_diag(jnp.swapaxes(s5_c_re[0].reshape(cshape), -1, -2)).astype(BF16)
    cim = _block_diag(jnp.swapaxes(s5_c_im[0].reshape(cshape), -1, -2)).astype(BF16)
    s5_out = _s5_mixer(u_tm, b, bcat, lam, cre, cim,
                       s5_d[0].reshape(1, S5_WIDTH), s5_w_glu[0].astype(BF16), s5_b_glu[0][None])

    att = _attention(q, k, vt, heads=MLA_HEADS, kv_heads=MLA_HEADS, dq=MLA_QPAD, dv=MLA_V,
                     q_tile0=0, n_q_tiles=n_tiles, out_rows=nt)
    wo = w_out_ab[0].astype(BF16)
    seg0 = lambda t: jnp.minimum(t, 1)
    gfin = g_final[None]
    h = _mix_ffn((x, ctx), s5_out, att, msel, wo[:S5_WIDTH], wo[S5_WIDTH:], ffn_w_up[0].astype(BF16),
                 ffn_conv_w[0], ffn_conv_b[0][None], ffn_w_down[0].astype(BF16), gfin,
                 dual=True, n_tiles=n_tiles, first_tiles=(0, 1),
                 last_tiles=(0, n_tiles - 1), seg_of_tile=seg0, final=False)

    msel = mod_select(1)
    tabs128 = _rope_tables(n_lat, RET_D)
    n_gv = GQA_KV_HEADS * GQA_D
    wcd = w_in_cd[0]
    rq, rk, rv, rg, gq, gk, gvt = _proj_cd(h, msel, wcd[:, :-n_gv].astype(BF16),
                                           jnp.transpose(wcd[:, -n_gv:]).astype(BF16),
                                           gqa_g_q[0][None], gqa_g_k[0][None], tabs128)
    ret = _retention(rq, rk, rv, rg, ret_decay_logit[0], n_lat)
    att = _attention(gq, gk, gvt, heads=GQA_HEADS, kv_heads=GQA_KV_HEADS, dq=GQA_D, dv=GQA_D,
                     q_tile0=1, n_q_tiles=n_tiles - 1, out_rows=n_lat)
    wo = w_out_cd[0].astype(BF16)
    n_half = RET_HEADS * RET_D
    return _mix_ffn((h,), ret, att, msel, wo[:n_half], wo[n_half:], ffn_w_up[1].astype(BF16),
                    ffn_conv_w[1], ffn_conv_b[1][None], ffn_w_down[1].astype(BF16), gfin,
                    dual=False, n_tiles=n_tiles - 1, first_tiles=(0,),
                    last_tiles=(n_tiles - 2,), seg_of_tile=lambda t: 1, final=True)
```

```python
import functools

import numpy as np
import jax
import jax.numpy as jnp
from jax import lax
from jax.experimental import pallas as pl
from jax.experimental.pallas import tpu as pltpu

F32 = jnp.float32
BF16 = jnp.bfloat16

GRID_W = 64
CTX_LEN = 256
EPS = 1e-6
ROPE_THETA = 10000.0

S5_WIDTH = 512
S5_GROUP = 16
S5_GROUPS = 32
S5_STATE = 64
S5_CH = S5_GROUPS * S5_STATE
S5_COLS = 4
S5_OUT_COLS = 2
MLA_HEADS = 4
MLA_NOPE = 128
MLA_ROPE = 64
MLA_V = 128
MLA_Q_RANK = 384
MLA_KV_RANK = 256
MLA_QPAD = 256
RET_HEADS = 4
RET_D = 128
RET_CHUNK = 256
GQA_HEADS = 4
GQA_KV_HEADS = 2
GQA_D = 128
LOG2E = 1.4426950408889634
MLA_QSCALE = (MLA_NOPE + MLA_ROPE) ** -0.5 * LOG2E
GQA_QSCALE = GQA_D ** -0.5 * LOG2E
FFN_CHUNK = 256

TM = 256
HALO = 16
BP = 2
KEY_SPLIT = 2
SCORE_AHEAD = 3
S5_LC = 32
S5_SPLIT = 2
SLABS = S5_WIDTH // 128
VMEM_LIMIT = 56 * 1024 * 1024


def _cparams(sem):
    return pltpu.CompilerParams(dimension_semantics=sem, vmem_limit_bytes=VMEM_LIMIT)


def _rms(x):
    return x * lax.rsqrt(jnp.mean(x * x, axis=-1, keepdims=True) + EPS)


def _dot(a, b):
    return jnp.dot(a, b, preferred_element_type=F32)


def _dot_nt(a, b):
    return lax.dot_general(a, b, (((1,), (1,)), ((), ())), preferred_element_type=F32)


def _rope(x, cos, sa, sb, quarter):
    return x * cos + pltpu.roll(x, quarter, 1) * sa + pltpu.roll(x, 128 - quarter, 1) * sb


def _rope_tables(n_lat, dim):
    quarter = dim // 4
    inv = (ROPE_THETA ** (-np.arange(quarter, dtype=np.float32) / quarter)).astype(np.float32)
    t = np.arange(n_lat)
    ang_r = (t // GRID_W).astype(np.float32)[:, None] * inv
    ang_c = (t % GRID_W).astype(np.float32)[:, None] * inv
    ang = np.concatenate([ang_r, ang_r, ang_c, ang_c], axis=-1).astype(np.float32)
    cos, sin = np.cos(ang), np.sin(ang)
    lane_q = (np.arange(dim) // quarter) % 2
    nt = CTX_LEN + n_lat
    cos_t = np.ones((nt, 128), np.float32)
    sa_t = np.zeros((nt, 128), np.float32)
    sb_t = np.zeros((nt, 128), np.float32)
    cos_t[CTX_LEN:, :dim] = cos
    sa_t[CTX_LEN:, :dim] = np.where(lane_q == 1, sin, 0.0)
    sb_t[CTX_LEN:, :dim] = np.where(lane_q == 0, -sin, 0.0)
    return jnp.asarray(cos_t), jnp.asarray(sa_t), jnp.asarray(sb_t)


def _mod_kernel(c_ref, w_ref, b_ref, o_ref):
    c = c_ref[...]
    s = c * jax.nn.sigmoid(c)
    o_ref[0] = _dot(s.astype(BF16), w_ref[0].astype(BF16)) + b_ref[0]


def _modulation(cond, w_mod, b_mod):
    depth, d, n6 = w_mod.shape
    rows = cond.shape[0]
    tn = 1024
    return pl.pallas_call(
        _mod_kernel,
        grid=(depth, n6 // tn),
        in_specs=[pl.BlockSpec((rows, d), lambda l, j: (0, 0)),
                  pl.BlockSpec((1, d, tn), lambda l, j: (l, 0, j)),
                  pl.BlockSpec((1, 1, tn), lambda l, j: (l, 0, j))],
        out_specs=pl.BlockSpec((1, rows, tn), lambda l, j: (l, 0, j)),
        out_shape=jax.ShapeDtypeStruct((depth, rows, n6), F32),
        compiler_params=_cparams(("arbitrary", "arbitrary")),
        name="modulation",
    )(cond, w_mod, b_mod.reshape(depth, 1, n6))


def _stream_rows(t, x_ref, ctx_ref, j):
    return jnp.where(t == 0, ctx_ref[j], x_ref[j])


def _proj_ab_kernel(x_ref, ctx_ref, m_ref, win_ref, gq_ref, wuq_ref, gkv_ref, wkn_ref, wvt_ref,
                    cos_ref, sa_ref, sb_ref, u_ref, q_ref, k_ref, vt_ref):
    t, pair = pl.program_id(0), pl.program_id(1)
    nb = u_ref.shape[1] // TM
    cos, sa, sb = cos_ref[...], sa_ref[...], sb_ref[...]
    ms = [m_ref[j, 0] for j in range(BP)]
    xns = [(_rms(_stream_rows(t, x_ref, ctx_ref, j)) * (1.0 + m[1:2]) + m[0:1]).astype(BF16)
           for j, m in enumerate(ms)]
    zs = [_dot(xn, win_ref[...]) for xn in xns]
    cqns = [(_rms(z[:, 512:896]) * gq_ref[...]).astype(BF16) for z in zs]
    ckvns = [(_rms(z[:, 896:1152]) * gkv_ref[...]).astype(BF16) for z in zs]
    qs = [_dot(cqn, wuq_ref[...]) * MLA_QSCALE for cqn in cqns]
    kns = [_dot(ckvn, wkn_ref[...]) for ckvn in ckvns]
    vts = [_dot_nt(wvt_ref[...], ckvn) for ckvn in ckvns]
    for j, (z, q, kn, vt) in enumerate(zip(zs, qs, kns, vts)):
        for sl in range(SLABS):
            u_ref[sl, pl.ds(pair * BP + j, TM, stride=nb), :] = z[:, 128 * sl:128 * (sl + 1)]
        krr = _rope(z[:, 1152:1280], cos, sa, sb, MLA_ROPE // 4)
        qparts, kparts = [], []
        for h in range(MLA_HEADS):
            o = MLA_QPAD * h
            qparts += [q[:, o:o + 128], _rope(q[:, o + 128:o + 256], cos, sa, sb, MLA_ROPE // 4)]
            kparts += [kn[:, 128 * h:128 * (h + 1)], krr]
        q_ref[j] = jnp.concatenate(qparts, axis=-1).astype(BF16)
        k_ref[j] = jnp.concatenate(kparts, axis=-1).astype(BF16)
        vt_ref[j] = vt.astype(BF16)


def _proj_ab(x, ctx, msel, win, gq, wuq, gkv, wkn, wvt, tabs):
    b, n_lat, d = x.shape
    nt = CTX_LEN + n_lat
    const = lambda t, bi: (0, 0)
    row = lambda t, bi: (t, 0)
    tok = lambda t, bi: (bi, t, 0)
    return pl.pallas_call(
        _proj_ab_kernel,
        grid=(nt // TM, b // BP),
        in_specs=[pl.BlockSpec((BP, TM, d), lambda t, bi: (bi, jnp.maximum(t - 1, 0), 0)),
                  pl.BlockSpec((BP, CTX_LEN, d), lambda t, bi: (jnp.where(t == 0, bi, 0), 0, 0)),
                  pl.BlockSpec((BP, 1, 6, d), lambda t, bi: (bi, jnp.minimum(t, 1), 0, 0)),
                  pl.BlockSpec(win.shape, const),
                  pl.BlockSpec(gq.shape, const),
                  pl.BlockSpec(wuq.shape, const),
                  pl.BlockSpec(gkv.shape, const),
                  pl.BlockSpec(wkn.shape, const),
                  pl.BlockSpec(wvt.shape, const),
                  pl.BlockSpec((TM, 128), row),
                  pl.BlockSpec((TM, 128), row),
                  pl.BlockSpec((TM, 128), row)],
        out_specs=[pl.BlockSpec((SLABS, TM * b, 128), lambda t, bi: (0, t, 0)),
                   pl.BlockSpec((BP, TM, MLA_HEADS * MLA_QPAD), tok),
                   pl.BlockSpec((BP, TM, MLA_HEADS * MLA_QPAD), tok),
                   pl.BlockSpec((BP, MLA_HEADS * MLA_V, TM), lambda t, bi: (bi, 0, t))],
        out_shape=[jax.ShapeDtypeStruct((SLABS, nt * b, 128), F32),
                   jax.ShapeDtypeStruct((b, nt, MLA_HEADS * MLA_QPAD), BF16),
                   jax.ShapeDtypeStruct((b, nt, MLA_HEADS * MLA_QPAD), BF16),
                   jax.ShapeDtypeStruct((b, MLA_HEADS * MLA_V, nt), BF16)],
        compiler_params=_cparams(("arbitrary", "arbitrary")),
        name="proj_ab",
    )(x, ctx, msel, win, gq, wuq, gkv, wkn, wvt, *tabs)


def _s5_prep_kernel(lre_ref, lim_ref, ls_ref, bre_ref, bim_ref, ore_ref, oim_ref, obre_ref, obim_ref):
    lre, lim = lre_ref[...], lim_ref[...]
    dt = jnp.exp(ls_ref[...])
    ar, ai = lre * dt, lim * dt
    mag = jnp.exp(ar)
    lb_re, lb_im = mag * jnp.cos(ai), mag * jnp.sin(ai)
    den = lre * lre + lim * lim
    cf_re = ((lb_re - 1.0) * lre + lb_im * lim) / den
    cf_im = (lb_im * lre - (lb_re - 1.0) * lim) / den
    ore_ref[...] = lb_re
    oim_ref[...] = lb_im
    bre, bim = bre_ref[...], bim_ref[...]
    obre_ref[...] = cf_re * bre - cf_im * bim
    obim_ref[...] = cf_re * bim + cf_im * bre


def _s5_prep(lam_re, lam_im, log_step, b_re, b_im):
    g2 = 2 * S5_GROUPS
    args = (lam_re.reshape(g2, 1, S5_STATE), lam_im.reshape(g2, 1, S5_STATE), log_step.reshape(g2, 1, 1),
            jnp.swapaxes(b_re, -1, -2).reshape(g2, S5_GROUP, S5_STATE),
            jnp.swapaxes(b_im, -1, -2).reshape(g2, S5_GROUP, S5_STATE))
    full = lambda a: pl.BlockSpec(a.shape, lambda i: (0,) * a.ndim)
    lam_sds = jax.ShapeDtypeStruct((g2, 1, S5_STATE), F32)
    bb_sds = jax.ShapeDtypeStruct((g2, S5_GROUP, S5_STATE), F32)
    return pl.pallas_call(
        _s5_prep_kernel,
        grid=(1,),
        in_specs=[full(a) for a in args],
        out_specs=[full(lam_sds), full(lam_sds), full(bb_sds), full(bb_sds)],
        out_shape=[lam_sds, lam_sds, bb_sds, bb_sds],
        name="s5_prep",
    )(*args)


def _block_diag(blocks):
    n, r, c = blocks.shape[-3:]
    eye = jnp.eye(n, dtype=blocks.dtype)
    out = blocks[..., :, :, None, :] * eye[:, None, :, None]
    return out.reshape(blocks.shape[:-3] + (n * r, n * c))


def _unslab(ref):
    return jnp.concatenate([ref[sl] for sl in range(SLABS)], axis=-1)


def _s5_chunk(u_ref, bcat_ref, lam_ref, cre_ref, cim_ref, scratch, lc, nb, reverse):
    st_ref, bre_ref, bim_ref = scratch
    gb = nb // S5_SPLIT
    rows_g = lc * gb
    u3 = _unslab(u_ref).reshape(lc, nb, S5_WIDTH)
    order = range(lc - 1, -1, -1) if reverse else range(lc)
    ys = []
    us = [u3[:, gb * g:gb * (g + 1)].reshape(rows_g, S5_WIDTH).astype(BF16) for g in range(S5_SPLIT)]
    for g, u in enumerate(us):
        for c in range(S5_COLS):
            bu = _dot(u[:, 128 * c:128 * (c + 1)], bcat_ref[c])
            bre_ref[g, :, 512 * c:512 * (c + 1)] = bu[:, :512]
            bim_ref[g, :, 512 * c:512 * (c + 1)] = bu[:, 512:]
    for g in range(S5_SPLIT):
        bs = slice(gb * g, gb * (g + 1))
        for c in range(S5_COLS):
            cs = slice(512 * c, 512 * (c + 1))
            lre = jnp.broadcast_to(lam_ref[0:1, cs], (gb, 512))
            lim = jnp.broadcast_to(lam_ref[1:2, cs], (gb, 512))
            sre, sim = st_ref[0, bs, cs], st_ref[1, bs, cs]
            for t in order:
                rs = slice(t * gb, (t + 1) * gb)
                nre = lre * sre - lim * sim + bre_ref[g, rs, cs]
                nim = lre * sim + lim * sre + bim_ref[g, rs, cs]
                bre_ref[g, rs, cs] = nre
                bim_ref[g, rs, cs] = nim
                sre, sim = nre, nim
            st_ref[0, bs, cs] = sre
            st_ref[1, bs, cs] = sim
        yg = []
        for c in range(S5_OUT_COLS):
            cs = slice(1024 * c, 1024 * (c + 1))
            yg.append(_dot(bre_ref[g, :, cs].astype(BF16), cre_ref[c])
                      - _dot(bim_ref[g, :, cs].astype(BF16), cim_ref[c]))
        ys.append(jnp.concatenate(yg, axis=-1).reshape(lc, gb, S5_WIDTH))
    return jnp.concatenate(ys, axis=1).reshape(lc * nb, S5_WIDTH)


def _s5_fwd_kernel(u_ref, bcat_ref, lam_ref, cre_ref, cim_ref, y_ref, *scratch, lc, nb):
    @pl.when(pl.program_id(0) == 0)
    def _():
        scratch[0][...] = jnp.zeros(scratch[0].shape, F32)

    y_ref[...] = _s5_chunk(u_ref, bcat_ref, lam_ref, cre_ref, cim_ref, scratch, lc, nb, False)


def _s5_bwd_kernel(u_ref, yf_ref, bcat_ref, lam_ref, cre_ref, cim_ref, d_ref, wglu_ref, bglu_ref, o_ref,
                   oslab_ref, *scratch, lc, nb):
    @pl.when(pl.program_id(0) == 0)
    def _():
        scratch[0][...] = jnp.zeros(scratch[0].shape, F32)

    yb = _s5_chunk(u_ref, bcat_ref, lam_ref, cre_ref, cim_ref, scratch, lc, nb, True)
    y = d_ref[...] * _unslab(u_ref) + yf_ref[...] + yb
    g = jax.nn.gelu(y)
    out = g * jax.nn.sigmoid(_dot(g.astype(BF16), wglu_ref[...]) + bglu_ref[...])
    for sl in range(SLABS):
        oslab_ref[sl] = out[:, 128 * sl:128 * (sl + 1)]
    for j in range(nb):
        o_ref[j] = jnp.concatenate([oslab_ref[sl, pl.ds(j, lc, stride=nb), :] for sl in range(SLABS)],
                                   axis=-1).astype(BF16)


def _s5_mixer(u_tm, nb, bcat, lam, cre, cim, d_skip, w_glu, b_glu):
    rows_total = u_tm.shape[1]
    lc = S5_LC
    rows = lc * nb
    n_steps = rows_total // rows
    n_ctx = CTX_LEN // lc
    scratch = [pltpu.VMEM((2, nb, S5_CH), F32), pltpu.VMEM((S5_SPLIT, rows // S5_SPLIT, S5_CH), F32),
               pltpu.VMEM((S5_SPLIT, rows // S5_SPLIT, S5_CH), F32)]
    fwd_blk = pl.BlockSpec((rows, S5_WIDTH), lambda i: (i, 0))
    fwd_slab = pl.BlockSpec((SLABS, rows, 128), lambda i: (0, i, 0))
    yf = pl.pallas_call(
        functools.partial(_s5_fwd_kernel, lc=lc, nb=nb),
        grid=(n_steps,),
        in_specs=[fwd_slab] + [pl.BlockSpec((None,) + a.shape[1:], lambda i, n=a.ndim: (0,) * n)
                              for a in (bcat, lam, cre, cim)],
        out_specs=fwd_blk,
        out_shape=jax.ShapeDtypeStruct((rows_total, S5_WIDTH), F32),
        scratch_shapes=scratch,
        compiler_params=_cparams(("arbitrary",)),
        name="s5_fwd",
    )(u_tm, bcat, lam, cre, cim)

    def chunk(i):
        return jnp.where(i < n_ctx, n_ctx - 1 - i, n_steps - 1 - (i - n_ctx))

    bwd_blk = pl.BlockSpec((rows, S5_WIDTH), lambda i: (chunk(i), 0))
    bwd_slab = pl.BlockSpec((SLABS, rows, 128), lambda i: (0, chunk(i), 0))
    whole = lambda a: pl.BlockSpec(a.shape, lambda i, n=a.ndim: (0,) * n)
    return pl.pallas_call(
        functools.partial(_s5_bwd_kernel, lc=lc, nb=nb),
        grid=(n_steps,),
        in_specs=[bwd_slab, bwd_blk] + [pl.BlockSpec((None,) + a.shape[1:], lambda i, n=a.ndim: (1,) + (0,) * (n - 1))
                                       for a in (bcat, lam, cre, cim)]
                 + [whole(d_skip), whole(w_glu), whole(b_glu)],
        out_specs=pl.BlockSpec((nb, lc, S5_WIDTH), lambda i: (0, chunk(i), 0)),
        out_shape=jax.ShapeDtypeStruct((nb, rows_total // nb, S5_WIDTH), BF16),
        scratch_shapes=[pltpu.VMEM((SLABS, rows, 128), F32)] + scratch,
        compiler_params=_cparams(("arbitrary",)),
        name="s5_bwd_glu",
    )(u_tm, yf, bcat, lam, cre, cim, d_skip, w_glu, b_glu)


def _attn_kernel(q_ref, k_ref, vt_ref, o_ref, *, heads, rep, dq, dv, n_ctx_keys, ctx_tile):
    def run(nk):
        nb = nk // KEY_SPLIT
        kb = [slice(nb * i, nb * (i + 1)) for i in range(KEY_SPLIT)]
        units = [(j, h) for j in range(BP) for h in range(heads)]

        def scores(j, h):
            g = h // rep
            q = q_ref[j, :, dq * h:dq * (h + 1)]
            return [_dot_nt(k_ref[j, r, dq * g:dq * (g + 1)], q) for r in kb]

        outs = [[] for _ in range(BP)]
        pending = [scores(*u) for u in units[:SCORE_AHEAD]]
        for i, (j, h) in enumerate(units):
            ss = pending.pop(0)
            if i + SCORE_AHEAD < len(units):
                pending.append(scores(*units[i + SCORE_AHEAD]))
            m = functools.reduce(jnp.maximum, [jnp.max(s, axis=0, keepdims=True) for s in ss])
            ps = [jnp.exp2(s - m) for s in ss]
            l = sum(jnp.sum(p, axis=0, keepdims=True) for p in ps)
            g = h // rep
            ot = sum(_dot(vt_ref[j, dv * g:dv * (g + 1), r], p.astype(BF16)) for r, p in zip(kb, ps))
            outs[j].append((ot / l).T)
        for j in range(BP):
            o_ref[j] = jnp.concatenate(outs[j], axis=-1).astype(o_ref.dtype)

    nk_all = k_ref.shape[1]
    if ctx_tile:
        qi = pl.program_id(1)

        @pl.when(qi == 0)
        def _():
            run(n_ctx_keys)

        @pl.when(qi > 0)
        def _():
            run(nk_all)
    else:
        run(nk_all)


def _attention(q, k, vt, *, heads, kv_heads, dq, dv, q_tile0, n_q_tiles, out_rows):
    b, nt, _ = q.shape
    return pl.pallas_call(
        functools.partial(_attn_kernel, heads=heads, rep=heads // kv_heads, dq=dq, dv=dv,
                          n_ctx_keys=CTX_LEN, ctx_tile=(q_tile0 == 0)),
        grid=(b // BP, n_q_tiles),
        in_specs=[pl.BlockSpec((BP, TM, heads * dq), lambda bi, qi: (bi, qi + q_tile0, 0)),
                  pl.BlockSpec((BP, nt, kv_heads * dq), lambda bi, qi: (bi, 0, 0)),
                  pl.BlockSpec((BP, kv_heads * dv, nt), lambda bi, qi: (bi, 0, 0))],
        out_specs=pl.BlockSpec((BP, TM, heads * dv), lambda bi, qi: (bi, qi, 0)),
        out_shape=jax.ShapeDtypeStruct((b, out_rows, heads * dv), BF16),
        compiler_params=_cparams(("parallel", "arbitrary")),
        name="attention",
    )(q, k, vt)


def _mix_ffn_kernel(*refs, dual, first_tiles, last_tiles, final):
    if dual:
        x_ref, ctx_ref, *refs = refs
    else:
        (x_ref, *refs), ctx_ref = refs, None
    (hp_ref, hq_ref, a_ref, ap_ref, aq_ref, b_ref, bprev_ref, bnext_ref, m_ref, wa_ref, wb_ref,
     wup_ref, cw_ref, cb_ref, wdn_ref, gfin_ref, o_ref, la_ref, lb_ref, h1_ref, xe_ref, acc_ref) = refs
    t = pl.program_id(0)
    has_prev = functools.reduce(jnp.logical_and, [t != ft for ft in first_tiles])
    has_next = functools.reduce(jnp.logical_and, [t != lt for lt in last_tiles])
    ext = TM + 2 * HALO
    main, prev, nxt_ = slice(0, TM), slice(TM, TM + HALO), slice(TM + HALO, ext)
    la_ref[main], la_ref[prev], la_ref[nxt_] = a_ref[0], ap_ref[0], aq_ref[0]
    lb_ref[main], lb_ref[prev], lb_ref[nxt_] = b_ref[0], bprev_ref[0], bnext_ref[0]
    m = m_ref[0, 0]
    h_main = _stream_rows(t, x_ref, ctx_ref, 0) if dual else x_ref[0]
    for rs in (slice(0, ext // 2), slice(ext // 2, ext)):
        h1_ref[rs] = _dot(la_ref[rs], wa_ref[...]) + _dot(lb_ref[rs], wb_ref[...])
    for r, h_in in ((main, h_main), (prev, hp_ref[0]), (nxt_, hq_ref[0])):
        h1 = h_in + m[2:3] * h1_ref[r]
        h1_ref[r] = h1
        xe_ref[r] = (_rms(h1) * (1.0 + m[4:5]) + m[3:4]).astype(BF16)

    xe = xe_ref[...]
    x = xe[:TM]
    row = lax.broadcasted_iota(jnp.int32, (TM, FFN_CHUNK), 0)
    f = wdn_ref.shape[0]
    n_chunks = f // FFN_CHUNK
    cols = lambda c: slice(FFN_CHUNK * c, FFN_CHUNK * (c + 1))

    def up(c):
        return (_dot(xe, wup_ref[:, cols(c)]),
                _dot(x, wup_ref[:, f + FFN_CHUNK * c:f + FFN_CHUNK * (c + 1)]))

    pending = up(0)
    for c in range(n_chunks):
        ae, g = pending
        if c + 1 < n_chunks:
            pending = up(c + 1)
        a = ae[:TM]
        a_prev = jnp.where(has_prev, ae[TM + HALO - 1:TM + HALO], 0.0)
        a_next = jnp.where(has_next, ae[TM + HALO:TM + HALO + 1], 0.0)
        a_dn = jnp.where(row == 0, a_prev, pltpu.roll(a, 1, 0))
        a_up = jnp.where(row == TM - 1, a_next, pltpu.roll(a, TM - 1, 0))
        cw = cw_ref[:, cols(c)]
        conv = cb_ref[:, cols(c)] + a_dn * cw[0:1] + a * cw[1:2] + a_up * cw[2:3]
        part = _dot((jax.nn.gelu(conv) * g).astype(BF16), wdn_ref[cols(c), :])
        if c == 0:
            acc_ref[...] = part
        else:
            acc_ref[...] += part
    out = h1_ref[main] + m[5:6] * acc_ref[...]
    if final:
        out = _rms(out) * gfin_ref[...]
    o_ref[0] = out


def _mix_ffn(hs, a, bmix, msel, wa, wb, wup, cw, cb, wdn, gfin, *, dual, n_tiles,
             first_tiles, last_tiles, seg_of_tile, final):
    b, _, d = hs[0].shape
    hb = TM // HALO
    ext = TM + 2 * HALO
    c2 = lambda t, bi: (0, 0)
    single = dict(pipeline_mode=pl.Buffered(1))
    prev_blk = lambda t: jnp.maximum(t * hb - 1, 0)
    next_blk = lambda t, n_rows: jnp.minimum((t + 1) * hb, n_rows // HALO - 1)
    if dual:
        n_x = hs[0].shape[1]
        lat = lambda t: jnp.maximum(t - 1, 0)
        h_specs = [pl.BlockSpec((1, TM, d), lambda t, bi: (bi, lat(t), 0)),
                   pl.BlockSpec((1, CTX_LEN, d), lambda t, bi: (jnp.where(t == 0, bi, 0), 0, 0)),
                   pl.BlockSpec((1, HALO, d), lambda t, bi: (bi, prev_blk(lat(t)), 0)),
                   pl.BlockSpec((1, HALO, d), lambda t, bi: (bi, next_blk(lat(t), n_x), 0))]
        h_args = [hs[0], hs[1], hs[0], hs[0]]
    else:
        n_h = hs[0].shape[1]
        h_specs = [pl.BlockSpec((1, TM, d), lambda t, bi: (bi, t + 1, 0)),
                   pl.BlockSpec((1, HALO, d), lambda t, bi: (bi, prev_blk(t + 1), 0)),
                   pl.BlockSpec((1, HALO, d), lambda t, bi: (bi, next_blk(t + 1, n_h), 0))]
        h_args = [hs[0], hs[0], hs[0]]

    def mixer_specs(arr):
        w, n_rows = arr.shape[2], arr.shape[1]
        return [pl.BlockSpec((1, TM, w), lambda t, bi: (bi, t, 0)),
                pl.BlockSpec((1, HALO, w), lambda t, bi: (bi, prev_blk(t), 0)),
                pl.BlockSpec((1, HALO, w), lambda t, bi: (bi, next_blk(t, n_rows), 0))]

    return pl.pallas_call(
        functools.partial(_mix_ffn_kernel, dual=dual, first_tiles=first_tiles, last_tiles=last_tiles, final=final),
        grid=(n_tiles, b),
        in_specs=h_specs + mixer_specs(a) + mixer_specs(bmix)
                 + [pl.BlockSpec((1, 1, 6, d), lambda t, bi: (bi, seg_of_tile(t), 0, 0)),
                    pl.BlockSpec(wa.shape, c2, **single),
                    pl.BlockSpec(wb.shape, c2, **single),
                    pl.BlockSpec(wup.shape, c2, **single),
                    pl.BlockSpec(cw.shape, c2, **single),
                    pl.BlockSpec(cb.shape, c2, **single),
                    pl.BlockSpec(wdn.shape, c2, **single),
                    pl.BlockSpec(gfin.shape, c2)],
        out_specs=pl.BlockSpec((1, TM, d), lambda t, bi: (bi, t, 0)),
        out_shape=jax.ShapeDtypeStruct((b, n_tiles * TM, d), F32),
        scratch_shapes=[pltpu.VMEM((ext, wa.shape[0]), BF16), pltpu.VMEM((ext, wb.shape[0]), BF16),
                        pltpu.VMEM((ext, d), F32), pltpu.VMEM((ext, d), BF16), pltpu.VMEM((TM, d), F32)],
        compiler_params=_cparams(("parallel", "parallel")),
        name="mix_ffn",
    )(*h_args, a, a, a, bmix, bmix, bmix, msel, wa, wb, wup, cw, cb, wdn, gfin)


def _proj_cd_kernel(h_ref, m_ref, win_ref, wgvt_ref, ggq_ref, ggk_ref, cos_ref, sa_ref, sb_ref,
                    rq_ref, rk_ref, rv_ref, rg_ref, gq_ref, gk_ref, gvt_ref):
    k_scale = RET_D ** -0.5
    n_ret = 4 * RET_HEADS * RET_D
    rope = functools.partial(_rope, cos=cos_ref[...], sa=sa_ref[...], sb=sb_ref[...], quarter=RET_D // 4)
    head = lambda z, base, h: z[:, base + 128 * h:base + 128 * (h + 1)]
    ms = [m_ref[j, 0] for j in range(BP)]
    xns = [(_rms(h_ref[j]) * (1.0 + m[1:2]) + m[0:1]).astype(BF16) for j, m in enumerate(ms)]
    z_gqa = [_dot(xn, win_ref[:, n_ret:]) for xn in xns]
    z_rqk = [_dot(xn, win_ref[:, :n_ret // 2]) for xn in xns]
    z_rvg = [_dot(xn, win_ref[:, n_ret // 2:n_ret]) for xn in xns]
    gvts = [_dot_nt(wgvt_ref[...], xn) for xn in xns]
    for j, z in enumerate(z_gqa):
        gq_ref[j] = jnp.concatenate([rope(_rms(head(z, 0, h)) * ggq_ref[...]) * GQA_QSCALE
                                     for h in range(GQA_HEADS)], -1).astype(BF16)
        gk_ref[j] = jnp.concatenate([rope(_rms(head(z, 512, h)) * ggk_ref[...]) for h in range(GQA_KV_HEADS)],
                                    -1).astype(BF16)
    for j, z in enumerate(z_rqk):
        rq_ref[j] = jnp.concatenate([rope(head(z, 0, h)) for h in range(RET_HEADS)], -1).astype(BF16)
        rk_ref[j] = jnp.concatenate([rope(head(z, 512, h)) * k_scale for h in range(RET_HEADS)], -1).astype(BF16)
    for j, (z, gvt) in enumerate(zip(z_rvg, gvts)):
        rv_ref[j] = z[:, :512].astype(BF16)
        rg_ref[j] = z[:, 512:]
        gvt_ref[j] = gvt.astype(BF16)


def _proj_cd(h, msel, win, wgvt, ggq, ggk, tabs):
    b, nt, d = h.shape
    const = lambda bi, t: (0, 0)
    row = lambda bi, t: (t, 0)
    tok = lambda bi, t: (bi, t, 0)
    widths = (512, 512, 512, 512, 512, 256)
    dtypes = (BF16, BF16, BF16, F32, BF16, BF16)
    n_gv = wgvt.shape[0]
    return pl.pallas_call(
        _proj_cd_kernel,
        grid=(b // BP, nt // TM),
        in_specs=[pl.BlockSpec((BP, TM, d), tok),
                  pl.BlockSpec((BP, 1, 6, d), lambda bi, t: (bi, jnp.minimum(t, 1), 0, 0)),
                  pl.BlockSpec(win.shape, const),
                  pl.BlockSpec(wgvt.shape, const),
                  pl.BlockSpec(ggq.shape, const),
                  pl.BlockSpec(ggk.shape, const),
                  pl.BlockSpec((TM, 128), row),
                  pl.BlockSpec((TM, 128), row),
                  pl.BlockSpec((TM, 128), row)],
        out_specs=[pl.BlockSpec((BP, TM, w), tok) for w in widths]
                  + [pl.BlockSpec((BP, n_gv, TM), lambda bi, t: (bi, 0, t))],
        out_shape=[jax.ShapeDtypeStruct((b, nt, w), dt) for w, dt in zip(widths, dtypes)]
                  + [jax.ShapeDtypeStruct((b, n_gv, nt), BF16)],
        compiler_params=_cparams(("parallel", "parallel")),
        name="proj_cd",
    )(h, msel, win, wgvt, ggq, ggk, *tabs)


def _retention_kernel(q_ref, k_ref, v_ref, g_ref, dl_ref, o_ref, *, n_lat):
    c = RET_CHUNK
    nc = n_lat // c
    lg = jax.nn.log_sigmoid(dl_ref[0])
    lgf, lgb = lg[0:1], lg[1:2]
    tdot = lambda a, b: lax.dot_general(a, b, (((0,), (0,)), ((), ())), preferred_element_type=F32)
    k_all, v_all = k_ref[0], v_ref[0]

    mc = lax.broadcasted_iota(jnp.int32, (CTX_LEN, RET_D), 0).astype(F32)
    kc = k_all[:CTX_LEN].astype(F32)
    vc = v_all[:CTX_LEN]
    s_f = tdot((kc * jnp.exp((CTX_LEN - 1.0 - mc) * lgf)).astype(BF16), vc)
    s_b = tdot((kc * jnp.exp(mc * lgb)).astype(BF16), vc)

    i = lax.broadcasted_iota(jnp.int32, (c, RET_D), 0).astype(F32)
    kd_f = jnp.exp((c - 1.0 - i) * lgf)
    kd_b = jnp.exp(i * lgb)
    qd_f = jnp.exp((i + 1.0) * lgf)
    qd_b = jnp.exp((c - i) * lgb)
    cd_f = jnp.exp(c * lgf)
    cd_b = jnp.exp(c * lgb)
    ii = lax.broadcasted_iota(jnp.int32, (c, c), 0)
    jj = lax.broadcasted_iota(jnp.int32, (c, c), 1)
    dist = (ii - jj).astype(F32)
    lgf_c = jnp.concatenate([lgf] * (c // RET_D), axis=-1)
    lgb_c = jnp.concatenate([lgb] * (c // RET_D), axis=-1)
    dec = (jnp.where(ii >= jj, jnp.exp(jnp.maximum(dist, 0.0) * lgf_c), 0.0)
           + jnp.where(ii <= jj, jnp.exp(jnp.maximum(-dist, 0.0) * lgb_c), 0.0))

    ks = [k_all[CTX_LEN + c * n:CTX_LEN + c * (n + 1)] for n in range(nc)]
    vs = [v_all[CTX_LEN + c * n:CTX_LEN + c * (n + 1)] for n in range(nc)]
    sf = [s_f]
    for n in range(nc - 1):
        sf.append(cd_f * sf[n] + tdot((ks[n].astype(F32) * kd_f).astype(BF16), vs[n]))
    sb = [None] * nc
    sb[nc - 1] = s_b
    for n in range(nc - 1, 0, -1):
        sb[n - 1] = cd_b * sb[n] + tdot((ks[n].astype(F32) * kd_b).astype(BF16), vs[n])

    for n in range(nc):
        rs = slice(CTX_LEN + c * n, CTX_LEN + c * (n + 1))
        qn = q_ref[0, rs]
        att = _dot_nt(qn, ks[n]) * dec
        o = (_dot(att.astype(BF16), vs[n])
             + _dot(qn, sf[n].astype(BF16)) * qd_f
             + _dot(qn, sb[n].astype(BF16)) * qd_b)
        gate = g_ref[0, rs]
        o_ref[0, c * n:c * (n + 1)] = (_rms(o) * (gate * jax.nn.sigmoid(gate))).astype(BF16)


def _retention(rq, rk, rv, rg, decay_logit, n_lat):
    b, nt, _ = rq.shape
    dl = jnp.broadcast_to(jnp.swapaxes(decay_logit, 0, 1)[:, :, None], (RET_HEADS, 2, 128))
    blk = pl.BlockSpec((1, nt, RET_D), lambda bi, h: (bi, 0, h))
    return pl.pallas_call(
        functools.partial(_retention_kernel, n_lat=n_lat),
        grid=(b, RET_HEADS),
        in_specs=[blk, blk, blk, blk, pl.BlockSpec((1, 2, 128), lambda bi, h: (h, 0, 0))],
        out_specs=pl.BlockSpec((1, n_lat, RET_D), lambda bi, h: (bi, 0, h)),
        out_shape=jax.ShapeDtypeStruct((b, n_lat, RET_HEADS * RET_D), BF16),
        compiler_params=_cparams(("parallel", "parallel")),
        name="retention",
    )(rq, rk, rv, rg, dl)


def kernel(x, c, ctx, c_ctx, w_mod, b_mod, w_in_ab, w_out_ab, s5_lam_re, s5_lam_im, s5_log_step, s5_b_re, s5_b_im, s5_c_re, s5_c_im, s5_d, s5_w_glu, s5_b_glu, mla_g_q, mla_w_uq, mla_g_kv, mla_w_ukv, w_in_cd, w_out_cd, ret_decay_logit, gqa_g_q, gqa_g_k, ffn_w_up, ffn_conv_w, ffn_conv_b, ffn_w_down, g_final):
    b, n_lat, d = x.shape
    nt = CTX_LEN + n_lat
    n_tiles = nt // TM

    rows = ((b + 1 + 7) // 8) * 8
    cond = jnp.concatenate([c, c_ctx[None], jnp.zeros((rows - b - 1, d), F32)], axis=0)
    mods = _modulation(cond, w_mod, b_mod).reshape(w_mod.shape[0], rows, 6, d)

    def mod_select(layer):
        mctx = jnp.broadcast_to(mods[layer, b][None], (b, 6, d))
        return jnp.stack([mctx, mods[layer, :b]], axis=1)

    msel = mod_select(0)
    win = jnp.pad(w_in_ab[0], ((0, 0), (0, 64))).astype(BF16)
    wuq = jnp.pad(mla_w_uq[0].reshape(MLA_Q_RANK, MLA_HEADS, MLA_NOPE + MLA_ROPE),
                  ((0, 0), (0, 0), (0, MLA_QPAD - MLA_NOPE - MLA_ROPE))).reshape(MLA_Q_RANK, -1).astype(BF16)
    wukv = mla_w_ukv[0].reshape(MLA_KV_RANK, MLA_HEADS, MLA_NOPE + MLA_V)
    wkn = wukv[:, :, :MLA_NOPE].reshape(MLA_KV_RANK, -1).astype(BF16)
    wvt = jnp.transpose(wukv[:, :, MLA_NOPE:].reshape(MLA_KV_RANK, -1)).astype(BF16)
    tabs64 = _rope_tables(n_lat, MLA_ROPE)
    u_tm, q, k, vt = _proj_ab(x, ctx, msel, win, mla_g_q[0][None], wuq, mla_g_kv[0][None], wkn, wvt, tabs64)

    lam_re, lam_im, bbt_re, bbt_im = _s5_prep(s5_lam_re[0], s5_lam_im[0], s5_log_step[0], s5_b_re[0], s5_b_im[0])
    lam = jnp.stack([lam_re.reshape(2, S5_CH), lam_im.reshape(2, S5_CH)], axis=1)
    bshape = (2, S5_COLS, 8, S5_GROUP, S5_STATE)
    bcat = jnp.concatenate([_block_diag(bbt_re.reshape(bshape)), _block_diag(bbt_im.reshape(bshape))],
                           axis=-1).astype(BF16)
    cshape = (2, S5_OUT_COLS, S5_GROUPS // S5_OUT_COLS, S5_GROUP, S5_STATE)
    cre = _block_diag(jnp.swapaxes(s5_c_re[0].reshape(cshape), -1, -2)).astype(BF16)
    cim = _block_diag(jnp.swapaxes(s5_c_im[0].reshape(cshape), -1, -2)).astype(BF16)
    s5_out = _s5_mixer(u_tm, b, bcat, lam, cre, cim,
                       s5_d[0].reshape(1, S5_WIDTH), s5_w_glu[0].astype(BF16), s5_b_glu[0][None])

    att = _attention(q, k, vt, heads=MLA_HEADS, kv_heads=MLA_HEADS, dq=MLA_QPAD, dv=MLA_V,
                     q_tile0=0, n_q_tiles=n_tiles, out_rows=nt)
    wo = w_out_ab[0].astype(BF16)
    seg0 = lambda t: jnp.minimum(t, 1)
    gfin = g_final[None]
    h = _mix_ffn((x, ctx), s5_out, att, msel, wo[:S5_WIDTH], wo[S5_WIDTH:], ffn_w_up[0].astype(BF16),
                 ffn_conv_w[0], ffn_conv_b[0][None], ffn_w_down[0].astype(BF16), gfin,
                 dual=True, n_tiles=n_tiles, first_tiles=(0, 1),
                 last_tiles=(0, n_tiles - 1), seg_of_tile=seg0, final=False)

    msel = mod_select(1)
    tabs128 = _rope_tables(n_lat, RET_D)
    n_gv = GQA_KV_HEADS * GQA_D
    wcd = w_in_cd[0]
    rq, rk, rv, rg, gq, gk, gvt = _proj_cd(h, msel, wcd[:, :-n_gv].astype(BF16),
                                           jnp.transpose(wcd[:, -n_gv:]).astype(BF16),
                                           gqa_g_q[0][None], gqa_g_k[0][None], tabs128)
    ret = _retention(rq, rk, rv, rg, ret_decay_logit[0], n_lat)
    att = _attention(gq, gk, gvt, heads=GQA_HEADS, kv_heads=GQA_KV_HEADS, dq=GQA_D, dv=GQA_D,
                     q_tile0=1, n_q_tiles=n_tiles - 1, out_rows=n_lat)
    wo = w_out_cd[0].astype(BF16)
    n_half = RET_HEADS * RET_D
    return _mix_ffn((h,), ret, att, msel, wo[:n_half], wo[n_half:], ffn_w_up[1].astype(BF16),
                    ffn_conv_w[1], ffn_conv_b[1][None], ffn_w_down[1].astype(BF16), gfin,
                    dual=False, n_tiles=n_tiles - 1, first_tiles=(0,),
                    last_tiles=(n_tiles - 2,), seg_of_tile=lambda t: 1, final=True)
```

```python
import functools

import numpy as np
import jax
import jax.numpy as jnp
from jax import lax
from jax.experimental import pallas as pl
from jax.experimental.pallas import tpu as pltpu

F32 = jnp.float32
BF16 = jnp.bfloat16

GRID_W = 64
CTX_LEN = 256
EPS = 1e-6
ROPE_THETA = 10000.0

S5_WIDTH = 512
S5_GROUP = 16
S5_GROUPS = 32
S5_STATE = 64
S5_CH = S5_GROUPS * S5_STATE
S5_COLS = 4
S5_OUT_COLS = 2
MLA_HEADS = 4
MLA_NOPE = 128
MLA_ROPE = 64
MLA_V = 128
MLA_Q_RANK = 384
MLA_KV_RANK = 256
MLA_QPAD = 256
RET_HEADS = 4
RET_D = 128
RET_CHUNK = 256
RET_HP = 2
GQA_HEADS = 4
GQA_KV_HEADS = 2
GQA_D = 128
LOG2E = 1.4426950408889634
MLA_QSCALE = (MLA_NOPE + MLA_ROPE) ** -0.5 * LOG2E
GQA_QSCALE = GQA_D ** -0.5 * LOG2E
FFN_CHUNK = 256

TM = 256
HALO = 16
BP = 2
KEY_SPLIT = 2
SCORE_AHEAD = 3
S5_LC = 32
S5_SPLIT = 2
SLABS = S5_WIDTH // 128
VMEM_LIMIT = 56 * 1024 * 1024


def _cparams(sem):
    return pltpu.CompilerParams(dimension_semantics=sem, vmem_limit_bytes=VMEM_LIMIT)


def _rms(x):
    return x * lax.rsqrt(jnp.mean(x * x, axis=-1, keepdims=True) + EPS)


def _dot(a, b):
    return jnp.dot(a, b, preferred_element_type=F32)


def _dot_nt(a, b):
    return lax.dot_general(a, b, (((1,), (1,)), ((), ())), preferred_element_type=F32)


def _rope(x, cos, sa, sb, quarter):
    return x * cos + pltpu.roll(x, quarter, 1) * sa + pltpu.roll(x, 128 - quarter, 1) * sb


def _rope_tables(n_lat, dim):
    quarter = dim // 4
    inv = (ROPE_THETA ** (-np.arange(quarter, dtype=np.float32) / quarter)).astype(np.float32)
    t = np.arange(n_lat)
    ang_r = (t // GRID_W).astype(np.float32)[:, None] * inv
    ang_c = (t % GRID_W).astype(np.float32)[:, None] * inv
    ang = np.concatenate([ang_r, ang_r, ang_c, ang_c], axis=-1).astype(np.float32)
    cos, sin = np.cos(ang), np.sin(ang)
    lane_q = (np.arange(dim) // quarter) % 2
    nt = CTX_LEN + n_lat
    cos_t = np.ones((nt, 128), np.float32)
    sa_t = np.zeros((nt, 128), np.float32)
    sb_t = np.zeros((nt, 128), np.float32)
    cos_t[CTX_LEN:, :dim] = cos
    sa_t[CTX_LEN:, :dim] = np.where(lane_q == 1, sin, 0.0)
    sb_t[CTX_LEN:, :dim] = np.where(lane_q == 0, -sin, 0.0)
    return jnp.asarray(cos_t), jnp.asarray(sa_t), jnp.asarray(sb_t)


def _mod_kernel(c_ref, w_ref, b_ref, o_ref):
    c = c_ref[...]
    s = c * jax.nn.sigmoid(c)
    o_ref[0] = _dot(s.astype(BF16), w_ref[0].astype(BF16)) + b_ref[0]


def _modulation(cond, w_mod, b_mod):
    depth, d, n6 = w_mod.shape
    rows = cond.shape[0]
    tn = 1024
    return pl.pallas_call(
        _mod_kernel,
        grid=(depth, n6 // tn),
        in_specs=[pl.BlockSpec((rows, d), lambda l, j: (0, 0)),
                  pl.BlockSpec((1, d, tn), lambda l, j: (l, 0, j)),
                  pl.BlockSpec((1, 1, tn), lambda l, j: (l, 0, j))],
        out_specs=pl.BlockSpec((1, rows, tn), lambda l, j: (l, 0, j)),
        out_shape=jax.ShapeDtypeStruct((depth, rows, n6), F32),
        compiler_params=_cparams(("arbitrary", "arbitrary")),
        name="modulation",
    )(cond, w_mod, b_mod.reshape(depth, 1, n6))


def _stream_rows(t, x_ref, ctx_ref, j):
    return jnp.where(t == 0, ctx_ref[j], x_ref[j])


def _proj_ab_kernel(x_ref, ctx_ref, m_ref, win_ref, gq_ref, wuq_ref, gkv_ref, wkn_ref, wvt_ref,
                    cos_ref, sa_ref, sb_ref, u_ref, q_ref, k_ref, vt_ref):
    t, pair = pl.program_id(0), pl.program_id(1)
    nb = u_ref.shape[1] // TM
    cos, sa, sb = cos_ref[...], sa_ref[...], sb_ref[...]
    ms = [m_ref[j, 0] for j in range(BP)]
    xns = [(_rms(_stream_rows(t, x_ref, ctx_ref, j)) * (1.0 + m[1:2]) + m[0:1]).astype(BF16)
           for j, m in enumerate(ms)]
    zs = [_dot(xn, win_ref[...]) for xn in xns]
    cqns = [(_rms(z[:, 512:896]) * gq_ref[...]).astype(BF16) for z in zs]
    ckvns = [(_rms(z[:, 896:1152]) * gkv_ref[...]).astype(BF16) for z in zs]
    qs = [_dot(cqn, wuq_ref[...]) * MLA_QSCALE for cqn in cqns]
    kns = [_dot(ckvn, wkn_ref[...]) for ckvn in ckvns]
    vts = [_dot_nt(wvt_ref[...], ckvn) for ckvn in ckvns]
    for j, (z, q, kn, vt) in enumerate(zip(zs, qs, kns, vts)):
        for sl in range(SLABS):
            u_ref[sl, pl.ds(pair * BP + j, TM, stride=nb), :] = z[:, 128 * sl:128 * (sl + 1)]
        krr = _rope(z[:, 1152:1280], cos, sa, sb, MLA_ROPE // 4)
        qparts, kparts = [], []
        for h in range(MLA_HEADS):
            o = MLA_QPAD * h
            qparts += [q[:, o:o + 128], _rope(q[:, o + 128:o + 256], cos, sa, sb, MLA_ROPE // 4)]
            kparts += [kn[:, 128 * h:128 * (h + 1)], krr]
        q_ref[j] = jnp.concatenate(qparts, axis=-1).astype(BF16)
        k_ref[j] = jnp.concatenate(kparts, axis=-1).astype(BF16)
        vt_ref[j] = vt.astype(BF16)


def _proj_ab(x, ctx, msel, win, gq, wuq, gkv, wkn, wvt, tabs):
    b, n_lat, d = x.shape
    nt = CTX_LEN + n_lat
    const = lambda t, bi: (0, 0)
    row = lambda t, bi: (t, 0)
    tok = lambda t, bi: (bi, t, 0)
    return pl.pallas_call(
        _proj_ab_kernel,
        grid=(nt // TM, b // BP),
        in_specs=[pl.BlockSpec((BP, TM, d), lambda t, bi: (bi, jnp.maximum(t - 1, 0), 0)),
                  pl.BlockSpec((BP, CTX_LEN, d), lambda t, bi: (jnp.where(t == 0, bi, 0), 0, 0)),
                  pl.BlockSpec((BP, 1, 6, d), lambda t, bi: (bi, jnp.minimum(t, 1), 0, 0)),
                  pl.BlockSpec(win.shape, const),
                  pl.BlockSpec(gq.shape, const),
                  pl.BlockSpec(wuq.shape, const),
                  pl.BlockSpec(gkv.shape, const),
                  pl.BlockSpec(wkn.shape, const),
                  pl.BlockSpec(wvt.shape, const),
                  pl.BlockSpec((TM, 128), row),
                  pl.BlockSpec((TM, 128), row),
                  pl.BlockSpec((TM, 128), row)],
        out_specs=[pl.BlockSpec((SLABS, TM * b, 128), lambda t, bi: (0, t, 0)),
                   pl.BlockSpec((BP, TM, MLA_HEADS * MLA_QPAD), tok),
                   pl.BlockSpec((BP, TM, MLA_HEADS * MLA_QPAD), tok),
                   pl.BlockSpec((BP, MLA_HEADS * MLA_V, TM), lambda t, bi: (bi, 0, t))],
        out_shape=[jax.ShapeDtypeStruct((SLABS, nt * b, 128), F32),
                   jax.ShapeDtypeStruct((b, nt, MLA_HEADS * MLA_QPAD), BF16),
                   jax.ShapeDtypeStruct((b, nt, MLA_HEADS * MLA_QPAD), BF16),
                   jax.ShapeDtypeStruct((b, MLA_HEADS * MLA_V, nt), BF16)],
        compiler_params=_cparams(("arbitrary", "arbitrary")),
        name="proj_ab",
    )(x, ctx, msel, win, gq, wuq, gkv, wkn, wvt, *tabs)


def _s5_prep_kernel(lre_ref, lim_ref, ls_ref, bre_ref, bim_ref, ore_ref, oim_ref, obre_ref, obim_ref):
    lre, lim = lre_ref[...], lim_ref[...]
    dt = jnp.exp(ls_ref[...])
    ar, ai = lre * dt, lim * dt
    mag = jnp.exp(ar)
    lb_re, lb_im = mag * jnp.cos(ai), mag * jnp.sin(ai)
    den = lre * lre + lim * lim
    cf_re = ((lb_re - 1.0) * lre + lb_im * lim) / den
    cf_im = (lb_im * lre - (lb_re - 1.0) * lim) / den
    ore_ref[...] = lb_re
    oim_ref[...] = lb_im
    bre, bim = bre_ref[...], bim_ref[...]
    obre_ref[...] = cf_re * bre - cf_im * bim
    obim_ref[...] = cf_re * bim + cf_im * bre


def _s5_prep(lam_re, lam_im, log_step, b_re, b_im):
    g2 = 2 * S5_GROUPS
    args = (lam_re.reshape(g2, 1, S5_STATE), lam_im.reshape(g2, 1, S5_STATE), log_step.reshape(g2, 1, 1),
            jnp.swapaxes(b_re, -1, -2).reshape(g2, S5_GROUP, S5_STATE),
            jnp.swapaxes(b_im, -1, -2).reshape(g2, S5_GROUP, S5_STATE))
    full = lambda a: pl.BlockSpec(a.shape, lambda i: (0,) * a.ndim)
    lam_sds = jax.ShapeDtypeStruct((g2, 1, S5_STATE), F32)
    bb_sds = jax.ShapeDtypeStruct((g2, S5_GROUP, S5_STATE), F32)
    return pl.pallas_call(
        _s5_prep_kernel,
        grid=(1,),
        in_specs=[full(a) for a in args],
        out_specs=[full(lam_sds), full(lam_sds), full(bb_sds), full(bb_sds)],
        out_shape=[lam_sds, lam_sds, bb_sds, bb_sds],
        name="s5_prep",
    )(*args)


def _block_diag(blocks):
    n, r, c = blocks.shape[-3:]
    eye = jnp.eye(n, dtype=blocks.dtype)
    out = blocks[..., :, :, None, :] * eye[:, None, :, None]
    return out.reshape(blocks.shape[:-3] + (n * r, n * c))


def _unslab(ref):
    return jnp.concatenate([ref[sl] for sl in range(SLABS)], axis=-1)


def _s5_chunk(u_ref, bcat_ref, lam_ref, cre_ref, cim_ref, scratch, lc, nb, reverse):
    st_ref, bre_ref, bim_ref = scratch
    gb = nb // S5_SPLIT
    rows_g = lc * gb
    u3 = _unslab(u_ref).reshape(lc, nb, S5_WIDTH)
    order = range(lc - 1, -1, -1) if reverse else range(lc)
    ys = []
    us = [u3[:, gb * g:gb * (g + 1)].reshape(rows_g, S5_WIDTH).astype(BF16) for g in range(S5_SPLIT)]
    for g, u in enumerate(us):
        for c in range(S5_COLS):
            bu = _dot(u[:, 128 * c:128 * (c + 1)], bcat_ref[c])
            bre_ref[g, :, 512 * c:512 * (c + 1)] = bu[:, :512]
            bim_ref[g, :, 512 * c:512 * (c + 1)] = bu[:, 512:]
    for g in range(S5_SPLIT):
        bs = slice(gb * g, gb * (g + 1))
        for c in range(S5_COLS):
            cs = slice(512 * c, 512 * (c + 1))
            lre = jnp.broadcast_to(lam_ref[0:1, cs], (gb, 512))
            lim = jnp.broadcast_to(lam_ref[1:2, cs], (gb, 512))
            sre, sim = st_ref[0, bs, cs], st_ref[1, bs, cs]
            for t in order:
                rs = slice(t * gb, (t + 1) * gb)
                nre = lre * sre - lim * sim + bre_ref[g, rs, cs]
                nim = lre * sim + lim * sre + bim_ref[g, rs, cs]
                bre_ref[g, rs, cs] = nre
                bim_ref[g, rs, cs] = nim
                sre, sim = nre, nim
            st_ref[0, bs, cs] = sre
            st_ref[1, bs, cs] = sim
        yg = []
        for c in range(S5_OUT_COLS):
            cs = slice(1024 * c, 1024 * (c + 1))
            yg.append(_dot(bre_ref[g, :, cs].astype(BF16), cre_ref[c])
                      - _dot(bim_ref[g, :, cs].astype(BF16), cim_ref[c]))
        ys.append(jnp.concatenate(yg, axis=-1).reshape(lc, gb, S5_WIDTH))
    return jnp.concatenate(ys, axis=1).reshape(lc * nb, S5_WIDTH)


def _s5_fwd_kernel(u_ref, bcat_ref, lam_ref, cre_ref, cim_ref, y_ref, *scratch, lc, nb):
    @pl.when(pl.program_id(0) == 0)
    def _():
        scratch[0][...] = jnp.zeros(scratch[0].shape, F32)

    y_ref[...] = _s5_chunk(u_ref, bcat_ref, lam_ref, cre_ref, cim_ref, scratch, lc, nb, False)


def _s5_bwd_kernel(u_ref, yf_ref, bcat_ref, lam_ref, cre_ref, cim_ref, d_ref, wglu_ref, bglu_ref, o_ref,
                   oslab_ref, *scratch, lc, nb):
    @pl.when(pl.program_id(0) == 0)
    def _():
        scratch[0][...] = jnp.zeros(scratch[0].shape, F32)

    yb = _s5_chunk(u_ref, bcat_ref, lam_ref, cre_ref, cim_ref, scratch, lc, nb, True)
    y = d_ref[...] * _unslab(u_ref) + yf_ref[...] + yb
    g = jax.nn.gelu(y)
    out = g * jax.nn.sigmoid(_dot(g.astype(BF16), wglu_ref[...]) + bglu_ref[...])
    for sl in range(SLABS):
        oslab_ref[sl] = out[:, 128 * sl:128 * (sl + 1)]
    for j in range(nb):
        o_ref[j] = jnp.concatenate([oslab_ref[sl, pl.ds(j, lc, stride=nb), :] for sl in range(SLABS)],
                                   axis=-1).astype(BF16)


def _s5_mixer(u_tm, nb, bcat, lam, cre, cim, d_skip, w_glu, b_glu):
    rows_total = u_tm.shape[1]
    lc = S5_LC
    rows = lc * nb
    n_steps = rows_total // rows
    n_ctx = CTX_LEN // lc
    scratch = [pltpu.VMEM((2, nb, S5_CH), F32), pltpu.VMEM((S5_SPLIT, rows // S5_SPLIT, S5_CH), F32),
               pltpu.VMEM((S5_SPLIT, rows // S5_SPLIT, S5_CH), F32)]
    fwd_blk = pl.BlockSpec((rows, S5_WIDTH), lambda i: (i, 0))
    fwd_slab = pl.BlockSpec((SLABS, rows, 128), lambda i: (0, i, 0))
    yf = pl.pallas_call(
        functools.partial(_s5_fwd_kernel, lc=lc, nb=nb),
        grid=(n_steps,),
        in_specs=[fwd_slab] + [pl.BlockSpec((None,) + a.shape[1:], lambda i, n=a.ndim: (0,) * n)
                              for a in (bcat, lam, cre, cim)],
        out_specs=fwd_blk,
        out_shape=jax.ShapeDtypeStruct((rows_total, S5_WIDTH), F32),
        scratch_shapes=scratch,
        compiler_params=_cparams(("arbitrary",)),
        name="s5_fwd",
    )(u_tm, bcat, lam, cre, cim)

    def chunk(i):
        return jnp.where(i < n_ctx, n_ctx - 1 - i, n_steps - 1 - (i - n_ctx))

    bwd_blk = pl.BlockSpec((rows, S5_WIDTH), lambda i: (chunk(i), 0))
    bwd_slab = pl.BlockSpec((SLABS, rows, 128), lambda i: (0, chunk(i), 0))
    whole = lambda a: pl.BlockSpec(a.shape, lambda i, n=a.ndim: (0,) * n)
    return pl.pallas_call(
        functools.partial(_s5_bwd_kernel, lc=lc, nb=nb),
        grid=(n_steps,),
        in_specs=[bwd_slab, bwd_blk] + [pl.BlockSpec((None,) + a.shape[1:], lambda i, n=a.ndim: (1,) + (0,) * (n - 1))
                                       for a in (bcat, lam, cre, cim)]
                 + [whole(d_skip), whole(w_glu), whole(b_glu)],
        out_specs=pl.BlockSpec((nb, lc, S5_WIDTH), lambda i: (0, chunk(i), 0)),
        out_shape=jax.ShapeDtypeStruct((nb, rows_total // nb, S5_WIDTH), BF16),
        scratch_shapes=[pltpu.VMEM((SLABS, rows, 128), F32)] + scratch,
        compiler_params=_cparams(("arbitrary",)),
        name="s5_bwd_glu",
    )(u_tm, yf, bcat, lam, cre, cim, d_skip, w_glu, b_glu)


def _attn_kernel(q_ref, k_ref, vt_ref, o_ref, *, heads, rep, dq, dv, n_ctx_keys, ctx_tile):
    def run(nk):
        nb = nk // KEY_SPLIT
        kb = [slice(nb * i, nb * (i + 1)) for i in range(KEY_SPLIT)]
        units = [(j, h) for j in range(BP) for h in range(heads)]

        def scores(j, h):
            g = h // rep
            q = q_ref[j, :, dq * h:dq * (h + 1)]
            return [_dot_nt(k_ref[j, r, dq * g:dq * (g + 1)], q) for r in kb]

        outs = [[] for _ in range(BP)]
        pending = [scores(*u) for u in units[:SCORE_AHEAD]]
        for i, (j, h) in enumerate(units):
            ss = pending.pop(0)
            if i + SCORE_AHEAD < len(units):
                pending.append(scores(*units[i + SCORE_AHEAD]))
            m = functools.reduce(jnp.maximum, [jnp.max(s, axis=0, keepdims=True) for s in ss])
            ps = [jnp.exp2(s - m) for s in ss]
            l = sum(jnp.sum(p, axis=0, keepdims=True) for p in ps)
            g = h // rep
            ot = sum(_dot(vt_ref[j, dv * g:dv * (g + 1), r], p.astype(BF16)) for r, p in zip(kb, ps))
            outs[j].append((ot / l).T)
        for j in range(BP):
            o_ref[j] = jnp.concatenate(outs[j], axis=-1).astype(o_ref.dtype)

    nk_all = k_ref.shape[1]
    if ctx_tile:
        qi = pl.program_id(1)

        @pl.when(qi == 0)
        def _():
            run(n_ctx_keys)

        @pl.when(qi > 0)
        def _():
            run(nk_all)
    else:
        run(nk_all)


def _attention(q, k, vt, *, heads, kv_heads, dq, dv, q_tile0, n_q_tiles, out_rows):
    b, nt, _ = q.shape
    return pl.pallas_call(
        functools.partial(_attn_kernel, heads=heads, rep=heads // kv_heads, dq=dq, dv=dv,
                          n_ctx_keys=CTX_LEN, ctx_tile=(q_tile0 == 0)),
        grid=(b // BP, n_q_tiles),
        in_specs=[pl.BlockSpec((BP, TM, heads * dq), lambda bi, qi: (bi, qi + q_tile0, 0)),
                  pl.BlockSpec((BP, nt, kv_heads * dq), lambda bi, qi: (bi, 0, 0)),
                  pl.BlockSpec((BP, kv_heads * dv, nt), lambda bi, qi: (bi, 0, 0))],
        out_specs=pl.BlockSpec((BP, TM, heads * dv), lambda bi, qi: (bi, qi, 0)),
        out_shape=jax.ShapeDtypeStruct((b, out_rows, heads * dv), BF16),
        compiler_params=_cparams(("parallel", "arbitrary")),
        name="attention",
    )(q, k, vt)


def _mix_ffn_kernel(*refs, dual, first_tiles, last_tiles, final):
    if dual:
        x_ref, ctx_ref, *refs = refs
    else:
        (x_ref, *refs), ctx_ref = refs, None
    (hp_ref, hq_ref, a_ref, ap_ref, aq_ref, b_ref, bprev_ref, bnext_ref, m_ref, wa_ref, wb_ref,
     wup_ref, cw_ref, cb_ref, wdn_ref, gfin_ref, o_ref, la_ref, lb_ref, h1_ref, xe_ref, acc_ref) = refs
    t = pl.program_id(0)
    has_prev = functools.reduce(jnp.logical_and, [t != ft for ft in first_tiles])
    has_next = functools.reduce(jnp.logical_and, [t != lt for lt in last_tiles])
    ext = TM + 2 * HALO
    main, prev, nxt_ = slice(0, TM), slice(TM, TM + HALO), slice(TM + HALO, ext)
    la_ref[main], la_ref[prev], la_ref[nxt_] = a_ref[0], ap_ref[0], aq_ref[0]
    lb_ref[main], lb_ref[prev], lb_ref[nxt_] = b_ref[0], bprev_ref[0], bnext_ref[0]
    m = m_ref[0, 0]
    h_main = _stream_rows(t, x_ref, ctx_ref, 0) if dual else x_ref[0]
    for rs in (slice(0, ext // 2), slice(ext // 2, ext)):
        h1_ref[rs] = _dot(la_ref[rs], wa_ref[...]) + _dot(lb_ref[rs], wb_ref[...])
    for r, h_in in ((main, h_main), (prev, hp_ref[0]), (nxt_, hq_ref[0])):
        h1 = h_in + m[2:3] * h1_ref[r]
        h1_ref[r] = h1
        xe_ref[r] = (_rms(h1) * (1.0 + m[4:5]) + m[3:4]).astype(BF16)

    xe = xe_ref[...]
    x = xe[:TM]
    row = lax.broadcasted_iota(jnp.int32, (TM, FFN_CHUNK), 0)
    f = wdn_ref.shape[0]
    n_chunks = f // FFN_CHUNK
    cols = lambda c: slice(FFN_CHUNK * c, FFN_CHUNK * (c + 1))

    def up(c):
        return (_dot(xe, wup_ref[:, cols(c)]),
                _dot(x, wup_ref[:, f + FFN_CHUNK * c:f + FFN_CHUNK * (c + 1)]))

    pending = up(0)
    for c in range(n_chunks):
        ae, g = pending
        if c + 1 < n_chunks:
            pending = up(c + 1)
        a = ae[:TM]
        a_prev = jnp.where(has_prev, ae[TM + HALO - 1:TM + HALO], 0.0)
        a_next = jnp.where(has_next, ae[TM + HALO:TM + HALO + 1], 0.0)
        a_dn = jnp.where(row == 0, a_prev, pltpu.roll(a, 1, 0))
        a_up = jnp.where(row == TM - 1, a_next, pltpu.roll(a, TM - 1, 0))
        cw = cw_ref[:, cols(c)]
        conv = cb_ref[:, cols(c)] + a_dn * cw[0:1] + a * cw[1:2] + a_up * cw[2:3]
        part = _dot((jax.nn.gelu(conv) * g).astype(BF16), wdn_ref[cols(c), :])
        if c == 0:
            acc_ref[...] = part
        else:
            acc_ref[...] += part
    out = h1_ref[main] + m[5:6] * acc_ref[...]
    if final:
        out = _rms(out) * gfin_ref[...]
    o_ref[0] = out


def _mix_ffn(hs, a, bmix, msel, wa, wb, wup, cw, cb, wdn, gfin, *, dual, n_tiles,
             first_tiles, last_tiles, seg_of_tile, final):
    b, _, d = hs[0].shape
    hb = TM // HALO
    ext = TM + 2 * HALO
    c2 = lambda t, bi: (0, 0)
    single = dict(pipeline_mode=pl.Buffered(1))
    prev_blk = lambda t: jnp.maximum(t * hb - 1, 0)
    next_blk = lambda t, n_rows: jnp.minimum((t + 1) * hb, n_rows // HALO - 1)
    if dual:
        n_x = hs[0].shape[1]
        lat = lambda t: jnp.maximum(t - 1, 0)
        h_specs = [pl.BlockSpec((1, TM, d), lambda t, bi: (bi, lat(t), 0)),
                   pl.BlockSpec((1, CTX_LEN, d), lambda t, bi: (jnp.where(t == 0, bi, 0), 0, 0)),
                   pl.BlockSpec((1, HALO, d), lambda t, bi: (bi, prev_blk(lat(t)), 0)),
                   pl.BlockSpec((1, HALO, d), lambda t, bi: (bi, next_blk(lat(t), n_x), 0))]
        h_args = [hs[0], hs[1], hs[0], hs[0]]
    else:
        n_h = hs[0].shape[1]
        h_specs = [pl.BlockSpec((1, TM, d), lambda t, bi: (bi, t + 1, 0)),
                   pl.BlockSpec((1, HALO, d), lambda t, bi: (bi, prev_blk(t + 1), 0)),
                   pl.BlockSpec((1, HALO, d), lambda t, bi: (bi, next_blk(t + 1, n_h), 0))]
        h_args = [hs[0], hs[0], hs[0]]

    def mixer_specs(arr):
        w, n_rows = arr.shape[2], arr.shape[1]
        return [pl.BlockSpec((1, TM, w), lambda t, bi: (bi, t, 0)),
                pl.BlockSpec((1, HALO, w), lambda t, bi: (bi, prev_blk(t), 0)),
                pl.BlockSpec((1, HALO, w), lambda t, bi: (bi, next_blk(t, n_rows), 0))]

    return pl.pallas_call(
        functools.partial(_mix_ffn_kernel, dual=dual, first_tiles=first_tiles, last_tiles=last_tiles, final=final),
        grid=(n_tiles, b),
        in_specs=h_specs + mixer_specs(a) + mixer_specs(bmix)
                 + [pl.BlockSpec((1, 1, 6, d), lambda t, bi: (bi, seg_of_tile(t), 0, 0)),
                    pl.BlockSpec(wa.shape, c2, **single),
                    pl.BlockSpec(wb.shape, c2, **single),
                    pl.BlockSpec(wup.shape, c2, **single),
                    pl.BlockSpec(cw.shape, c2, **single),
                    pl.BlockSpec(cb.shape, c2, **single),
                    pl.BlockSpec(wdn.shape, c2, **single),
                    pl.BlockSpec(gfin.shape, c2)],
        out_specs=pl.BlockSpec((1, TM, d), lambda t, bi: (bi, t, 0)),
        out_shape=jax.ShapeDtypeStruct((b, n_tiles * TM, d), F32),
        scratch_shapes=[pltpu.VMEM((ext, wa.shape[0]), BF16), pltpu.VMEM((ext, wb.shape[0]), BF16),
                        pltpu.VMEM((ext, d), F32), pltpu.VMEM((ext, d), BF16), pltpu.VMEM((TM, d), F32)],
        compiler_params=_cparams(("parallel", "parallel")),
        name="mix_ffn",
    )(*h_args, a, a, a, bmix, bmix, bmix, msel, wa, wb, wup, cw, cb, wdn, gfin)


def _proj_cd_kernel(h_ref, m_ref, win_ref, wgvt_ref, ggq_ref, ggk_ref, cos_ref, sa_ref, sb_ref,
                    rq_ref, rk_ref, rv_ref, rg_ref, gq_ref, gk_ref, gvt_ref):
    k_scale = RET_D ** -0.5
    n_ret = 4 * RET_HEADS * RET_D
    rope = functools.partial(_rope, cos=cos_ref[...], sa=sa_ref[...], sb=sb_ref[...], quarter=RET_D // 4)
    head = lambda z, base, h: z[:, base + 128 * h:base + 128 * (h + 1)]
    ms = [m_ref[j, 0] for j in range(BP)]
    xns = [(_rms(h_ref[j]) * (1.0 + m[1:2]) + m[0:1]).astype(BF16) for j, m in enumerate(ms)]
    z_gqa = [_dot(xn, win_ref[:, n_ret:]) for xn in xns]
    z_rqk = [_dot(xn, win_ref[:, :n_ret // 2]) for xn in xns]
    z_rvg = [_dot(xn, win_ref[:, n_ret // 2:n_ret]) for xn in xns]
    gvts = [_dot_nt(wgvt_ref[...], xn) for xn in xns]
    for j, z in enumerate(z_gqa):
        gq_ref[j] = jnp.concatenate([rope(_rms(head(z, 0, h)) * ggq_ref[...]) * GQA_QSCALE
                                     for h in range(GQA_HEADS)], -1).astype(BF16)
        gk_ref[j] = jnp.concatenate([rope(_rms(head(z, 512, h)) * ggk_ref[...]) for h in range(GQA_KV_HEADS)],
                                    -1).astype(BF16)
    for j, z in enumerate(z_rqk):
        rq_ref[j] = jnp.concatenate([rope(head(z, 0, h)) for h in range(RET_HEADS)], -1).astype(BF16)
        rk_ref[j] = jnp.concatenate([rope(head(z, 512, h)) * k_scale for h in range(RET_HEADS)], -1).astype(BF16)
    for j, (z, gvt) in enumerate(zip(z_rvg, gvts)):
        rv_ref[j] = z[:, :512].astype(BF16)
        rg_ref[j] = z[:, 512:]
        gvt_ref[j] = gvt.astype(BF16)


def _proj_cd(h, msel, win, wgvt, ggq, ggk, tabs):
    b, nt, d = h.shape
    const = lambda bi, t: (0, 0)
    row = lambda bi, t: (t, 0)
    tok = lambda bi, t: (bi, t, 0)
    widths = (512, 512, 512, 512, 512, 256)
    dtypes = (BF16, BF16, BF16, F32, BF16, BF16)
    n_gv = wgvt.shape[0]
    return pl.pallas_call(
        _proj_cd_kernel,
        grid=(b // BP, nt // TM),
        in_specs=[pl.BlockSpec((BP, TM, d), tok),
                  pl.BlockSpec((BP, 1, 6, d), lambda bi, t: (bi, jnp.minimum(t, 1), 0, 0)),
                  pl.BlockSpec(win.shape, const),
                  pl.BlockSpec(wgvt.shape, const),
                  pl.BlockSpec(ggq.shape, const),
                  pl.BlockSpec(ggk.shape, const),
                  pl.BlockSpec((TM, 128), row),
                  pl.BlockSpec((TM, 128), row),
                  pl.BlockSpec((TM, 128), row)],
        out_specs=[pl.BlockSpec((BP, TM, w), tok) for w in widths]
                  + [pl.BlockSpec((BP, n_gv, TM), lambda bi, t: (bi, 0, t))],
        out_shape=[jax.ShapeDtypeStruct((b, nt, w), dt) for w, dt in zip(widths, dtypes)]
                  + [jax.ShapeDtypeStruct((b, n_gv, nt), BF16)],
        compiler_params=_cparams(("parallel", "parallel")),
        name="proj_cd",
    )(h, msel, win, wgvt, ggq, ggk, *tabs)


def _retention_kernel(q_ref, k_ref, v_ref, g_ref, dl_ref, o_ref, *, n_lat):
    c = RET_CHUNK
    nc = n_lat // c
    tdot = lambda a, b: lax.dot_general(a, b, (((0,), (0,)), ((), ())), preferred_element_type=F32)
    chunk = lambda n: slice(CTX_LEN + c * n, CTX_LEN + c * (n + 1))
    mc = lax.broadcasted_iota(jnp.int32, (CTX_LEN, RET_D), 0).astype(F32)
    i = lax.broadcasted_iota(jnp.int32, (c, RET_D), 0).astype(F32)
    ii = lax.broadcasted_iota(jnp.int32, (c, c), 0)
    jj = lax.broadcasted_iota(jnp.int32, (c, c), 1)
    dist = (ii - jj).astype(F32)
    for hh in range(RET_HP):
        hs = slice(RET_D * hh, RET_D * (hh + 1))
        k = lambda rows: k_ref[0, rows, hs]
        v = lambda rows: v_ref[0, rows, hs]
        lg = jax.nn.log_sigmoid(dl_ref[hh])
        lgf, lgb = lg[0:1], lg[1:2]

        kc = k(slice(0, CTX_LEN)).astype(F32)
        vc = v(slice(0, CTX_LEN))
        s_f = tdot((kc * jnp.exp((CTX_LEN - 1.0 - mc) * lgf)).astype(BF16), vc)
        s_b = tdot((kc * jnp.exp(mc * lgb)).astype(BF16), vc)

        kd_f = jnp.exp((c - 1.0 - i) * lgf)
        kd_b = jnp.exp(i * lgb)
        qd_f = jnp.exp((i + 1.0) * lgf)
        qd_b = jnp.exp((c - i) * lgb)
        cd_f = jnp.exp(c * lgf)
        cd_b = jnp.exp(c * lgb)
        lgf_c = jnp.concatenate([lgf] * (c // RET_D), axis=-1)
        lgb_c = jnp.concatenate([lgb] * (c // RET_D), axis=-1)
        dec = (jnp.where(ii >= jj, jnp.exp(jnp.maximum(dist, 0.0) * lgf_c), 0.0)
               + jnp.where(ii <= jj, jnp.exp(jnp.maximum(-dist, 0.0) * lgb_c), 0.0))

        sf = [s_f]
        for n in range(nc - 1):
            sf.append(cd_f * sf[n] + tdot((k(chunk(n)).astype(F32) * kd_f).astype(BF16), v(chunk(n))))
        sb = [None] * nc
        sb[nc - 1] = s_b
        for n in range(nc - 1, 0, -1):
            sb[n - 1] = cd_b * sb[n] + tdot((k(chunk(n)).astype(F32) * kd_b).astype(BF16), v(chunk(n)))

        for n in range(nc):
            qn = q_ref[0, chunk(n), hs]
            att = _dot_nt(qn, k(chunk(n))) * dec
            o = (_dot(att.astype(BF16), v(chunk(n)))
                 + _dot(qn, sf[n].astype(BF16)) * qd_f
                 + _dot(qn, sb[n].astype(BF16)) * qd_b)
            gate = g_ref[0, chunk(n), hs]
            o_ref[0, c * n:c * (n + 1), hs] = (_rms(o) * (gate * jax.nn.sigmoid(gate))).astype(BF16)


def _retention(rq, rk, rv, rg, decay_logit, n_lat):
    b, nt, _ = rq.shape
    dl = jnp.broadcast_to(jnp.swapaxes(decay_logit, 0, 1)[:, :, None], (RET_HEADS, 2, 128))
    blk = pl.BlockSpec((1, nt, RET_HP * RET_D), lambda bi, hp: (bi, 0, hp))
    return pl.pallas_call(
        functools.partial(_retention_kernel, n_lat=n_lat),
        grid=(b, RET_HEADS // RET_HP),
        in_specs=[blk, blk, blk, blk, pl.BlockSpec((RET_HP, 2, 128), lambda bi, hp: (hp, 0, 0))],
        out_specs=pl.BlockSpec((1, n_lat, RET_HP * RET_D), lambda bi, hp: (bi, 0, hp)),
        out_shape=jax.ShapeDtypeStruct((b, n_lat, RET_HEADS * RET_D), BF16),
        compiler_params=_cparams(("parallel", "parallel")),
        name="retention",
    )(rq, rk, rv, rg, dl)


def kernel(x, c, ctx, c_ctx, w_mod, b_mod, w_in_ab, w_out_ab, s5_lam_re, s5_lam_im, s5_log_step, s5_b_re, s5_b_im, s5_c_re, s5_c_im, s5_d, s5_w_glu, s5_b_glu, mla_g_q, mla_w_uq, mla_g_kv, mla_w_ukv, w_in_cd, w_out_cd, ret_decay_logit, gqa_g_q, gqa_g_k, ffn_w_up, ffn_conv_w, ffn_conv_b, ffn_w_down, g_final):
    b, n_lat, d = x.shape
    nt = CTX_LEN + n_lat
    n_tiles = nt // TM

    rows = ((b + 1 + 7) // 8) * 8
    cond = jnp.concatenate([c, c_ctx[None], jnp.zeros((rows - b - 1, d), F32)], axis=0)
    mods = _modulation(cond, w_mod, b_mod).reshape(w_mod.shape[0], rows, 6, d)

    def mod_select(layer):
        mctx = jnp.broadcast_to(mods[layer, b][None], (b, 6, d))
        return jnp.stack([mctx, mods[layer, :b]], axis=1)

    msel = mod_select(0)
    win = jnp.pad(w_in_ab[0], ((0, 0), (0, 64))).astype(BF16)
    wuq = jnp.pad(mla_w_uq[0].reshape(MLA_Q_RANK, MLA_HEADS, MLA_NOPE + MLA_ROPE),
                  ((0, 0), (0, 0), (0, MLA_QPAD - MLA_NOPE - MLA_ROPE))).reshape(MLA_Q_RANK, -1).astype(BF16)
    wukv = mla_w_ukv[0].reshape(MLA_KV_RANK, MLA_HEADS, MLA_NOPE + MLA_V)
    wkn = wukv[:, :, :MLA_NOPE].reshape(MLA_KV_RANK, -1).astype(BF16)
    wvt = jnp.transpose(wukv[:, :, MLA_NOPE:].reshape(MLA_KV_RANK, -1)).astype(BF16)
    tabs64 = _rope_tables(n_lat, MLA_ROPE)
    u_tm, q, k, vt = _proj_ab(x, ctx, msel, win, mla_g_q[0][None], wuq, mla_g_kv[0][None], wkn, wvt, tabs64)

    lam_re, lam_im, bbt_re, bbt_im = _s5_prep(s5_lam_re[0], s5_lam_im[0], s5_log_step[0], s5_b_re[0], s5_b_im[0])
    lam = jnp.stack([lam_re.reshape(2, S5_CH), lam_im.reshape(2, S5_CH)], axis=1)
    bshape = (2, S5_COLS, 8, S5_GROUP, S5_STATE)
    bcat = jnp.concatenate([_block_diag(bbt_re.reshape(bshape)), _block_diag(bbt_im.reshape(bshape))],
                           axis=-1).astype(BF16)
    cshape = (2, S5_OUT_COLS, S5_GROUPS // S5_OUT_COLS, S5_GROUP, S5_STATE)
    cre = _block_diag(jnp.swapaxes(s5_c_re[0].reshape(cshape), -1, -2)).astype(BF16)
    cim = _block_diag(jnp.swapaxes(s5_c_im[0].reshape(cshape), -1, -2)).astype(BF16)
    s5_out = _s5_mixer(u_tm, b, bcat, lam, cre, cim,
                       s5_d[0].reshape(1, S5_WIDTH), s5_w_glu[0].astype(BF16), s5_b_glu[0][None])

    att = _attention(q, k, vt, heads=MLA_HEADS, kv_heads=MLA_HEADS, dq=MLA_QPAD, dv=MLA_V,
                     q_tile0=0, n_q_tiles=n_tiles, out_rows=nt)
    wo = w_out_ab[0].astype(BF16)
    seg0 = lambda t: jnp.minimum(t, 1)
    gfin = g_final[None]
    h = _mix_ffn((x, ctx), s5_out, att, msel, wo[:S5_WIDTH], wo[S5_WIDTH:], ffn_w_up[0].astype(BF16),
                 ffn_conv_w[0], ffn_conv_b[0][None], ffn_w_down[0].astype(BF16), gfin,
                 dual=True, n_tiles=n_tiles, first_tiles=(0, 1),
                 last_tiles=(0, n_tiles - 1), seg_of_tile=seg0, final=False)

    msel = mod_select(1)
    tabs128 = _rope_tables(n_lat, RET_D)
    n_gv = GQA_KV_HEADS * GQA_D
    wcd = w_in_cd[0]
    rq, rk, rv, rg, gq, gk, gvt = _proj_cd(h, msel, wcd[:, :-n_gv].astype(BF16),
                                           jnp.transpose(wcd[:, -n_gv:]).astype(BF16),
                                           gqa_g_q[0][None], gqa_g_k[0][None], tabs128)
    ret = _retention(rq, rk, rv, rg, ret_decay_logit[0], n_lat)
    att = _attention(gq, gk, gvt, heads=GQA_HEADS, kv_heads=GQA_KV_HEADS, dq=GQA_D, dv=GQA_D,
                     q_tile0=1, n_q_tiles=n_tiles - 1, out_rows=n_lat)
    wo = w_out_cd[0].astype(BF16)
    n_half = RET_HEADS * RET_D
    return _mix_ffn((h,), ret, att, msel, wo[:n_half], wo[n_half:], ffn_w_up[1].astype(BF16),
                    ffn_conv_w[1], ffn_conv_b[1][None], ffn_w_down[1].astype(BF16), gfin,
                    dual=False, n_tiles=n_tiles - 1, first_tiles=(0,),
                    last_tiles=(n_tiles - 2,), seg_of_tile=lambda t: 1, final=True)
```
